```python
import math
import jax, jax.numpy as jnp
from jax import lax
import numpy as np

D_MODEL = 1024
BATCH = 4
SEQ = 4096
DEPTH = 4
DEC_BATCH = 32
DEC_SEQ = 16
PAST_LEN = 4096

CHUNK = 64
N_META = 16
Q_BLOCK = 128
EPS = 1e-6
D_FF = 2816

MLA_HEADS = 4
MLA_NOPE = 64
MLA_ROPE = 32
MLA_QK = MLA_NOPE + MLA_ROPE
MLA_V = 64
MLA_Q_LORA = 256
MLA_KV_LORA = 128
ROPE_THETA = 10000.0
LRU_HEADS = 4
LRU_WIDTH = 256
LRU_HEAD_DIM = LRU_WIDTH // LRU_HEADS
LRU_C = 8.0
CONV_W = 4
GDN_HEADS = 4
GDN_DK = 64
GDN_DV = 64
GDN_CHUNK = 64
GDN_QKV = GDN_HEADS * (2 * GDN_DK + GDN_DV)
RWKV_HEADS = 4
RWKV_HD = 64
RWKV_WIDTH = RWKV_HEADS * RWKV_HD
RWKV_DECAY_LORA = 64
RWKV_A_LORA = 64
RWKV_G_LORA = 128
RWKV_IN = 3 * RWKV_WIDTH + RWKV_DECAY_LORA + RWKV_A_LORA + RWKV_G_LORA
RWKV_GN_EPS = 64e-5

A_IN = MLA_Q_LORA + MLA_KV_LORA + MLA_ROPE
B_IN = 2 * LRU_WIDTH
C_IN = GDN_QKV + GDN_HEADS * GDN_DV + 2 * GDN_HEADS
IN_WIDTH = A_IN + B_IN + C_IN + RWKV_IN
MIX_WIDTH = MLA_HEADS * MLA_V + LRU_WIDTH + GDN_HEADS * GDN_DV + RWKV_WIDTH

kernel_name = "hybrid_streaming_encoder_step"

F32 = jnp.float32


def rms_norm(x, g):
    xf = x.astype(F32)
    y = xf * lax.rsqrt(jnp.mean(xf * xf, axis=-1, keepdims=True) + EPS)
    return (y * g.astype(F32)).astype(x.dtype)


def l2_normalize(x):
    xf = x.astype(F32)
    return xf * lax.rsqrt(jnp.sum(xf * xf, axis=-1, keepdims=True) + EPS)


def swiglu(x, w1, w2):
    gate, up = jnp.split(x @ w1, 2, axis=-1)
    return (jax.nn.silu(gate) * up) @ w2


def causal_conv(x, buf, w):
    T = x.shape[1]
    xp = jnp.concatenate([buf.astype(x.dtype), x], axis=1)
    y = sum(xp[:, j:j + T] * w[j] for j in range(CONV_W))
    return y, xp[:, -(CONV_W - 1):]


def rope_cos_sin(pos):
    inv = ROPE_THETA ** (-jnp.arange(0, MLA_ROPE, 2, dtype=F32) / MLA_ROPE)
    ang = pos.astype(F32)[:, None] * inv[None, :]
    return jnp.cos(ang), jnp.sin(ang)


def apply_rope(x, cos, sin):
    x1, x2 = jnp.split(x.astype(F32), 2, axis=-1)
    return jnp.concatenate([x1 * cos - x2 * sin, x1 * sin + x2 * cos], axis=-1).astype(x.dtype)


def mla_qkv(cols, pos, lp):
    B, T, _ = cols.shape
    c_q, c_kv, k_pe = jnp.split(cols, [MLA_Q_LORA, MLA_Q_LORA + MLA_KV_LORA], axis=-1)
    cos, sin = rope_cos_sin(pos)
    q = (rms_norm(c_q, lp["mla_q_a_norm"]) @ lp["mla_w_uq"]).reshape(B, T, MLA_HEADS, MLA_QK)
    q = jnp.concatenate([q[..., :MLA_NOPE], apply_rope(q[..., MLA_NOPE:], cos[:, None], sin[:, None])], axis=-1)
    q = rms_norm(q, lp["mla_q_norm"])
    ckv = rms_norm(c_kv, lp["mla_kv_a_norm"])
    kpe = apply_rope(k_pe, cos, sin)
    return q, ckv, kpe


def mla_keys(ckv, kpe, lp):
    B, S, _ = ckv.shape
    kv = (ckv @ lp["mla_w_ukv"]).reshape(B, S, MLA_HEADS, MLA_NOPE + MLA_V)
    k = jnp.concatenate([kv[..., :MLA_NOPE], jnp.broadcast_to(kpe[:, :, None, :], (B, S, MLA_HEADS, MLA_ROPE))], axis=-1)
    return rms_norm(k, lp["mla_k_norm"]), kv[..., MLA_NOPE:]


def attend(q, k, v, mask):
    s = jnp.einsum("bqhe,bshe->bhqs", q.astype(F32), k.astype(F32)) * (1.0 / math.sqrt(MLA_QK))
    if mask is not None:
        s = jnp.where(mask, s, -1e30)
    p = jax.nn.softmax(s, axis=-1)
    return jnp.einsum("bhqs,bshd->bqhd", p, v.astype(F32)).astype(v.dtype)


def mla_prompt(cols, lp):
    B, L, _ = cols.shape
    q, ckv, kpe = mla_qkv(cols, jnp.arange(L), lp)
    k, v = mla_keys(ckv, kpe, lp)
    o_meta = attend(q[:, :N_META], k[:, :N_META], v[:, :N_META], None)
    n_frames = L - N_META
    nb = n_frames // Q_BLOCK
    frame_chunk = (jnp.arange(n_frames) // CHUNK).astype(jnp.int32)
    key_chunk = jnp.concatenate([jnp.full((N_META,), -1, jnp.int32), frame_chunk])
    qf = jnp.moveaxis(q[:, N_META:].reshape(B, nb, Q_BLOCK, MLA_HEADS, MLA_QK), 1, 0)
    cq = frame_chunk.reshape(nb, Q_BLOCK)

    def block(args):
        qb, cqb = args
        return attend(qb, k, v, key_chunk[None, :] <= cqb[:, None])

    o = lax.map(block, (qf, cq))
    o = jnp.moveaxis(o, 0, 1).reshape(B, n_frames, MLA_HEADS * MLA_V)
    return jnp.concatenate([o_meta.reshape(B, N_META, MLA_HEADS * MLA_V), o], axis=1), ckv, kpe


def mla_sample(cols, cache_ckv, cache_kpe, lp):
    B, T, _ = cols.shape
    past = cache_ckv.shape[1]
    q, ckv, kpe = mla_qkv(cols, past + jnp.arange(T), lp)
    k, v = mla_keys(jnp.concatenate([cache_ckv.astype(ckv.dtype), ckv], axis=1),
                    jnp.concatenate([cache_kpe.astype(kpe.dtype), kpe], axis=1), lp)
    o = attend(q, k, v, None)
    return o.reshape(B, T, MLA_HEADS * MLA_V), ckv, kpe


def linear_scan(a, b, h0):
    b = b.at[:, 0].add(a[:, 0] * h0)

    def comb(l, r):
        return l[0] * r[0], r[0] * l[1] + r[1]

    _, h = lax.associative_scan(comb, (a, b), axis=1)
    return h


def rglru(cols, conv_buf, h0, lp):
    B, T, _ = cols.shape
    xb, gb = jnp.split(cols, 2, axis=-1)
    xc, new_buf = causal_conv(xb, conv_buf, lp["lru_conv_w"])
    xc = xc + lp["lru_conv_b"]
    xh = xc.reshape(B, T, LRU_HEADS, LRU_HEAD_DIM)
    r = jax.nn.sigmoid(jnp.einsum("bthi,hij->bthj", xh, lp["lru_wa"]).reshape(B, T, LRU_WIDTH) + lp["lru_ba"])
    i = jax.nn.sigmoid(jnp.einsum("bthi,hij->bthj", xh, lp["lru_wx"]).reshape(B, T, LRU_WIDTH) + lp["lru_bx"])
    log_a = -LRU_C * r.astype(F32) * jax.nn.softplus(-lp["lru_lambda"].astype(F32))
    a = jnp.exp(log_a)
    b = jnp.sqrt(-jnp.expm1(2.0 * log_a)) * (i * xc).astype(F32)
    h = linear_scan(a, b, h0.astype(F32))
    y = h.astype(cols.dtype) * jax.nn.gelu(gb)
    return y, new_buf, h[:, -1]


def chunk_gated_delta(q, k, v, g, beta, S0):
    B, T, H, K = k.shape
    C = GDN_CHUNK
    N = -(-T // C)
    pad = N * C - T

    def prep(x):
        x = jnp.pad(x, [(0, 0), (0, pad)] + [(0, 0)] * (x.ndim - 2))
        x = jnp.moveaxis(x, 2, 1)
        return x.reshape((B, H, N, C) + x.shape[3:])

    q, k, v, g, beta = (prep(t) for t in (q, k, v, g, beta))
    gc = jnp.cumsum(g, axis=-1)
    tril = jnp.tril(jnp.ones((C, C), bool))
    strict = jnp.tril(jnp.ones((C, C), bool), -1)
    diff = gc[..., :, None] - gc[..., None, :]
    decay = jnp.where(tril, jnp.exp(jnp.where(tril, diff, 0.0)), 0.0)
    kb = k * beta[..., None]
    lmat = jnp.where(strict, jnp.einsum("bhnik,bhnjk->bhnij", kb, k) * decay, 0.0)
    eye = jnp.eye(C, dtype=F32)
    tinv = lax.linalg.triangular_solve(eye + lmat, jnp.broadcast_to(eye, lmat.shape),
                                       left_side=True, lower=True, unit_diagonal=True)
    u = tinv @ (v * beta[..., None])
    w = tinv @ (kb * jnp.exp(gc)[..., None])
    attn = jnp.einsum("bhnik,bhnjk->bhnij", q, k) * decay

    def step(S, xs):
        qc, kc, gcc, uc, wc, ac = xs
        v_new = uc - wc @ S
        o = (qc * jnp.exp(gcc)[..., None]) @ S + ac @ v_new
        glast = gcc[..., -1:]
        S = S * jnp.exp(glast)[..., None] + jnp.swapaxes(kc * jnp.exp(glast - gcc)[..., None], -1, -2) @ v_new
        return S, o

    xs = tuple(jnp.moveaxis(t, 2, 0) for t in (q, k, gc, u, w, attn))
    S, o = lax.scan(step, S0, xs)
    o = jnp.moveaxis(o, 0, 2).reshape(B, H, N * C, -1)[:, :, :T]
    return jnp.moveaxis(o, 1, 2), S


def gated_delta(cols, conv_buf, S0, lp):
    B, T, _ = cols.shape
    qkv, z, a_in, b_in = jnp.split(cols, [GDN_QKV, GDN_QKV + GDN_HEADS * GDN_DV,
                                          GDN_QKV + GDN_HEADS * GDN_DV + GDN_HEADS], axis=-1)
    qkv, new_buf = causal_conv(qkv, conv_buf, lp["gdn_conv_w"])
    qkv = jax.nn.silu(qkv)
    q, k, v = jnp.split(qkv, [GDN_HEADS * GDN_DK, 2 * GDN_HEADS * GDN_DK], axis=-1)
    q = l2_normalize(q.reshape(B, T, GDN_HEADS, GDN_DK)) * (GDN_DK ** -0.5)
    k = l2_normalize(k.reshape(B, T, GDN_HEADS, GDN_DK))
    v = v.reshape(B, T, GDN_HEADS, GDN_DV).astype(F32)
    beta = jax.nn.sigmoid(b_in.astype(F32))
    g = -jnp.exp(lp["gdn_a_log"].astype(F32)) * jax.nn.softplus(a_in.astype(F32) + lp["gdn_dt_bias"].astype(F32))
    o, S = chunk_gated_delta(q, k, v, g, beta, S0.astype(F32))
    o = rms_norm(o, lp["gdn_o_norm"]) * jax.nn.silu(z.reshape(B, T, GDN_HEADS, GDN_DV).astype(F32))
    return o.reshape(B, T, GDN_HEADS * GDN_DV).astype(cols.dtype), new_buf, S


def rwkv7(cols, prev, S0, lp):
    B, T, _ = cols.shape
    shifted = jnp.concatenate([prev[:, None].astype(cols.dtype), cols[:, :-1]], axis=1)
    xm = cols + (shifted - cols) * lp["rwkv_mu"]
    W = RWKV_WIDTH
    r, k, v, w_lo, a_lo, g_lo = jnp.split(
        xm, [W, 2 * W, 3 * W, 3 * W + RWKV_DECAY_LORA, 3 * W + RWKV_DECAY_LORA + RWKV_A_LORA], axis=-1)
    hs = lambda t: t.reshape(B, T, RWKV_HEADS, RWKV_HD)
    hp = lambda p: p.reshape(RWKV_HEADS, RWKV_HD).astype(F32)
    w = jnp.exp(-0.606531 * jax.nn.sigmoid((lp["rwkv_w0"] + jnp.tanh(w_lo) @ lp["rwkv_w_b"]).astype(F32)))
    a = hs(jax.nn.sigmoid((lp["rwkv_a0"] + a_lo @ lp["rwkv_a_b"]).astype(F32)))
    g = jax.nn.sigmoid(g_lo) @ lp["rwkv_g_b"]
    kk = l2_normalize(hs(k * lp["rwkv_k_k"]))
    k = hs(k).astype(F32) * (1.0 + (a - 1.0) * hp(lp["rwkv_k_a"]))
    r = hs(r).astype(F32)
    v = hs(v).astype(F32)
    w = hs(w)

    def step(S, xs):
        r_t, w_t, k_t, v_t, kk_t, a_t = xs
        sa = jnp.einsum("bhvk,bhk->bhv", S, -kk_t)
        S = S * w_t[:, :, None, :] + sa[..., None] * (kk_t * a_t)[:, :, None, :] + v_t[..., None] * k_t[:, :, None, :]
        return S, jnp.einsum("bhvk,bhk->bhv", S, r_t)

    xs = tuple(jnp.moveaxis(t, 1, 0) for t in (r, w, k, v, kk, a))
    S, o = lax.scan(step, S0.astype(F32), xs)
    o = jnp.moveaxis(o, 0, 1)
    mu = jnp.mean(o, axis=-1, keepdims=True)
    var = jnp.mean(jnp.square(o - mu), axis=-1, keepdims=True)
    o = (o - mu) * lax.rsqrt(var + RWKV_GN_EPS) * hp(lp["rwkv_ln_w"]) + hp(lp["rwkv_ln_b"])
    o = o + jnp.sum(r * k * lp["rwkv_r_k"].astype(F32), axis=-1, keepdims=True) * v
    y = o.reshape(B, T, W) * g.astype(F32)
    return y.astype(cols.dtype), cols[:, -1], S


def token_mixing(h, lp, st, sample):
    cols = h @ lp["w_in"]
    ca, cb, cc, cd = jnp.split(cols, [A_IN, A_IN + B_IN, A_IN + B_IN + C_IN], axis=-1)
    if sample:
        ya, ckv, kpe = mla_sample(ca, st["mla_ckv"], st["mla_kpe"], lp)
    else:
        ya, ckv, kpe = mla_prompt(ca, lp)
    yb, lru_conv, lru_h = rglru(cb, st["lru_conv"], st["lru_h"], lp)
    yc, gdn_conv, gdn_s = gated_delta(cc, st["gdn_conv"], st["gdn_s"], lp)
    yd, shift, rwkv_s = rwkv7(cd, st["rwkv_shift"], st["rwkv_s"], lp)
    y = jnp.concatenate([ya, yb, yc, yd], axis=-1) @ lp["w_out"]
    return y, (ckv, kpe, lru_conv, lru_h, gdn_conv, gdn_s, shift, rwkv_s)


def run_trunk(x, layers, states, sample, final_norm):
    new = []
    for l in range(DEPTH):
        lp = {name: w[l] for name, w in layers.items()}
        st = {name: s[l] for name, s in states.items()}
        x = x + 0.5 * swiglu(rms_norm(x, lp["norm_ffn1"]), lp["ffn1_w1"], lp["ffn1_w2"])
        y, ns = token_mixing(rms_norm(x, lp["norm_mix"]), lp, st, sample)
        x = x + y
        x = x + 0.5 * swiglu(rms_norm(x, lp["norm_ffn2"]), lp["ffn2_w1"], lp["ffn2_w2"])
        new.append(ns)
    stacked = [jnp.stack(t) for t in zip(*new)]
    return rms_norm(x, final_norm), stacked


def setup_inputs(seed: int = 0) -> dict:
    key = jax.random.key(seed)
    ks = iter(jax.random.split(key, 64))

    def nrm(shape, scale=1.0):
        return scale * jax.random.normal(next(ks), shape, F32)

    def gain(shape):
        return 1.0 + 0.05 * jax.random.normal(next(ks), shape, F32)

    def unif(shape, lo, hi):
        return jax.random.uniform(next(ks), shape, F32, lo, hi)

    L = DEPTH
    inp = {}
    inp["x_prompt"] = nrm((BATCH, SEQ, D_MODEL))
    inp["x_sample"] = nrm((DEC_BATCH, DEC_SEQ, D_MODEL))
    inp["cache_mla_ckv"] = nrm((L, DEC_BATCH, PAST_LEN, MLA_KV_LORA))
    inp["cache_mla_kpe"] = nrm((L, DEC_BATCH, PAST_LEN, MLA_ROPE))
    inp["state_lru_conv"] = nrm((L, DEC_BATCH, CONV_W - 1, LRU_WIDTH))
    inp["state_lru_h"] = nrm((L, DEC_BATCH, LRU_WIDTH), 0.5)
    inp["state_gdn_conv"] = nrm((L, DEC_BATCH, CONV_W - 1, GDN_QKV))
    inp["state_gdn_s"] = nrm((L, DEC_BATCH, GDN_HEADS, GDN_DK, GDN_DV), 0.3)
    inp["state_rwkv_shift"] = nrm((L, DEC_BATCH, RWKV_IN))
    inp["state_rwkv_s"] = nrm((L, DEC_BATCH, RWKV_HEADS, RWKV_HD, RWKV_HD), 0.3)
    inp["meta_tokens"] = nrm((N_META, D_MODEL))
    inp["norm_ffn1"] = gain((L, D_MODEL))
    inp["ffn1_w1"] = nrm((L, D_MODEL, 2 * D_FF), D_MODEL ** -0.5)
    inp["ffn1_w2"] = nrm((L, D_FF, D_MODEL), D_FF ** -0.5)
    inp["norm_mix"] = gain((L, D_MODEL))
    inp["w_in"] = nrm((L, D_MODEL, IN_WIDTH), D_MODEL ** -0.5)
    inp["mla_q_a_norm"] = gain((L, MLA_Q_LORA))
    inp["mla_w_uq"] = nrm((L, MLA_Q_LORA, MLA_HEADS * MLA_QK), MLA_Q_LORA ** -0.5)
    inp["mla_kv_a_norm"] = gain((L, MLA_KV_LORA))
    inp["mla_w_ukv"] = nrm((L, MLA_KV_LORA, MLA_HEADS * (MLA_NOPE + MLA_V)), MLA_KV_LORA ** -0.5)
    inp["mla_q_norm"] = gain((L, MLA_QK))
    inp["mla_k_norm"] = gain((L, MLA_QK))
    inp["lru_conv_w"] = nrm((L, CONV_W, LRU_WIDTH), CONV_W ** -0.5)
    inp["lru_conv_b"] = nrm((L, LRU_WIDTH), 0.05)
    inp["lru_wa"] = nrm((L, LRU_HEADS, LRU_HEAD_DIM, LRU_HEAD_DIM), LRU_HEAD_DIM ** -0.5)
    inp["lru_ba"] = nrm((L, LRU_WIDTH), 0.1)
    inp["lru_wx"] = nrm((L, LRU_HEADS, LRU_HEAD_DIM, LRU_HEAD_DIM), LRU_HEAD_DIM ** -0.5)
    inp["lru_bx"] = nrm((L, LRU_WIDTH), 0.1)
    s = unif((L, LRU_WIDTH), 0.9, 0.999) ** (1.0 / LRU_C)
    inp["lru_lambda"] = jnp.log(s) - jnp.log1p(-s)
    inp["gdn_conv_w"] = nrm((L, CONV_W, GDN_QKV), CONV_W ** -0.5)
    inp["gdn_a_log"] = jnp.log(unif((L, GDN_HEADS), 1.0, 16.0))
    dt = jnp.exp(unif((L, GDN_HEADS), math.log(1e-3), math.log(1e-1)))
    inp["gdn_dt_bias"] = dt + jnp.log(-jnp.expm1(-dt))
    inp["gdn_o_norm"] = gain((L, GDN_DV))
    inp["rwkv_mu"] = unif((L, RWKV_IN), 0.0, 1.0)
    inp["rwkv_w0"] = nrm((L, RWKV_WIDTH), 0.5)
    inp["rwkv_w_b"] = nrm((L, RWKV_DECAY_LORA, RWKV_WIDTH), 0.5 * RWKV_DECAY_LORA ** -0.5)
    inp["rwkv_a0"] = nrm((L, RWKV_WIDTH), 0.1)
    inp["rwkv_a_b"] = nrm((L, RWKV_A_LORA, RWKV_WIDTH), 0.5 * RWKV_A_LORA ** -0.5)
    inp["rwkv_g_b"] = nrm((L, RWKV_G_LORA, RWKV_WIDTH), RWKV_G_LORA ** -0.5)
    inp["rwkv_k_k"] = gain((L, RWKV_WIDTH))
    inp["rwkv_k_a"] = gain((L, RWKV_WIDTH))
    inp["rwkv_r_k"] = nrm((L, RWKV_HEADS, RWKV_HD), 0.1)
    inp["rwkv_ln_w"] = gain((L, RWKV_WIDTH))
    inp["rwkv_ln_b"] = nrm((L, RWKV_WIDTH), 0.02)
    inp["w_out"] = nrm((L, MIX_WIDTH, D_MODEL), MIX_WIDTH ** -0.5)
    inp["norm_ffn2"] = gain((L, D_MODEL))
    inp["ffn2_w1"] = nrm((L, D_MODEL, 2 * D_FF), D_MODEL ** -0.5)
    inp["ffn2_w2"] = nrm((L, D_FF, D_MODEL), D_FF ** -0.5)
    inp["final_norm"] = gain((D_MODEL,))
    return inp


def reference(x_prompt, x_sample, cache_mla_ckv, cache_mla_kpe, state_lru_conv, state_lru_h,
              state_gdn_conv, state_gdn_s, state_rwkv_shift, state_rwkv_s, meta_tokens,
              norm_ffn1, ffn1_w1, ffn1_w2, norm_mix, w_in, mla_q_a_norm, mla_w_uq, mla_kv_a_norm,
              mla_w_ukv, mla_q_norm, mla_k_norm, lru_conv_w, lru_conv_b, lru_wa, lru_ba, lru_wx, lru_bx,
              lru_lambda, gdn_conv_w, gdn_a_log, gdn_dt_bias, gdn_o_norm, rwkv_mu, rwkv_w0, rwkv_w_b,
              rwkv_a0, rwkv_a_b, rwkv_g_b, rwkv_k_k, rwkv_k_a, rwkv_r_k, rwkv_ln_w, rwkv_ln_b, w_out,
              norm_ffn2, ffn2_w1, ffn2_w2, final_norm):
    layers = dict(norm_ffn1=norm_ffn1, ffn1_w1=ffn1_w1, ffn1_w2=ffn1_w2, norm_mix=norm_mix, w_in=w_in,
                  mla_q_a_norm=mla_q_a_norm, mla_w_uq=mla_w_uq, mla_kv_a_norm=mla_kv_a_norm,
                  mla_w_ukv=mla_w_ukv, mla_q_norm=mla_q_norm, mla_k_norm=mla_k_norm,
                  lru_conv_w=lru_conv_w, lru_conv_b=lru_conv_b, lru_wa=lru_wa, lru_ba=lru_ba,
                  lru_wx=lru_wx, lru_bx=lru_bx, lru_lambda=lru_lambda, gdn_conv_w=gdn_conv_w,
                  gdn_a_log=gdn_a_log, gdn_dt_bias=gdn_dt_bias, gdn_o_norm=gdn_o_norm, rwkv_mu=rwkv_mu,
                  rwkv_w0=rwkv_w0, rwkv_w_b=rwkv_w_b, rwkv_a0=rwkv_a0, rwkv_a_b=rwkv_a_b, rwkv_g_b=rwkv_g_b,
                  rwkv_k_k=rwkv_k_k, rwkv_k_a=rwkv_k_a, rwkv_r_k=rwkv_r_k, rwkv_ln_w=rwkv_ln_w,
                  rwkv_ln_b=rwkv_ln_b, w_out=w_out, norm_ffn2=norm_ffn2, ffn2_w1=ffn2_w1, ffn2_w2=ffn2_w2)
    B = x_prompt.shape[0]
    dt = x_prompt.dtype
    zero_states = dict(
        lru_conv=jnp.zeros((DEPTH, B, CONV_W - 1, LRU_WIDTH), dt),
        lru_h=jnp.zeros((DEPTH, B, LRU_WIDTH), F32),
        gdn_conv=jnp.zeros((DEPTH, B, CONV_W - 1, GDN_QKV), dt),
        gdn_s=jnp.zeros((DEPTH, B, GDN_HEADS, GDN_DK, GDN_DV), F32),
        rwkv_shift=jnp.zeros((DEPTH, B, RWKV_IN), dt),
        rwkv_s=jnp.zeros((DEPTH, B, RWKV_HEADS, RWKV_HD, RWKV_HD), F32))
    x0 = jnp.concatenate([jnp.broadcast_to(meta_tokens.astype(dt)[None], (B, N_META, D_MODEL)), x_prompt], axis=1)
    yp, p_new = run_trunk(x0, layers, zero_states, False, final_norm)
    sample_states = dict(mla_ckv=cache_mla_ckv, mla_kpe=cache_mla_kpe, lru_conv=state_lru_conv,
                         lru_h=state_lru_h, gdn_conv=state_gdn_conv, gdn_s=state_gdn_s,
                         rwkv_shift=state_rwkv_shift, rwkv_s=state_rwkv_s)
    ys, s_new = run_trunk(x_sample, layers, sample_states, True, final_norm)
    p_ckv, p_kpe, p_lru_conv, p_lru_h, p_gdn_conv, p_gdn_s, p_rwkv_shift, p_rwkv_s = p_new
    s_ckv, s_kpe, s_lru_conv, s_lru_h, s_gdn_conv, s_gdn_s, s_rwkv_shift, s_rwkv_s = s_new
    return (yp[:, N_META:], ys, p_ckv, p_kpe, p_lru_conv, p_lru_h, p_gdn_conv, p_gdn_s, p_rwkv_shift, p_rwkv_s,
            s_ckv, s_kpe, s_lru_conv, s_lru_h, s_gdn_conv, s_gdn_s, s_rwkv_shift, s_rwkv_s)
```

```python
import functools
import math

import jax
import jax.numpy as jnp
import numpy as np
from jax import lax
from jax.experimental import pallas as pl
from jax.experimental.pallas import tpu as pltpu

F32 = jnp.float32
BF16 = jnp.bfloat16
HIGHEST = lax.Precision.HIGHEST

EPS = 1e-6
N_META = 16
CHUNK = 64
CONV_W = 4
HEADS = 4
HD = 64
MIXW = HEADS * HD
MLA_NOPE = 64
MLA_ROPE = 32
MLA_QK = MLA_NOPE + MLA_ROPE
MLA_PAD = 128
ROPE_THETA = 10000.0
LRU_C = 8.0
RWKV_GN_EPS = 64e-5
RWKV_DECAY_SCALE = -0.606531

VMEM_LIMIT = 56 * 1024 * 1024

A_W = 640
B_W = 512
C_W = 1536
D_W = 1024


def _bdot(a, b):
    return jnp.dot(a.astype(BF16), b.astype(BF16), preferred_element_type=F32)


def _bdot_nt(a, b):
    return lax.dot_general(a.astype(BF16), b.astype(BF16), (((1,), (1,)), ((), ())),
                           preferred_element_type=F32)


def _bdot_tn(a, b):
    return lax.dot_general(a.astype(BF16), b.astype(BF16), (((0,), (0,)), ((), ())),
                           preferred_element_type=F32)


def _hdot(a, b):
    return jnp.dot(a, b, precision=HIGHEST, preferred_element_type=F32)


def _split3(x):
    x1 = x.astype(BF16)
    r1 = x - x1.astype(F32)
    x2 = r1.astype(BF16)
    r2 = r1 - x2.astype(F32)
    return x1, x2, r2.astype(BF16)


def _xdot_r01(x, m01):
    x1, x2, x3 = _split3(x)
    d = functools.partial(jnp.dot, preferred_element_type=F32)
    return d(x1, m01) + d(x2, m01) + d(x3, m01)


def _xdot_l01(m01, x):
    x1, x2, x3 = _split3(x)
    d = functools.partial(jnp.dot, preferred_element_type=F32)
    return d(m01, x1) + d(m01, x2) + d(m01, x3)


def _sigmoid(x):
    return 1.0 / (1.0 + jnp.exp(-x))


def _silu(x):
    return x * _sigmoid(x)


def _softplus(x):
    return jnp.maximum(x, 0.0) + jnp.log1p(jnp.exp(-jnp.abs(x)))


def _gelu_tanh(x):
    return 0.5 * x * (1.0 + jnp.tanh(0.7978845608028654 * (x + 0.044715 * (x * x * x))))


def _rms_rows(x, g):
    return x * lax.rsqrt(jnp.mean(x * x, axis=-1, keepdims=True) + EPS) * g


def _block_diag_mask():
    r = lax.broadcasted_iota(jnp.int32, (MIXW, MIXW), 0) >> 6
    c = lax.broadcasted_iota(jnp.int32, (MIXW, MIXW), 1) >> 6
    return r == c


def _embed(x, bd):
    c = x.shape[0]
    if c < HD:
        x = jnp.concatenate([x, jnp.zeros((HD - c, MIXW), x.dtype)], axis=0)
    return jnp.where(bd, jnp.concatenate([x] * HEADS, axis=0), 0.0)


def _head_masks(c):
    row = lax.broadcasted_iota(jnp.int32, (c, MIXW), 0)
    col = lax.broadcasted_iota(jnp.int32, (c, MIXW), 1) & (HD - 1)
    return row >= col, row > col, row == col


def _tril01(c):
    r = lax.broadcasted_iota(jnp.int32, (c, c), 0)
    k = lax.broadcasted_iota(jnp.int32, (c, c), 1)
    return jnp.where(r >= k, 1.0, 0.0).astype(BF16)


def _inv_unit_lower(lm, eye4, bd):
    c = lm.shape[0]
    n = -lm
    t = jnp.where(eye4, 1.0, 0.0) + n
    p = n
    for _ in range(max(int(math.log2(c)) - 1, 0)):
        p = _hdot(p, _embed(p, bd))
        t = t + _hdot(t, _embed(p, bd))
    return t


def _shift_rows(x, s, fill):
    rolled = pltpu.roll(x, s, 0)
    row = lax.broadcasted_iota(jnp.int32, x.shape, 0)
    return jnp.where(row >= s, rolled, fill)


def _causal_conv(xext_ref, w_ref, tt):
    acc = None
    for j in range(CONV_W):
        term = xext_ref[pl.ds(8 - (CONV_W - 1) + j, tt), :] * w_ref[j:j + 1, :]
        acc = term if acc is None else acc + term
    return acc


def _const_spec(shape):
    nd = len(shape)
    return pl.BlockSpec(shape, lambda *_: (0,) * nd, pipeline_mode=pl.Buffered(1))


def _token_kernel(*refs, has_pre, has_post, has_final, d_ff, ff_chunk):
    it = iter(refs)
    x_ref = next(it)
    if has_pre:
        y_refs = [next(it) for _ in range(4)]
        wo_refs = [next(it) for _ in range(4)]
    gffn_ref, w1_ref, w2_ref = next(it), next(it), next(it)
    if has_post:
        gmix_ref, win_ref = next(it), next(it)
    if has_final:
        gfin_ref = next(it)
    xo_ref = next(it)
    if has_post:
        col_refs = [next(it) for _ in range(4)]

    x = x_ref[...]
    if has_pre:
        for y_ref, wo_ref in zip(y_refs, wo_refs):
            x = x + jnp.dot(y_ref[...], wo_ref[...], preferred_element_type=F32)
    h = _rms_rows(x, gffn_ref[...]).astype(BF16)
    acc = None
    for j in range(d_ff // ff_chunk):
        lo = j * ff_chunk
        gate = jnp.dot(h, w1_ref[:, lo:lo + ff_chunk], preferred_element_type=F32)
        up = jnp.dot(h, w1_ref[:, d_ff + lo:d_ff + lo + ff_chunk], preferred_element_type=F32)
        act = (_silu(gate) * up).astype(BF16)
        part = jnp.dot(act, w2_ref[lo:lo + ff_chunk, :], preferred_element_type=F32)
        acc = part if acc is None else acc + part
    x = x + 0.5 * acc
    if has_final:
        xo_ref[...] = _rms_rows(x, gfin_ref[...])
    else:
        xo_ref[...] = x
    if has_post:
        hm = _rms_rows(x, gmix_ref[...]).astype(BF16)
        off = 0
        for c_ref in col_refs:
            wd = c_ref.shape[-1]
            c_ref[...] = jnp.dot(hm, win_ref[:, off:off + wd], preferred_element_type=F32)
            off += wd


def _token_call(x, lw, *, pre=None, post=False, final_g=None, tm):
    n, d = x.shape
    d_ff = lw["w2"].shape[0]
    ff_chunk = d_ff // 2 if (d_ff // 2) % 128 == 0 else d_ff
    row = lambda w: pl.BlockSpec((tm, w), lambda i: (i, 0))
    args, specs = [x], [row(d)]
    if pre is not None:
        ys, wos = pre
        for y in ys:
            args.append(y)
            specs.append(row(y.shape[1]))
        for w in wos:
            args.append(w)
            specs.append(_const_spec(w.shape))
    for name in ("g_ffn", "w1", "w2"):
        args.append(lw[name])
        specs.append(_const_spec(lw[name].shape))
    if post:
        for name in ("g_mix", "w_in"):
            args.append(lw[name])
            specs.append(_const_spec(lw[name].shape))
    if final_g is not None:
        args.append(final_g)
        specs.append(_const_spec(final_g.shape))
    out_shape = [jax.ShapeDtypeStruct((n, d), F32)]
    out_specs = [row(d)]
    if post:
        for wd in (A_W, B_W, C_W, D_W):
            out_shape.append(jax.ShapeDtypeStruct((n, wd), F32))
            out_specs.append(row(wd))
    body = functools.partial(_token_kernel, has_pre=pre is not None, has_post=post,
                             has_final=final_g is not None, d_ff=d_ff, ff_chunk=ff_chunk)
    return pl.pallas_call(
        body, grid=(n // tm,), in_specs=specs, out_specs=out_specs, out_shape=out_shape,
        compiler_params=pltpu.CompilerParams(dimension_semantics=("parallel",),
                                             vmem_limit_bytes=VMEM_LIMIT),
        name="token_block")(*args)


def _mla_prep_kernel(ca_ref, cos_ref, sin_ref, gqa_ref, wq_ref, gkva_ref, wkv_ref, gq_ref, gk_ref,
                     q_ref, k_ref, v_ref, ckv_ref, kpe_ref):
    cs = cos_ref[...]
    sn = sin_ref[...]
    hw = HEADS * MLA_PAD
    cqn = _rms_rows(ca_ref[:, 0:256], gqa_ref[...])
    qq = _bdot(cqn, wq_ref[...])
    scale = 1.0 / math.sqrt(MLA_QK)
    for h in range(HEADS):
        lo = h * MLA_PAD
        blk = qq[:, lo:lo + MLA_PAD] * cs + qq[:, hw + lo:hw + lo + MLA_PAD] * sn
        ms = jnp.sum(blk * blk, axis=-1, keepdims=True) * (1.0 / MLA_QK)
        q_ref[:, lo:lo + MLA_PAD] = (blk * lax.rsqrt(ms + EPS) * (gq_ref[...] * scale)).astype(BF16)
    ckv = _rms_rows(ca_ref[:, 256:384], gkva_ref[...])
    ckv_ref[...] = ckv
    kpe = ca_ref[:, 384:512] * cs + ca_ref[:, 512:640] * sn
    kpe_ref[...] = kpe[:, 0:MLA_ROPE]
    kv = _bdot(ckv, wkv_ref[...])
    for h in range(HEADS):
        lo = h * MLA_PAD
        kb = kv[:, lo:lo + MLA_PAD] + kpe
        ms = jnp.sum(kb * kb, axis=-1, keepdims=True) * (1.0 / MLA_QK)
        k_ref[:, lo:lo + MLA_PAD] = (kb * lax.rsqrt(ms + EPS) * gk_ref[...]).astype(BF16)
        v_ref[:, lo:lo + MLA_PAD] = kv[:, hw + lo:hw + lo + MLA_PAD].astype(BF16)


def _mla_prep_call(ca, cos_t, sin_t, lw, *, tm):
    n = ca.shape[0]
    hw = HEADS * MLA_PAD
    row = lambda w: pl.BlockSpec((tm, w), lambda i: (i, 0))
    consts = [lw["mla_gqa"], lw["mla_wq"], lw["mla_gkva"], lw["mla_wkv"], lw["mla_gq"], lw["mla_gk"]]
    return pl.pallas_call(
        _mla_prep_kernel, grid=(n // tm,),
        in_specs=[row(A_W), row(MLA_PAD), row(MLA_PAD)] + [_const_spec(c.shape) for c in consts],
        out_specs=[row(hw), row(hw), row(hw), row(128), row(MLA_ROPE)],
        out_shape=[jax.ShapeDtypeStruct((n, hw), BF16)] * 3
        + [jax.ShapeDtypeStruct((n, 128), F32), jax.ShapeDtypeStruct((n, MLA_ROPE), F32)],
        compiler_params=pltpu.CompilerParams(dimension_semantics=("parallel",),
                                             vmem_limit_bytes=VMEM_LIMIT),
        name="mla_prep")(ca, cos_t, sin_t, *consts)


def _softmax_step(carry, s, vblk):
    m, l, acc = carry
    m_new = jnp.maximum(m, jnp.max(s, axis=-1, keepdims=True))
    alpha = jnp.exp(m - m_new)
    p = jnp.exp(s - m_new)
    l = alpha * l + jnp.sum(p, axis=-1, keepdims=True)
    acc = alpha * acc + jnp.dot(p.astype(BF16), vblk, preferred_element_type=F32)
    return m_new, l, acc


def _mla_prompt_kernel(q_ref, k_ref, v_ref, o_ref, *, tq, tk, n_kb):
    q0 = pl.program_id(1) * tq
    last_chunk = (q0 + tq - 1 - N_META) >> 6
    k_end = N_META + CHUNK * (last_chunk + 1)
    nkb = jnp.minimum((k_end + tk - 1) // tk, n_kb)
    qchunk = (q0 + lax.broadcasted_iota(jnp.int32, (tq, 1), 0) - N_META) >> 6
    for h in range(HEADS):
        lo = h * MLA_PAD
        qh = q_ref[:, lo:lo + MLA_PAD]

        def body(kb, carry, lo=lo, qh=qh):
            k0 = pl.multiple_of(kb * tk, tk)
            kblk = k_ref[pl.ds(k0, tk), lo:lo + MLA_PAD]
            vblk = v_ref[pl.ds(k0, tk), lo:lo + MLA_PAD]
            s = lax.dot_general(qh, kblk, (((1,), (1,)), ((), ())), preferred_element_type=F32)
            kchunk = (k0 + lax.broadcasted_iota(jnp.int32, (1, tk), 1) - N_META) >> 6
            s = jnp.where(kchunk <= qchunk, s, -1e30)
            return _softmax_step(carry, s, vblk)

        init = (jnp.full((tq, 1), -1e30, F32), jnp.zeros((tq, 1), F32), jnp.zeros((tq, MLA_PAD), F32))
        _, l, acc = lax.fori_loop(0, nkb, body, init)
        o_ref[:, lo:lo + MLA_PAD] = (acc / l).astype(BF16)


def _mla_prompt_call(q, k, v, *, tq, tk):
    b, lp, hw = q.shape
    body = functools.partial(_mla_prompt_kernel, tq=tq, tk=tk, n_kb=lp // tk)
    return pl.pallas_call(
        body, grid=(b, lp // tq),
        in_specs=[pl.BlockSpec((None, tq, hw), lambda i, j: (i, j, 0)),
                  pl.BlockSpec((None, lp, hw), lambda i, j: (i, 0, 0)),
                  pl.BlockSpec((None, lp, hw), lambda i, j: (i, 0, 0))],
        out_specs=pl.BlockSpec((None, tq, hw), lambda i, j: (i, j, 0)),
        out_shape=jax.ShapeDtypeStruct((b, lp, hw), BF16),
        compiler_params=pltpu.CompilerParams(dimension_semantics=("parallel", "arbitrary"),
                                             vmem_limit_bytes=VMEM_LIMIT),
        name="mla_prompt_attn")(q, k, v)


def _mla_sample_kernel(q_ref, kn_ref, vn_ref, cckv_ref, ckpe_ref, wkv_ref, gk_ref, o_ref, kpe_pad,
                       *, tk, past):
    hw = HEADS * MLA_PAD
    t = q_ref.shape[0]
    kpe_pad[...] = jnp.zeros(kpe_pad.shape, F32)
    kpe_pad[:, 0:MLA_ROPE] = ckpe_ref[...]
    qs = [q_ref[:, h * MLA_PAD:(h + 1) * MLA_PAD] for h in range(HEADS)]

    def body(c, carry):
        k0 = pl.multiple_of(c * tk, tk)
        kv = _bdot(cckv_ref[pl.ds(k0, tk), :], wkv_ref[...])
        kpe = kpe_pad[pl.ds(k0, tk), :]
        out = []
        for h in range(HEADS):
            lo = h * MLA_PAD
            kb = kv[:, lo:lo + MLA_PAD] + kpe
            ms = jnp.sum(kb * kb, axis=-1, keepdims=True) * (1.0 / MLA_QK)
            kn = (kb * lax.rsqrt(ms + EPS) * gk_ref[...]).astype(BF16)
            s = lax.dot_general(qs[h], kn, (((1,), (1,)), ((), ())), preferred_element_type=F32)
            out.append(_softmax_step(carry[h], s, kv[:, hw + lo:hw + lo + MLA_PAD].astype(BF16)))
        return tuple(out)

    init = tuple((jnp.full((t, 1), -1e30, F32), jnp.zeros((t, 1), F32), jnp.zeros((t, MLA_PAD), F32))
                 for _ in range(HEADS))
    carry = lax.fori_loop(0, past // tk, body, init)
    for h in range(HEADS):
        lo = h * MLA_PAD
        s = lax.dot_general(qs[h], kn_ref[:, lo:lo + MLA_PAD], (((1,), (1,)), ((), ())),
                            preferred_element_type=F32)
        _, l, acc = _softmax_step(carry[h], s, vn_ref[:, lo:lo + MLA_PAD])
        o_ref[:, lo:lo + MLA_PAD] = (acc / l).astype(BF16)


def _mla_sample_call(q, kn, vn, cache_ckv, cache_kpe, layer, lw, *, tk):
    b, t, hw = q.shape
    past = cache_ckv.shape[2]
    seq = lambda: pl.BlockSpec((None, t, hw), lambda i: (i, 0, 0))
    body = functools.partial(_mla_sample_kernel, tk=tk, past=past)
    return pl.pallas_call(
        body, grid=(b,),
        in_specs=[seq(), seq(), seq(),
                  pl.BlockSpec((None, None, past, cache_ckv.shape[3]), lambda i: (layer, i, 0, 0)),
                  pl.BlockSpec((None, None, past, cache_kpe.shape[3]), lambda i: (layer, i, 0, 0)),
                  _const_spec(lw["mla_wkv"].shape), _const_spec(lw["mla_gk"].shape)],
        out_specs=seq(),
        out_shape=jax.ShapeDtypeStruct((b, t, hw), BF16),
        scratch_shapes=[pltpu.VMEM((past, MLA_PAD), F32)],
        compiler_params=pltpu.CompilerParams(dimension_semantics=("arbitrary",),
                                             vmem_limit_bytes=VMEM_LIMIT),
        name="mla_sample_attn")(q, kn, vn, cache_ckv, cache_kpe, lw["mla_wkv"], lw["mla_gk"])


def _lru_kernel(cb_ref, buf_ref, h0_ref, cw_ref, cbias_ref, wa_ref, ba_ref, wx_ref, bx_ref, lam_ref,
                y_ref, nbuf_ref, hl_ref, xext, hcar, *, tt, t_valid):
    ti = pl.program_id(1)

    @pl.when(ti == 0)
    def _():
        xext[0:8, :] = buf_ref[...]
        hcar[...] = h0_ref[...]

    @pl.when(ti > 0)
    def _():
        xext[0:8, :] = xext[tt:tt + 8, :]

    xext[8:tt + 8, :] = cb_ref[:, 0:MIXW]
    xc = _causal_conv(xext, cw_ref, tt) + cbias_ref[...]
    r = _sigmoid(_bdot(xc, wa_ref[...]) + ba_ref[...])
    i = _sigmoid(_bdot(xc, wx_ref[...]) + bx_ref[...])
    log_a = -LRU_C * r * _softplus(-lam_ref[...])
    pos = ti * tt + lax.broadcasted_iota(jnp.int32, (tt, 1), 0)
    log_a = jnp.where(pos < t_valid, log_a, 0.0)
    a = jnp.exp(log_a)
    b = jnp.sqrt(-jnp.tanh(log_a) * (a * a + 1.0)) * (i * xc)
    s = 1
    while s < tt:
        b = a * _shift_rows(b, s, 0.0) + b
        a = a * _shift_rows(a, s, 1.0)
        s *= 2
    h = a * hcar[...] + b
    hcar[...] = h[tt - 1:tt, :]
    y_ref[...] = (h * _gelu_tanh(cb_ref[:, MIXW:2 * MIXW])).astype(BF16)

    last = t_valid - 1

    @pl.when(ti == last // tt)
    def _():
        r0 = last % tt
        hl_ref[...] = h[r0:r0 + 1, :]
        nbuf_ref[...] = xext[pl.ds(8 + r0 - (CONV_W - 2), CONV_W - 1), :]


def _lru_call(cb, buf8, h0, lw, *, tt, t_valid):
    b, lp, _ = cb.shape
    consts = [lw["lru_cw"], lw["lru_cb"], lw["lru_wa"], lw["lru_ba"], lw["lru_wx"], lw["lru_bx"],
              lw["lru_lam"]]
    body = functools.partial(_lru_kernel, tt=tt, t_valid=t_valid)
    per_b = lambda s: pl.BlockSpec((None,) + s, lambda i, j: (i, 0, 0))
    return pl.pallas_call(
        body, grid=(b, lp // tt),
        in_specs=[pl.BlockSpec((None, tt, B_W), lambda i, j: (i, j, 0)), per_b((8, MIXW)),
                  per_b((1, MIXW))] + [_const_spec(c.shape) for c in consts],
        out_specs=[pl.BlockSpec((None, tt, MIXW), lambda i, j: (i, j, 0)),
                   per_b((CONV_W - 1, MIXW)), per_b((1, MIXW))],
        out_shape=[jax.ShapeDtypeStruct((b, lp, MIXW), BF16),
                   jax.ShapeDtypeStruct((b, CONV_W - 1, MIXW), F32),
                   jax.ShapeDtypeStruct((b, 1, MIXW), F32)],
        scratch_shapes=[pltpu.VMEM((tt + 8, MIXW), F32), pltpu.VMEM((1, MIXW), F32)],
        compiler_params=pltpu.CompilerParams(dimension_semantics=("parallel", "arbitrary"),
                                             vmem_limit_bytes=VMEM_LIMIT),
        name="rglru")(cb, buf8, h0, *consts)


def _gdn_kernel(cc_ref, buf_ref, s0_ref, cw_ref, alog_ref, dtb_ref, gon_ref,
                y_ref, nbuf_ref, so_ref, xext, q_s, k_s, v_s, b_s, g_s, o_s,
                *, tt, c, t_valid):
    ti = pl.program_id(1)
    qkw = 3 * MIXW

    @pl.when(ti == 0)
    def _():
        xext[0:8, :] = buf_ref[...]
        so_ref[...] = s0_ref[...]

    @pl.when(ti > 0)
    def _():
        xext[0:8, :] = xext[tt:tt + 8, :]

    xext[8:tt + 8, :] = cc_ref[:, 0:qkw]
    xs = _silu(_causal_conv(xext, cw_ref, tt))
    bd = _block_diag_mask()
    bd01 = jnp.where(bd, 1.0, 0.0).astype(BF16)
    q = xs[:, 0:MIXW]
    k = xs[:, MIXW:2 * MIXW]
    q_s[...] = q * lax.rsqrt(_xdot_r01(q * q, bd01) + EPS) * (HD ** -0.5)
    k_s[...] = k * lax.rsqrt(_xdot_r01(k * k, bd01) + EPS)
    v_s[...] = xs[:, 2 * MIXW:3 * MIXW]
    pos = ti * tt + lax.broadcasted_iota(jnp.int32, (tt, 1), 0)
    valid = pos < t_valid
    a_in = cc_ref[:, qkw + MIXW:qkw + 2 * MIXW]
    b_in = cc_ref[:, qkw + 2 * MIXW:qkw + 3 * MIXW]
    b_s[...] = jnp.where(valid, _sigmoid(b_in), 0.0)
    g_s[...] = jnp.where(valid, -jnp.exp(alog_ref[...]) * _softplus(a_in + dtb_ref[...]), 0.0)

    tril4, strict4, eye4 = _head_masks(c)
    tril01 = _tril01(c)
    ones01 = jnp.ones((c, c), BF16)

    def chunk(ci, carry):
        r0 = pl.multiple_of(ci * c, c)
        rows = pl.ds(r0, c)
        qc, kc, vc, beta, g = q_s[rows, :], k_s[rows, :], v_s[rows, :], b_s[rows, :], g_s[rows, :]
        gc = _xdot_l01(tril01, g)
        gct = _xdot_l01(ones01, jnp.where(eye4, gc, 0.0))
        decay = jnp.where(tril4, jnp.exp(jnp.where(tril4, gc - gct, 0.0)), 0.0)
        kb = kc * beta
        kemb = _embed(kc, bd)
        lm = jnp.where(strict4, _bdot_nt(kb, kemb) * decay, 0.0)
        tinv = _inv_unit_lower(lm, eye4, bd)
        egc = jnp.exp(gc)
        u = _bdot(tinv, _embed(vc * beta, bd))
        w = _bdot(tinv, _embed(kb * egc, bd))
        attn = jnp.where(tril4, _bdot_nt(qc, kemb) * decay, 0.0)
        s_bd = so_ref[...]
        v_new = u - _bdot(w, s_bd)
        o_s[rows, :] = _bdot(qc * egc, s_bd) + _bdot(attn, _embed(v_new, bd))
        glast = gc[c - 1:c, :]
        upd = _bdot_tn(kc * jnp.exp(glast - gc), v_new)
        so_ref[...] = s_bd * jnp.exp(glast) + jnp.where(bd, upd, 0.0)
        return carry

    lax.fori_loop(0, tt // c, chunk, 0)

    o = o_s[...]
    ms = _xdot_r01(o * o, bd01) * (1.0 / HD)
    z = cc_ref[:, qkw:qkw + MIXW]
    y_ref[...] = (o * lax.rsqrt(ms + EPS) * gon_ref[...] * _silu(z)).astype(BF16)

    last = t_valid - 1

    @pl.when(ti == last // tt)
    def _():
        r0 = last % tt
        nbuf_ref[...] = xext[pl.ds(8 + r0 - (CONV_W - 2), CONV_W - 1), :]


def _gdn_call(cc, buf8, s0, lw, *, tt, c, t_valid):
    b, lp, _ = cc.shape
    qkw = 3 * MIXW
    consts = [lw["gdn_cw"], lw["gdn_alog"], lw["gdn_dtb"], lw["gdn_gon"]]
    body = functools.partial(_gdn_kernel, tt=tt, c=c, t_valid=t_valid)
    per_b = lambda s: pl.BlockSpec((None,) + s, lambda i, j: (i, 0, 0))
    return pl.pallas_call(
        body, grid=(b, lp // tt),
        in_specs=[pl.BlockSpec((None, tt, C_W), lambda i, j: (i, j, 0)), per_b((8, qkw)),
                  per_b((MIXW, MIXW))] + [_const_spec(x.shape) for x in consts],
        out_specs=[pl.BlockSpec((None, tt, MIXW), lambda i, j: (i, j, 0)),
                   per_b((CONV_W - 1, qkw)), per_b((MIXW, MIXW))],
        out_shape=[jax.ShapeDtypeStruct((b, lp, MIXW), BF16),
                   jax.ShapeDtypeStruct((b, CONV_W - 1, qkw), F32),
                   jax.ShapeDtypeStruct((b, MIXW, MIXW), F32)],
        scratch_shapes=[pltpu.VMEM((tt + 8, qkw), F32)] + [pltpu.VMEM((tt, MIXW), F32)] * 6,
        compiler_params=pltpu.CompilerParams(dimension_semantics=("parallel", "arbitrary"),
                                             vmem_limit_bytes=VMEM_LIMIT),
        name="gated_delta")(cc, buf8, s0, *consts)


def _rwkv_kernel(cd_ref, prev_ref, s0_ref, mu_ref, w0_ref, wb_ref, a0_ref, ab_ref, gb_ref, kk_ref,
                 ka_ref, rk_ref, lnw_ref, lnb_ref,
                 y_ref, shift_ref, so_ref, xext, r_s, k_s, v_s, n_s, b_s, w_s, o_s,
                 *, tt, c, t_valid):
    ti = pl.program_id(1)

    @pl.when(ti == 0)
    def _():
        xext[0:8, :] = prev_ref[...]
        so_ref[...] = s0_ref[...]

    @pl.when(ti > 0)
    def _():
        xext[0:8, :] = xext[tt:tt + 8, :]

    x = cd_ref[...]
    xext[8:tt + 8, :] = x
    xm = x + (xext[pl.ds(7, tt), :] - x) * mu_ref[...]
    pos = ti * tt + lax.broadcasted_iota(jnp.int32, (tt, 1), 0)
    valid = pos < t_valid
    bd = _block_diag_mask()
    bd01 = jnp.where(bd, 1.0, 0.0).astype(BF16)
    r = xm[:, 0:MIXW]
    k = xm[:, MIXW:2 * MIXW]
    v = xm[:, 2 * MIXW:3 * MIXW]
    lo_blk = xm[:, 3 * MIXW:3 * MIXW + 128]
    logw = RWKV_DECAY_SCALE * _sigmoid(w0_ref[...] + _bdot(jnp.tanh(lo_blk), wb_ref[...]))
    a = _sigmoid(a0_ref[...] + _bdot(lo_blk, ab_ref[...]))
    kkv = k * kk_ref[...]
    kkn = kkv * lax.rsqrt(_xdot_r01(kkv * kkv, bd01) + EPS)
    kkn = jnp.where(valid, kkn, 0.0)
    kmod = jnp.where(valid, k * (1.0 + (a - 1.0) * ka_ref[...]), 0.0)
    r_s[...] = r
    k_s[...] = kmod
    v_s[...] = v
    n_s[...] = kkn
    b_s[...] = kkn * a
    w_s[...] = jnp.where(valid, logw, 0.0)

    tril4, strict4, eye4 = _head_masks(c)
    tril01 = _tril01(c)

    def chunk(ci, carry):
        r0 = pl.multiple_of(ci * c, c)
        rows = pl.ds(r0, c)
        rc, kc, vc, nc, bc, lw = r_s[rows, :], k_s[rows, :], v_s[rows, :], n_s[rows, :], b_s[rows, :], w_s[rows, :]
        cum = _xdot_l01(tril01, lw)
        pin = jnp.exp(-cum)
        rd = rc * jnp.exp(cum)
        nd = nc * jnp.exp(cum - lw)
        kinv = kc * pin
        binv = bc * pin
        kiemb = _embed(kinv, bd)
        biemb = _embed(binv, bd)
        a_nb = jnp.where(strict4, _bdot_nt(nd, biemb), 0.0)
        a_nk = jnp.where(strict4, _bdot_nt(nd, kiemb), 0.0)
        a_rk = jnp.where(tril4, _bdot_nt(rd, kiemb), 0.0)
        a_rb = jnp.where(tril4, _bdot_nt(rd, biemb), 0.0)
        tinv = _inv_unit_lower(a_nb, eye4, bd)
        st = so_ref[...]
        vemb = _embed(vc, bd)
        xr = _bdot_nt(nd, st) + _bdot(a_nk, vemb)
        u = _bdot(tinv, _embed(xr, bd))
        o_s[rows, :] = _bdot_nt(rd, st) + _bdot(a_rk, vemb) - _bdot(a_rb, _embed(u, bd))
        pc = jnp.exp(cum[c - 1:c, :])
        upd = _bdot_tn(vc, kinv * pc) - _bdot_tn(u, binv * pc)
        so_ref[...] = st * pc + jnp.where(bd, upd, 0.0)
        return carry

    lax.fori_loop(0, tt // c, chunk, 0)

    o = o_s[...]
    mean = _xdot_r01(o, bd01) * (1.0 / HD)
    d = o - mean
    var = _xdot_r01(d * d, bd01) * (1.0 / HD)
    o = d * lax.rsqrt(var + RWKV_GN_EPS) * lnw_ref[...] + lnb_ref[...]
    kmod_all = k * (1.0 + (a - 1.0) * ka_ref[...])
    o = o + _xdot_r01(r * kmod_all * rk_ref[...], bd01) * v
    g = _bdot(_sigmoid(xm[:, 3 * MIXW + 128:3 * MIXW + 256]), gb_ref[...])
    y_ref[...] = (o * g).astype(BF16)

    last = t_valid - 1

    @pl.when(ti == last // tt)
    def _():
        shift_ref[...] = xext[pl.ds(8 + last % tt, 1), :]


def _rwkv_call(cd, prev8, s0, lw, *, tt, c, t_valid):
    b, lp, _ = cd.shape
    consts = [lw["rwkv_mu"], lw["rwkv_w0"], lw["rwkv_wb"], lw["rwkv_a0"], lw["rwkv_ab"], lw["rwkv_gb"],
              lw["rwkv_kk"], lw["rwkv_ka"], lw["rwkv_rk"], lw["rwkv_lnw"], lw["rwkv_lnb"]]
    body = functools.partial(_rwkv_kernel, tt=tt, c=c, t_valid=t_valid)
    per_b = lambda s: pl.BlockSpec((None,) + s, lambda i, j: (i, 0, 0))
    return pl.pallas_call(
        body, grid=(b, lp // tt),
        in_specs=[pl.BlockSpec((None, tt, D_W), lambda i, j: (i, j, 0)), per_b((8, D_W)),
                  per_b((MIXW, MIXW))] + [_const_spec(x.shape) for x in consts],
        out_specs=[pl.BlockSpec((None, tt, MIXW), lambda i, j: (i, j, 0)),
                   per_b((1, D_W)), per_b((MIXW, MIXW))],
        out_shape=[jax.ShapeDtypeStruct((b, lp, MIXW), BF16),
                   jax.ShapeDtypeStruct((b, 1, D_W), F32),
                   jax.ShapeDtypeStruct((b, MIXW, MIXW), F32)],
        scratch_shapes=[pltpu.VMEM((tt + 8, D_W), F32)] + [pltpu.VMEM((tt, MIXW), F32)] * 7,
        compiler_params=pltpu.CompilerParams(dimension_semantics=("parallel", "arbitrary"),
                                             vmem_limit_bytes=VMEM_LIMIT),
        name="rwkv7")(cd, prev8, s0, *consts)


def _pad_cols(w, width):
    return jnp.pad(w, ((0, 0), (0, width - w.shape[1])))


def _mla_head_cols(w, n_heads, per_head, pieces):
    blocks = []
    for h in range(n_heads):
        cols = [sign * w[:, h * per_head + lo:h * per_head + hi] for lo, hi, sign in pieces]
        blocks.append(_pad_cols(jnp.concatenate(cols, axis=1), MLA_PAD))
    return jnp.concatenate(blocks, axis=1)


def _block_diag_heads(w):
    out = jnp.zeros((MIXW, MIXW), w.dtype)
    for h in range(HEADS):
        out = out.at[h * HD:(h + 1) * HD, h * HD:(h + 1) * HD].set(w[h])
    return out


def _prep_layer_weights(p, l):
    half = MLA_ROPE // 2
    row = lambda v: v.reshape(1, -1).astype(F32)
    lw = {}
    for tag in ("1", "2"):
        lw["ffn" + tag] = dict(g_ffn=row(p["norm_ffn" + tag][l]), w1=p["ffn%s_w1" % tag][l].astype(BF16),
                               w2=p["ffn%s_w2" % tag][l].astype(BF16))
    w_in = p["w_in"][l]
    o = 0
    cq = w_in[:, o:o + 256]; o += 256
    ckv = w_in[:, o:o + 128]; o += 128
    kpe = w_in[:, o:o + MLA_ROPE]; o += MLA_ROPE
    xb_gb = w_in[:, o:o + 512]; o += 512
    qkvz = w_in[:, o:o + 1024]; o += 1024
    a_in = w_in[:, o:o + HEADS]; o += HEADS
    b_in = w_in[:, o:o + HEADS]; o += HEADS
    rw = w_in[:, o:o + 1024]
    kpe_rot = jnp.concatenate([-kpe[:, half:], kpe[:, :half]], axis=1)
    w_in_p = jnp.concatenate([
        cq, ckv, _pad_cols(kpe, 128), _pad_cols(kpe_rot, 128),
        xb_gb,
        qkvz, jnp.repeat(a_in, HD, axis=1), jnp.repeat(b_in, HD, axis=1),
        rw], axis=1)
    lw["ffn1"]["g_mix"] = row(p["norm_mix"][l])
    lw["ffn1"]["w_in"] = w_in_p.astype(BF16)
    w_out = p["w_out"][l]
    wo_a = jnp.concatenate([jnp.pad(w_out[h * HD:(h + 1) * HD], ((0, MLA_PAD - HD), (0, 0)))
                            for h in range(HEADS)], axis=0)
    lw["w_out"] = [wo_a.astype(BF16)] + [w_out[MIXW * i:MIXW * (i + 1)].astype(BF16) for i in (1, 2, 3)]
    wuq = p["mla_w_uq"][l]
    wq = _mla_head_cols(wuq, HEADS, MLA_QK, [(MLA_NOPE, MLA_QK, 1.0), (0, MLA_NOPE, 1.0)])
    wq_rot = _mla_head_cols(wuq, HEADS, MLA_QK, [(MLA_NOPE + half, MLA_QK, -1.0),
                                                (MLA_NOPE, MLA_NOPE + half, 1.0)])
    lw["mla_wq"] = jnp.concatenate([wq, wq_rot], axis=1).astype(BF16)
    wukv = p["mla_w_ukv"][l]
    zero_rope = jnp.zeros((wukv.shape[0], MLA_ROPE), wukv.dtype)
    wk = jnp.concatenate([_pad_cols(jnp.concatenate([zero_rope, wukv[:, h * 128:h * 128 + MLA_NOPE]], axis=1),
                                    MLA_PAD) for h in range(HEADS)], axis=1)
    wv = jnp.concatenate([_pad_cols(wukv[:, h * 128 + MLA_NOPE:(h + 1) * 128], MLA_PAD)
                          for h in range(HEADS)], axis=1)
    lw["mla_wkv"] = jnp.concatenate([wk, wv], axis=1).astype(BF16)
    lw["mla_gqa"] = row(p["mla_q_a_norm"][l])
    lw["mla_gkva"] = row(p["mla_kv_a_norm"][l])
    perm = lambda g: row(_pad_cols(jnp.concatenate([g[MLA_NOPE:], g[:MLA_NOPE]])[None], MLA_PAD))
    lw["mla_gq"] = perm(p["mla_q_norm"][l])
    lw["mla_gk"] = perm(p["mla_k_norm"][l])
    lw["lru_cw"] = p["lru_conv_w"][l]
    lw["lru_cb"] = row(p["lru_conv_b"][l])
    lw["lru_wa"] = _block_diag_heads(p["lru_wa"][l]).astype(BF16)
    lw["lru_ba"] = row(p["lru_ba"][l])
    lw["lru_wx"] = _block_diag_heads(p["lru_wx"][l]).astype(BF16)
    lw["lru_bx"] = row(p["lru_bx"][l])
    lw["lru_lam"] = row(p["lru_lambda"][l])
    lw["gdn_cw"] = p["gdn_conv_w"][l]
    lw["gdn_alog"] = row(jnp.repeat(p["gdn_a_log"][l], HD))
    lw["gdn_dtb"] = row(jnp.repeat(p["gdn_dt_bias"][l], HD))
    lw["gdn_gon"] = row(jnp.tile(p["gdn_o_norm"][l], HEADS))
    z64 = jnp.zeros((64, MIXW), F32)
    lw["rwkv_mu"] = row(p["rwkv_mu"][l])
    lw["rwkv_w0"] = row(p["rwkv_w0"][l])
    lw["rwkv_wb"] = jnp.concatenate([p["rwkv_w_b"][l], z64], axis=0).astype(BF16)
    lw["rwkv_a0"] = row(p["rwkv_a0"][l])
    lw["rwkv_ab"] = jnp.concatenate([z64, p["rwkv_a_b"][l]], axis=0).astype(BF16)
    lw["rwkv_gb"] = p["rwkv_g_b"][l].astype(BF16)
    lw["rwkv_kk"] = row(p["rwkv_k_k"][l])
    lw["rwkv_ka"] = row(p["rwkv_k_a"][l])
    lw["rwkv_rk"] = row(p["rwkv_r_k"][l])
    lw["rwkv_lnw"] = row(p["rwkv_ln_w"][l])
    lw["rwkv_lnb"] = row(p["rwkv_ln_b"][l])
    return lw


def _rope_tables(pos):
    inv = ROPE_THETA ** (-jnp.arange(0, MLA_ROPE, 2, dtype=F32) / MLA_ROPE)
    ang = pos.astype(F32)[:, None] * inv[None, :]
    cos, sin = jnp.cos(ang), jnp.sin(ang)
    n = pos.shape[0]
    cos_t = jnp.concatenate([cos, cos, jnp.ones((n, MLA_PAD - MLA_ROPE), F32)], axis=1)
    sin_t = jnp.concatenate([sin, sin, jnp.zeros((n, MLA_PAD - MLA_ROPE), F32)], axis=1)
    return cos_t, sin_t


def _embed_state(s):
    b = s.shape[0]
    out = jnp.zeros((b, MIXW, MIXW), F32)
    for h in range(HEADS):
        out = out.at[:, h * HD:(h + 1) * HD, h * HD:(h + 1) * HD].set(s[:, h].astype(F32))
    return out


def _extract_state(s_bd):
    return jnp.stack([s_bd[:, h * HD:(h + 1) * HD, h * HD:(h + 1) * HD] for h in range(HEADS)], axis=1)


def _hist8(rows):
    return jnp.pad(rows.astype(F32), ((0, 0), (8 - rows.shape[1], 0), (0, 0)))


def _run_group(x, lws, final_g, states, cfg):
    b, lp, d = x.shape
    n = b * lp
    t_valid, tm, tt, c = cfg["t_valid"], cfg["tm"], cfg["tt"], cfg["c"]
    cos_t, sin_t = cfg["rope"]
    xf = x.reshape(n, d)
    new_states = []
    depth = len(lws)
    for l in range(depth):
        lw = lws[l]
        st = states[l]
        xf, ca, cb, cc, cd = _token_call(xf, lw["ffn1"], post=True, tm=tm)
        q, k, v, ckv, kpe = _mla_prep_call(ca, cos_t, sin_t, lw, tm=tm)
        r3 = lambda a: a.reshape(b, lp, a.shape[-1])
        if st["mla"] is None:
            ya = _mla_prompt_call(r3(q), r3(k), r3(v), tq=cfg["tq"], tk=cfg["tk"])
        else:
            cache_ckv, cache_kpe = st["mla"]
            ya = _mla_sample_call(r3(q), r3(k), r3(v), cache_ckv, cache_kpe, l, lw, tk=cfg["tk"])
        yb, lru_conv, lru_h = _lru_call(r3(cb), st["lru_conv"], st["lru_h"], lw, tt=tt, t_valid=t_valid)
        yc, gdn_conv, gdn_s = _gdn_call(r3(cc), st["gdn_conv"], st["gdn_s"], lw, tt=tt, c=c, t_valid=t_valid)
        yd, shift, rwkv_s = _rwkv_call(r3(cd), st["rwkv_shift"], st["rwkv_s"], lw, tt=tt, c=c,
                                       t_valid=t_valid)
        ys = [ya.reshape(n, -1), yb.reshape(n, -1), yc.reshape(n, -1), yd.reshape(n, -1)]
        xf = _token_call(xf, lw["ffn2"], pre=(ys, lw["w_out"]), tm=tm,
                         final_g=final_g if l == depth - 1 else None)[0]
        new_states.append((r3(ckv)[:, :t_valid], r3(kpe)[:, :t_valid], lru_conv, lru_h[:, 0], gdn_conv,
                           _extract_state(gdn_s), shift[:, 0], _extract_state(rwkv_s)))
    stacked = [jnp.stack(t) for t in zip(*new_states)]
    return xf.reshape(b, lp, d), stacked


def _group_config(b, t_valid):
    if t_valid <= CHUNK:
        lp = t_valid
        return dict(lp=lp, t_valid=t_valid, tm=b * lp if b * lp <= 512 else lp, tt=lp, c=lp, tq=lp, tk=1024)
    lp = -(-t_valid // 384) * 384
    return dict(lp=lp, t_valid=t_valid, tm=384, tt=384, c=CHUNK, tq=384, tk=384)


def kernel(x_prompt, x_sample, cache_mla_ckv, cache_mla_kpe, state_lru_conv, state_lru_h, state_gdn_conv, state_gdn_s, state_rwkv_shift, state_rwkv_s, meta_tokens, norm_ffn1, ffn1_w1, ffn1_w2, norm_mix, w_in, mla_q_a_norm, mla_w_uq, mla_kv_a_norm, mla_w_ukv, mla_q_norm, mla_k_norm, lru_conv_w, lru_conv_b, lru_wa, lru_ba, lru_wx, lru_bx, lru_lambda, gdn_conv_w, gdn_a_log, gdn_dt_bias, gdn_o_norm, rwkv_mu, rwkv_w0, rwkv_w_b, rwkv_a0, rwkv_a_b, rwkv_g_b, rwkv_k_k, rwkv_k_a, rwkv_r_k, rwkv_ln_w, rwkv_ln_b, w_out, norm_ffn2, ffn2_w1, ffn2_w2, final_norm):
    p = dict(norm_ffn1=norm_ffn1, ffn1_w1=ffn1_w1, ffn1_w2=ffn1_w2, norm_mix=norm_mix, w_in=w_in,
             mla_q_a_norm=mla_q_a_norm, mla_w_uq=mla_w_uq, mla_kv_a_norm=mla_kv_a_norm,
             mla_w_ukv=mla_w_ukv, mla_q_norm=mla_q_norm, mla_k_norm=mla_k_norm,
             lru_conv_w=lru_conv_w, lru_conv_b=lru_conv_b, lru_wa=lru_wa, lru_ba=lru_ba,
             lru_wx=lru_wx, lru_bx=lru_bx, lru_lambda=lru_lambda, gdn_conv_w=gdn_conv_w,
             gdn_a_log=gdn_a_log, gdn_dt_bias=gdn_dt_bias, gdn_o_norm=gdn_o_norm, rwkv_mu=rwkv_mu,
             rwkv_w0=rwkv_w0, rwkv_w_b=rwkv_w_b, rwkv_a0=rwkv_a0, rwkv_a_b=rwkv_a_b, rwkv_g_b=rwkv_g_b,
             rwkv_k_k=rwkv_k_k, rwkv_k_a=rwkv_k_a, rwkv_r_k=rwkv_r_k, rwkv_ln_w=rwkv_ln_w,
             rwkv_ln_b=rwkv_ln_b, w_out=w_out, norm_ffn2=norm_ffn2, ffn2_w1=ffn2_w1, ffn2_w2=ffn2_w2)
    depth = w_in.shape[0]
    lws = [_prep_layer_weights(p, l) for l in range(depth)]
    final_g = final_norm.reshape(1, -1).astype(F32)
    d = x_prompt.shape[-1]

    bp, seq, _ = x_prompt.shape
    tp = N_META + seq
    cfg = _group_config(bp, tp)
    lp = cfg["lp"]
    x0 = jnp.concatenate([jnp.broadcast_to(meta_tokens.astype(F32)[None], (bp, N_META, d)), x_prompt,
                          jnp.zeros((bp, lp - tp, d), F32)], axis=1)
    cfg["rope"] = tuple(jnp.tile(t, (bp, 1)) for t in _rope_tables(jnp.arange(lp)))
    zero = dict(mla=None, lru_conv=jnp.zeros((bp, 8, MIXW), F32), lru_h=jnp.zeros((bp, 1, MIXW), F32),
                gdn_conv=jnp.zeros((bp, 8, 3 * MIXW), F32), gdn_s=jnp.zeros((bp, MIXW, MIXW), F32),
                rwkv_shift=jnp.zeros((bp, 8, D_W), F32), rwkv_s=jnp.zeros((bp, MIXW, MIXW), F32))
    yp, p_new = _run_group(x0, lws, final_g, [zero] * depth, cfg)

    bs, ts, _ = x_sample.shape
    past = cache_mla_ckv.shape[2]
    cfg_s = _group_config(bs, ts)
    cfg_s["rope"] = tuple(jnp.tile(t, (bs, 1)) for t in _rope_tables(past + jnp.arange(ts)))
    st_s = [dict(mla=(cache_mla_ckv, cache_mla_kpe), lru_conv=_hist8(state_lru_conv[l]),
                 lru_h=state_lru_h[l][:, None].astype(F32), gdn_conv=_hist8(state_gdn_conv[l]),
                 gdn_s=_embed_state(state_gdn_s[l]), rwkv_shift=_hist8(state_rwkv_shift[l][:, None]),
                 rwkv_s=_embed_state(state_rwkv_s[l])) for l in range(depth)]
    ys, s_new = _run_group(x_sample, lws, final_g, st_s, cfg_s)
    return (yp[:, N_META:tp], ys) + tuple(p_new) + tuple(s_new)
```

```python
import functools
import math

import jax
import jax.numpy as jnp
import numpy as np
from jax import lax
from jax.experimental import pallas as pl
from jax.experimental.pallas import tpu as pltpu

F32 = jnp.float32
BF16 = jnp.bfloat16
HIGHEST = lax.Precision.HIGHEST

EPS = 1e-6
N_META = 16
CHUNK = 64
CONV_W = 4
HEADS = 4
HD = 64
MIXW = HEADS * HD
MLA_NOPE = 64
MLA_ROPE = 32
MLA_QK = MLA_NOPE + MLA_ROPE
MLA_PAD = 128
ROPE_THETA = 10000.0
LRU_C = 8.0
RWKV_GN_EPS = 64e-5
RWKV_DECAY_SCALE = -0.606531

VMEM_LIMIT = 56 * 1024 * 1024

A_W = 640
B_W = 512
C_W = 1536
D_W = 1024


def _bdot(a, b):
    return jnp.dot(a.astype(BF16), b.astype(BF16), preferred_element_type=F32)


def _bdot_nt(a, b):
    return lax.dot_general(a.astype(BF16), b.astype(BF16), (((1,), (1,)), ((), ())),
                           preferred_element_type=F32)


def _bdot_tn(a, b):
    return lax.dot_general(a.astype(BF16), b.astype(BF16), (((0,), (0,)), ((), ())),
                           preferred_element_type=F32)


def _hdot(a, b):
    return jnp.dot(a, b, precision=HIGHEST, preferred_element_type=F32)


def _split3(x):
    x1 = x.astype(BF16)
    r1 = x - x1.astype(F32)
    x2 = r1.astype(BF16)
    r2 = r1 - x2.astype(F32)
    return x1, x2, r2.astype(BF16)


def _xdot_r01(x, m01):
    x1, x2, x3 = _split3(x)
    d = functools.partial(jnp.dot, preferred_element_type=F32)
    return d(x1, m01) + d(x2, m01) + d(x3, m01)


def _xdot_l01(m01, x):
    x1, x2, x3 = _split3(x)
    d = functools.partial(jnp.dot, preferred_element_type=F32)
    return d(m01, x1) + d(m01, x2) + d(m01, x3)


def _sigmoid(x):
    return 1.0 / (1.0 + jnp.exp(-x))


def _silu(x):
    return x * _sigmoid(x)


def _softplus(x):
    return jnp.maximum(x, 0.0) + jnp.log1p(jnp.exp(-jnp.abs(x)))


def _gelu_tanh(x):
    return 0.5 * x * (1.0 + jnp.tanh(0.7978845608028654 * (x + 0.044715 * (x * x * x))))


def _rms_rows(x, g):
    return x * lax.rsqrt(jnp.mean(x * x, axis=-1, keepdims=True) + EPS) * g


def _block_diag_mask():
    r = lax.broadcasted_iota(jnp.int32, (MIXW, MIXW), 0) >> 6
    c = lax.broadcasted_iota(jnp.int32, (MIXW, MIXW), 1) >> 6
    return r == c


def _embed(x, bd01):
    c = x.shape[0]
    x = x.astype(BF16)
    if c < HD:
        x = jnp.concatenate([x, jnp.zeros((HD - c, MIXW), BF16)], axis=0)
    return jnp.concatenate([x] * HEADS, axis=0) * bd01


def _mm(a, b):
    return jnp.dot(a.astype(BF16), b, preferred_element_type=F32)


def _mm_nt(a, b):
    return lax.dot_general(a.astype(BF16), b, (((1,), (1,)), ((), ())), preferred_element_type=F32)


def _head_masks(c):
    row = lax.broadcasted_iota(jnp.int32, (c, MIXW), 0)
    col = lax.broadcasted_iota(jnp.int32, (c, MIXW), 1) & (HD - 1)
    return row >= col, row > col, row == col


def _tril01(c):
    r = lax.broadcasted_iota(jnp.int32, (c, c), 0)
    k = lax.broadcasted_iota(jnp.int32, (c, c), 1)
    return jnp.where(r >= k, 1.0, 0.0).astype(BF16)


def _inv_unit_lower(lms, eye4, bd01):
    c = lms[0].shape[0]
    levels = int(math.log2(c))
    eye = jnp.where(eye4, 1.0, 0.0)
    n = [-x for x in lms]
    t = [eye + x for x in n]
    if levels < 2:
        return t
    p = [_mm(x, _embed(x, bd01)) for x in n]
    for _ in range(levels - 2):
        r = [_mm(jnp.concatenate([pi, ti], axis=0), _embed(pi, bd01)) for pi, ti in zip(p, t)]
        p = [x[0:c] for x in r]
        t = [ti + x[c:2 * c] for ti, x in zip(t, r)]
    return [ti + _mm(ti, _embed(pi, bd01)) for pi, ti in zip(p, t)]


def _shift_rows(x, s, fill):
    rolled = pltpu.roll(x, s, 0)
    row = lax.broadcasted_iota(jnp.int32, x.shape, 0)
    return jnp.where(row >= s, rolled, fill)


def _causal_conv(xext_ref, w_ref, tt):
    acc = None
    for j in range(CONV_W):
        term = xext_ref[pl.ds(8 - (CONV_W - 1) + j, tt), :] * w_ref[j:j + 1, :]
        acc = term if acc is None else acc + term
    return acc


def _const_spec(shape):
    nd = len(shape)
    return pl.BlockSpec(shape, lambda *_: (0,) * nd, pipeline_mode=pl.Buffered(1))


def _token_kernel(*refs, has_pre, has_post, has_final, d_ff, ff_chunk):
    it = iter(refs)
    x_ref = next(it)
    if has_pre:
        y_refs = [next(it) for _ in range(4)]
        wo_refs = [next(it) for _ in range(4)]
    gffn_ref, w1_ref, w2_ref = next(it), next(it), next(it)
    if has_post:
        gmix_ref, win_ref = next(it), next(it)
    if has_final:
        gfin_ref = next(it)
    xo_ref = next(it)
    if has_post:
        col_refs = [next(it) for _ in range(4)]

    x = x_ref[...]
    if has_pre:
        for y_ref, wo_ref in zip(y_refs, wo_refs):
            x = x + jnp.dot(y_ref[...], wo_ref[...], preferred_element_type=F32)
    h = _rms_rows(x, gffn_ref[...]).astype(BF16)
    acc = None
    for j in range(d_ff // ff_chunk):
        lo = j * ff_chunk
        gate = jnp.dot(h, w1_ref[:, lo:lo + ff_chunk], preferred_element_type=F32)
        up = jnp.dot(h, w1_ref[:, d_ff + lo:d_ff + lo + ff_chunk], preferred_element_type=F32)
        act = (_silu(gate) * up).astype(BF16)
        part = jnp.dot(act, w2_ref[lo:lo + ff_chunk, :], preferred_element_type=F32)
        acc = part if acc is None else acc + part
    x = x + 0.5 * acc
    if has_final:
        xo_ref[...] = _rms_rows(x, gfin_ref[...])
    else:
        xo_ref[...] = x
    if has_post:
        hm = _rms_rows(x, gmix_ref[...]).astype(BF16)
        off = 0
        for c_ref in col_refs:
            wd = c_ref.shape[-1]
            c_ref[...] = jnp.dot(hm, win_ref[:, off:off + wd], preferred_element_type=F32)
            off += wd


def _token_call(x, lw, *, pre=None, post=False, final_g=None, tm):
    n, d = x.shape
    d_ff = lw["w2"].shape[0]
    ff_chunk = d_ff // 2 if (d_ff // 2) % 128 == 0 else d_ff
    row = lambda w: pl.BlockSpec((tm, w), lambda i: (i, 0))
    args, specs = [x], [row(d)]
    if pre is not None:
        ys, wos = pre
        for y in ys:
            args.append(y)
            specs.append(row(y.shape[1]))
        for w in wos:
            args.append(w)
            specs.append(_const_spec(w.shape))
    for name in ("g_ffn", "w1", "w2"):
        args.append(lw[name])
        specs.append(_const_spec(lw[name].shape))
    if post:
        for name in ("g_mix", "w_in"):
            args.append(lw[name])
            specs.append(_const_spec(lw[name].shape))
    if final_g is not None:
        args.append(final_g)
        specs.append(_const_spec(final_g.shape))
    out_shape = [jax.ShapeDtypeStruct((n, d), F32)]
    out_specs = [row(d)]
    if post:
        for wd in (A_W, B_W, C_W, D_W):
            out_shape.append(jax.ShapeDtypeStruct((n, wd), F32))
            out_specs.append(row(wd))
    body = functools.partial(_token_kernel, has_pre=pre is not None, has_post=post,
                             has_final=final_g is not None, d_ff=d_ff, ff_chunk=ff_chunk)
    return pl.pallas_call(
        body, grid=(n // tm,), in_specs=specs, out_specs=out_specs, out_shape=out_shape,
        compiler_params=pltpu.CompilerParams(dimension_semantics=("parallel",),
                                             vmem_limit_bytes=VMEM_LIMIT),
        name="token_block")(*args)


def _mla_prep_kernel(ca_ref, cos_ref, sin_ref, gqa_ref, wq_ref, gkva_ref, wkv_ref, gq_ref, gk_ref,
                     q_ref, k_ref, v_ref, ckv_ref, kpe_ref):
    cs = cos_ref[...]
    sn = sin_ref[...]
    hw = HEADS * MLA_PAD
    cqn = _rms_rows(ca_ref[:, 0:256], gqa_ref[...])
    qq = _bdot(cqn, wq_ref[...])
    scale = 1.0 / math.sqrt(MLA_QK)
    for h in range(HEADS):
        lo = h * MLA_PAD
        blk = qq[:, lo:lo + MLA_PAD] * cs + qq[:, hw + lo:hw + lo + MLA_PAD] * sn
        ms = jnp.sum(blk * blk, axis=-1, keepdims=True) * (1.0 / MLA_QK)
        q_ref[:, lo:lo + MLA_PAD] = (blk * lax.rsqrt(ms + EPS) * (gq_ref[...] * scale)).astype(BF16)
    ckv = _rms_rows(ca_ref[:, 256:384], gkva_ref[...])
    ckv_ref[...] = ckv
    kpe = ca_ref[:, 384:512] * cs + ca_ref[:, 512:640] * sn
    kpe_ref[...] = kpe[:, 0:MLA_ROPE]
    kv = _bdot(ckv, wkv_ref[...])
    for h in range(HEADS):
        lo = h * MLA_PAD
        kb = kv[:, lo:lo + MLA_PAD] + kpe
        ms = jnp.sum(kb * kb, axis=-1, keepdims=True) * (1.0 / MLA_QK)
        k_ref[:, lo:lo + MLA_PAD] = (kb * lax.rsqrt(ms + EPS) * gk_ref[...]).astype(BF16)
        v_ref[:, lo:lo + MLA_PAD] = kv[:, hw + lo:hw + lo + MLA_PAD].astype(BF16)


def _mla_prep_call(ca, cos_t, sin_t, lw, *, tm):
    n = ca.shape[0]
    hw = HEADS * MLA_PAD
    row = lambda w: pl.BlockSpec((tm, w), lambda i: (i, 0))
    consts = [lw["mla_gqa"], lw["mla_wq"], lw["mla_gkva"], lw["mla_wkv"], lw["mla_gq"], lw["mla_gk"]]
    return pl.pallas_call(
        _mla_prep_kernel, grid=(n // tm,),
        in_specs=[row(A_W), row(MLA_PAD), row(MLA_PAD)] + [_const_spec(c.shape) for c in consts],
        out_specs=[row(hw), row(hw), row(hw), row(128), row(MLA_ROPE)],
        out_shape=[jax.ShapeDtypeStruct((n, hw), BF16)] * 3
        + [jax.ShapeDtypeStruct((n, 128), F32), jax.ShapeDtypeStruct((n, MLA_ROPE), F32)],
        compiler_params=pltpu.CompilerParams(dimension_semantics=("parallel",),
                                             vmem_limit_bytes=VMEM_LIMIT),
        name="mla_prep")(ca, cos_t, sin_t, *consts)


def _softmax_step(carry, s, vblk):
    m, l, acc = carry
    m_new = jnp.maximum(m, jnp.max(s, axis=-1, keepdims=True))
    alpha = jnp.exp(m - m_new)
    p = jnp.exp(s - m_new)
    l = alpha * l + jnp.sum(p, axis=-1, keepdims=True)
    acc = alpha * acc + jnp.dot(p.astype(BF16), vblk, preferred_element_type=F32)
    return m_new, l, acc


def _mla_prompt_kernel(q_ref, k_ref, v_ref, o_ref, *, tq, tk, n_kb):
    q0 = pl.program_id(1) * tq
    last_chunk = (q0 + tq - 1 - N_META) >> 6
    k_end = N_META + CHUNK * (last_chunk + 1)
    nkb = jnp.minimum((k_end + tk - 1) // tk, n_kb)
    qchunk = (q0 + lax.broadcasted_iota(jnp.int32, (tq, 1), 0) - N_META) >> 6
    for h in range(HEADS):
        lo = h * MLA_PAD
        qh = q_ref[:, lo:lo + MLA_PAD]

        def body(kb, carry, lo=lo, qh=qh):
            k0 = pl.multiple_of(kb * tk, tk)
            kblk = k_ref[pl.ds(k0, tk), lo:lo + MLA_PAD]
            vblk = v_ref[pl.ds(k0, tk), lo:lo + MLA_PAD]
            s = lax.dot_general(qh, kblk, (((1,), (1,)), ((), ())), preferred_element_type=F32)
            kchunk = (k0 + lax.broadcasted_iota(jnp.int32, (1, tk), 1) - N_META) >> 6
            s = jnp.where(kchunk <= qchunk, s, -1e30)
            return _softmax_step(carry, s, vblk)

        init = (jnp.full((tq, 1), -1e30, F32), jnp.zeros((tq, 1), F32), jnp.zeros((tq, MLA_PAD), F32))
        _, l, acc = lax.fori_loop(0, nkb, body, init)
        o_ref[:, lo:lo + MLA_PAD] = (acc / l).astype(BF16)


def _mla_prompt_call(q, k, v, *, tq, tk):
    b, lp, hw = q.shape
    body = functools.partial(_mla_prompt_kernel, tq=tq, tk=tk, n_kb=lp // tk)
    return pl.pallas_call(
        body, grid=(b, lp // tq),
        in_specs=[pl.BlockSpec((None, tq, hw), lambda i, j: (i, j, 0)),
                  pl.BlockSpec((None, lp, hw), lambda i, j: (i, 0, 0)),
                  pl.BlockSpec((None, lp, hw), lambda i, j: (i, 0, 0))],
        out_specs=pl.BlockSpec((None, tq, hw), lambda i, j: (i, j, 0)),
        out_shape=jax.ShapeDtypeStruct((b, lp, hw), BF16),
        compiler_params=pltpu.CompilerParams(dimension_semantics=("parallel", "arbitrary"),
                                             vmem_limit_bytes=VMEM_LIMIT),
        name="mla_prompt_attn")(q, k, v)


def _mla_sample_kernel(q_ref, kn_ref, vn_ref, cckv_ref, ckpe_ref, wkv_ref, gk_ref, o_ref, kpe_pad,
                       *, tk, past):
    hw = HEADS * MLA_PAD
    t = q_ref.shape[0]
    kpe_pad[...] = jnp.zeros(kpe_pad.shape, F32)
    kpe_pad[:, 0:MLA_ROPE] = ckpe_ref[...]
    qs = [q_ref[:, h * MLA_PAD:(h + 1) * MLA_PAD] for h in range(HEADS)]

    def body(c, carry):
        k0 = pl.multiple_of(c * tk, tk)
        kv = _bdot(cckv_ref[pl.ds(k0, tk), :], wkv_ref[...])
        kpe = kpe_pad[pl.ds(k0, tk), :]
        out = []
        for h in range(HEADS):
            lo = h * MLA_PAD
            kb = kv[:, lo:lo + MLA_PAD] + kpe
            ms = jnp.sum(kb * kb, axis=-1, keepdims=True) * (1.0 / MLA_QK)
            kn = (kb * lax.rsqrt(ms + EPS) * gk_ref[...]).astype(BF16)
            s = lax.dot_general(qs[h], kn, (((1,), (1,)), ((), ())), preferred_element_type=F32)
            out.append(_softmax_step(carry[h], s, kv[:, hw + lo:hw + lo + MLA_PAD].astype(BF16)))
        return tuple(out)

    init = tuple((jnp.full((t, 1), -1e30, F32), jnp.zeros((t, 1), F32), jnp.zeros((t, MLA_PAD), F32))
                 for _ in range(HEADS))
    carry = lax.fori_loop(0, past // tk, body, init)
    for h in range(HEADS):
        lo = h * MLA_PAD
        s = lax.dot_general(qs[h], kn_ref[:, lo:lo + MLA_PAD], (((1,), (1,)), ((), ())),
                            preferred_element_type=F32)
        _, l, acc = _softmax_step(carry[h], s, vn_ref[:, lo:lo + MLA_PAD])
        o_ref[:, lo:lo + MLA_PAD] = (acc / l).astype(BF16)


def _mla_sample_call(q, kn, vn, cache_ckv, cache_kpe, layer, lw, *, tk):
    b, t, hw = q.shape
    past = cache_ckv.shape[2]
    seq = lambda: pl.BlockSpec((None, t, hw), lambda i: (i, 0, 0))
    body = functools.partial(_mla_sample_kernel, tk=tk, past=past)
    return pl.pallas_call(
        body, grid=(b,),
        in_specs=[seq(), seq(), seq(),
                  pl.BlockSpec((None, None, past, cache_ckv.shape[3]), lambda i: (layer, i, 0, 0)),
                  pl.BlockSpec((None, None, past, cache_kpe.shape[3]), lambda i: (layer, i, 0, 0)),
                  _const_spec(lw["mla_wkv"].shape), _const_spec(lw["mla_gk"].shape)],
        out_specs=seq(),
        out_shape=jax.ShapeDtypeStruct((b, t, hw), BF16),
        scratch_shapes=[pltpu.VMEM((past, MLA_PAD), F32)],
        compiler_params=pltpu.CompilerParams(dimension_semantics=("arbitrary",),
                                             vmem_limit_bytes=VMEM_LIMIT),
        name="mla_sample_attn")(q, kn, vn, cache_ckv, cache_kpe, lw["mla_wkv"], lw["mla_gk"])


def _lru_kernel(cb_ref, buf_ref, h0_ref, cw_ref, cbias_ref, wa_ref, ba_ref, wx_ref, bx_ref, lam_ref,
                y_ref, nbuf_ref, hl_ref, xext, hcar, *, tt, t_valid):
    ti = pl.program_id(1)

    @pl.when(ti == 0)
    def _():
        xext[0:8, :] = buf_ref[...]
        hcar[...] = h0_ref[...]

    @pl.when(ti > 0)
    def _():
        xext[0:8, :] = xext[tt:tt + 8, :]

    xext[8:tt + 8, :] = cb_ref[:, 0:MIXW]
    xc = _causal_conv(xext, cw_ref, tt) + cbias_ref[...]
    r = _sigmoid(_bdot(xc, wa_ref[...]) + ba_ref[...])
    i = _sigmoid(_bdot(xc, wx_ref[...]) + bx_ref[...])
    log_a = -LRU_C * r * _softplus(-lam_ref[...])
    pos = ti * tt + lax.broadcasted_iota(jnp.int32, (tt, 1), 0)
    log_a = jnp.where(pos < t_valid, log_a, 0.0)
    a = jnp.exp(log_a)
    b = jnp.sqrt(-jnp.tanh(log_a) * (a * a + 1.0)) * (i * xc)
    s = 1
    while s < tt:
        b = a * _shift_rows(b, s, 0.0) + b
        a = a * _shift_rows(a, s, 1.0)
        s *= 2
    h = a * hcar[...] + b
    hcar[...] = h[tt - 1:tt, :]
    y_ref[...] = (h * _gelu_tanh(cb_ref[:, MIXW:2 * MIXW])).astype(BF16)

    last = t_valid - 1

    @pl.when(ti == last // tt)
    def _():
        r0 = last % tt
        hl_ref[...] = h[r0:r0 + 1, :]
        nbuf_ref[...] = xext[pl.ds(8 + r0 - (CONV_W - 2), CONV_W - 1), :]


def _lru_call(cb, buf8, h0, lw, *, tt, t_valid):
    b, lp, _ = cb.shape
    consts = [lw["lru_cw"], lw["lru_cb"], lw["lru_wa"], lw["lru_ba"], lw["lru_wx"], lw["lru_bx"],
              lw["lru_lam"]]
    body = functools.partial(_lru_kernel, tt=tt, t_valid=t_valid)
    per_b = lambda s: pl.BlockSpec((None,) + s, lambda i, j: (i, 0, 0))
    return pl.pallas_call(
        body, grid=(b, lp // tt),
        in_specs=[pl.BlockSpec((None, tt, B_W), lambda i, j: (i, j, 0)), per_b((8, MIXW)),
                  per_b((1, MIXW))] + [_const_spec(c.shape) for c in consts],
        out_specs=[pl.BlockSpec((None, tt, MIXW), lambda i, j: (i, j, 0)),
                   per_b((CONV_W - 1, MIXW)), per_b((1, MIXW))],
        out_shape=[jax.ShapeDtypeStruct((b, lp, MIXW), BF16),
                   jax.ShapeDtypeStruct((b, CONV_W - 1, MIXW), F32),
                   jax.ShapeDtypeStruct((b, 1, MIXW), F32)],
        scratch_shapes=[pltpu.VMEM((tt + 8, MIXW), F32), pltpu.VMEM((1, MIXW), F32)],
        compiler_params=pltpu.CompilerParams(dimension_semantics=("parallel", "arbitrary"),
                                             vmem_limit_bytes=VMEM_LIMIT),
        name="rglru")(cb, buf8, h0, *consts)


def _gdn_kernel(cc_ref, buf_ref, s0_ref, cw_ref, alog_ref, dtb_ref, gon_ref,
                y_ref, nbuf_ref, so_ref, xext, q_s, k_s, v_s, b_s, g_s, o_s, u_s, a_s, w_s, e_s, l_s,
                m_s, h_s, n_s, *, tt, c, t_valid):
    ti = pl.program_id(1)
    qkw = 3 * MIXW

    @pl.when(ti == 0)
    def _():
        xext[0:8, :] = buf_ref[...]
        so_ref[...] = s0_ref[...]

    @pl.when(ti > 0)
    def _():
        xext[0:8, :] = xext[tt:tt + 8, :]

    xext[8:tt + 8, :] = cc_ref[:, 0:qkw]
    xs = _silu(_causal_conv(xext, cw_ref, tt))
    bd = _block_diag_mask()
    bd01 = jnp.where(bd, 1.0, 0.0).astype(BF16)
    q = xs[:, 0:MIXW]
    k = xs[:, MIXW:2 * MIXW]
    q_s[...] = q * lax.rsqrt(_xdot_r01(q * q, bd01) + EPS) * (HD ** -0.5)
    k_s[...] = k * lax.rsqrt(_xdot_r01(k * k, bd01) + EPS)
    v_s[...] = xs[:, 2 * MIXW:3 * MIXW]
    pos = ti * tt + lax.broadcasted_iota(jnp.int32, (tt, 1), 0)
    valid = pos < t_valid
    a_in = cc_ref[:, qkw + MIXW:qkw + 2 * MIXW]
    b_in = cc_ref[:, qkw + 2 * MIXW:qkw + 3 * MIXW]
    b_s[...] = jnp.where(valid, _sigmoid(b_in), 0.0)
    g_s[...] = jnp.where(valid, -jnp.exp(alog_ref[...]) * _softplus(a_in + dtb_ref[...]), 0.0)

    tril4, strict4, eye4 = _head_masks(c)
    tril01 = _tril01(c)
    ones01 = jnp.ones((c, c), BF16)
    n_chunks = tt // c

    ch = range(n_chunks)
    rs = [slice(ci * c, (ci + 1) * c) for ci in ch]
    gc = [_xdot_l01(tril01, g_s[r, :]) for r in rs]
    gct = [jnp.sum(jnp.where(eye4, x, 0.0), axis=0, keepdims=True) for x in gc]
    decay = [jnp.where(tril4, jnp.exp(jnp.where(tril4, gc[i] - gct[i], 0.0)), 0.0) for i in ch]
    kc = [k_s[r, :] for r in rs]
    beta = [b_s[r, :] for r in rs]
    kb = [kc[i] * beta[i] for i in ch]
    sc = [_mm_nt(jnp.concatenate([kb[i], q_s[rs[i], :]], axis=0), _embed(kc[i], bd01)) for i in ch]
    tinv = _inv_unit_lower([jnp.where(strict4, sc[i][0:c] * decay[i], 0.0) for i in ch], eye4, bd01)
    attn = [jnp.where(tril4, sc[i][c:2 * c] * decay[i], 0.0).astype(BF16) for i in ch]
    egc = [jnp.exp(x) for x in gc]
    uw = [_mm(tinv[i], jnp.concatenate([_embed(v_s[rs[i], :] * beta[i], bd01),
                                        _embed(kb[i] * egc[i], bd01)], axis=1)) for i in ch]
    glast = [x[c - 1:c, :] for x in gc]
    kd = [(kc[i] * jnp.exp(glast[i] - gc[i])).astype(BF16) for i in ch]
    an = [lax.dot_general(kd[i], jnp.concatenate([uw[i][:, MIXW:], uw[i][:, :MIXW]], axis=1).astype(BF16),
                          (((0,), (0,)), ((), ())), preferred_element_type=F32) for i in ch]
    for i in ch:
        u_s[rs[i], :] = uw[i][:, 0:MIXW]
        w_s[rs[i], :] = uw[i][:, MIXW:2 * MIXW].astype(BF16)
        e_s[rs[i], :] = (q_s[rs[i], :] * egc[i]).astype(BF16)
        a_s[rs[i], :] = attn[i]
        m_s[i] = jnp.where(bd, an[i][:, 0:MIXW], 0.0).astype(BF16)
        n_s[i] = jnp.where(bd, an[i][:, MIXW:2 * MIXW], 0.0)
        l_s[i:i + 1, :] = jnp.exp(glast[i])

    for i in ch:
        s_bd = so_ref[...]
        s16 = s_bd.astype(BF16)
        h_s[i] = s16
        so_ref[...] = s_bd * l_s[i:i + 1, :] + n_s[i] - jnp.dot(m_s[i], s16, preferred_element_type=F32)

    r = [jnp.dot(jnp.concatenate([w_s[rs[i], :], e_s[rs[i], :]], axis=0), h_s[i], preferred_element_type=F32)
         for i in ch]
    v_new = [u_s[rs[i], :] - r[i][0:c] for i in ch]
    ov = [jnp.dot(a_s[rs[i], :], _embed(v_new[i], bd01), preferred_element_type=F32) for i in ch]
    for i in ch:
        o_s[rs[i], :] = r[i][c:2 * c] + ov[i]

    o = o_s[...]
    ms = _xdot_r01(o * o, bd01) * (1.0 / HD)
    z = cc_ref[:, qkw:qkw + MIXW]
    y_ref[...] = (o * lax.rsqrt(ms + EPS) * gon_ref[...] * _silu(z)).astype(BF16)

    last = t_valid - 1

    @pl.when(ti == last // tt)
    def _():
        r0 = last % tt
        nbuf_ref[...] = xext[pl.ds(8 + r0 - (CONV_W - 2), CONV_W - 1), :]


def _gdn_call(cc, buf8, s0, lw, *, tt, c, t_valid):
    b, lp, _ = cc.shape
    qkw = 3 * MIXW
    consts = [lw["gdn_cw"], lw["gdn_alog"], lw["gdn_dtb"], lw["gdn_gon"]]
    body = functools.partial(_gdn_kernel, tt=tt, c=c, t_valid=t_valid)
    per_b = lambda s: pl.BlockSpec((None,) + s, lambda i, j: (i, 0, 0))
    return pl.pallas_call(
        body, grid=(b, lp // tt),
        in_specs=[pl.BlockSpec((None, tt, C_W), lambda i, j: (i, j, 0)), per_b((8, qkw)),
                  per_b((MIXW, MIXW))] + [_const_spec(x.shape) for x in consts],
        out_specs=[pl.BlockSpec((None, tt, MIXW), lambda i, j: (i, j, 0)),
                   per_b((CONV_W - 1, qkw)), per_b((MIXW, MIXW))],
        out_shape=[jax.ShapeDtypeStruct((b, lp, MIXW), BF16),
                   jax.ShapeDtypeStruct((b, CONV_W - 1, qkw), F32),
                   jax.ShapeDtypeStruct((b, MIXW, MIXW), F32)],
        scratch_shapes=[pltpu.VMEM((tt + 8, qkw), F32)] + [pltpu.VMEM((tt, MIXW), F32)] * 7
        + [pltpu.VMEM((tt, MIXW), BF16)] * 3 + [pltpu.VMEM((-(-(tt // c) // 8) * 8, MIXW), F32),
                                                pltpu.VMEM((tt // c, MIXW, MIXW), BF16),
                                                pltpu.VMEM((tt // c, MIXW, MIXW), BF16),
                                                pltpu.VMEM((tt // c, MIXW, MIXW), F32)],
        compiler_params=pltpu.CompilerParams(dimension_semantics=("parallel", "arbitrary"),
                                             vmem_limit_bytes=VMEM_LIMIT),
        name="gated_delta")(cc, buf8, s0, *consts)


def _rwkv_kernel(cd_ref, prev_ref, s0_ref, mu_ref, w0_ref, wb_ref, a0_ref, ab_ref, gb_ref, kk_ref,
                 ka_ref, rk_ref, lnw_ref, lnb_ref,
                 y_ref, shift_ref, so_ref, xext, r_s, k_s, v_s, n_s, b_s, w_s, o_s, xv_s,
                 t_s, a_s, lhs_s, p_s, g_s, h_s, d_s, *, tt, c, t_valid):
    ti = pl.program_id(1)

    @pl.when(ti == 0)
    def _():
        xext[0:8, :] = prev_ref[...]
        so_ref[...] = s0_ref[...]

    @pl.when(ti > 0)
    def _():
        xext[0:8, :] = xext[tt:tt + 8, :]

    x = cd_ref[...]
    xext[8:tt + 8, :] = x
    xm = x + (xext[pl.ds(7, tt), :] - x) * mu_ref[...]
    pos = ti * tt + lax.broadcasted_iota(jnp.int32, (tt, 1), 0)
    valid = pos < t_valid
    bd = _block_diag_mask()
    bd01 = jnp.where(bd, 1.0, 0.0).astype(BF16)
    r = xm[:, 0:MIXW]
    k = xm[:, MIXW:2 * MIXW]
    v = xm[:, 2 * MIXW:3 * MIXW]
    lo_blk = xm[:, 3 * MIXW:3 * MIXW + 128]
    logw = RWKV_DECAY_SCALE * _sigmoid(w0_ref[...] + _bdot(jnp.tanh(lo_blk), wb_ref[...]))
    a = _sigmoid(a0_ref[...] + _bdot(lo_blk, ab_ref[...]))
    kkv = k * kk_ref[...]
    kkn = kkv * lax.rsqrt(_xdot_r01(kkv * kkv, bd01) + EPS)
    kkn = jnp.where(valid, kkn, 0.0)
    kmod = jnp.where(valid, k * (1.0 + (a - 1.0) * ka_ref[...]), 0.0)
    r_s[...] = r
    k_s[...] = kmod
    v_s[...] = v
    n_s[...] = kkn
    b_s[...] = kkn * a
    w_s[...] = jnp.where(valid, logw, 0.0)

    tril4, strict4, eye4 = _head_masks(c)
    tril01 = _tril01(c)

    n_chunks = tt // c

    ch = range(n_chunks)
    rs = [slice(ci * c, (ci + 1) * c) for ci in ch]
    lw = [w_s[r, :] for r in rs]
    cum = [_xdot_l01(tril01, x) for x in lw]
    ecum = [jnp.exp(x) for x in cum]
    pin = [jnp.exp(-x) for x in cum]
    pc = [x[c - 1:c, :] for x in ecum]
    kinv = [k_s[rs[i], :] * pin[i] for i in ch]
    binv = [b_s[rs[i], :] * pin[i] for i in ch]
    nd = [n_s[rs[i], :] * jnp.exp(cum[i] - lw[i]) for i in ch]
    lhs = [jnp.concatenate([nd[i], r_s[rs[i], :] * ecum[i]], axis=0).astype(BF16) for i in ch]
    sk = [_mm_nt(lhs[i], _embed(kinv[i], bd01)) for i in ch]
    sb = [_mm_nt(lhs[i], _embed(binv[i], bd01)) for i in ch]
    tinv = _inv_unit_lower([jnp.where(strict4, sb[i][0:c], 0.0) for i in ch], eye4, bd01)
    av = [_mm(jnp.concatenate([jnp.where(strict4, sk[i][0:c], 0.0), jnp.where(tril4, sk[i][c:2 * c], 0.0)],
                              axis=0), _embed(v_s[rs[i], :], bd01)) for i in ch]
    tn = [_mm(tinv[i], jnp.concatenate([_embed(nd[i], bd01), _embed(av[i][0:c], bd01)], axis=1)) for i in ch]
    bp = [(binv[i] * pc[i]).astype(BF16) for i in ch]
    gh = [lax.dot_general(tn[i].astype(BF16), bp[i], (((0,), (0,)), ((), ())), preferred_element_type=F32)
          for i in ch]
    vk = [_bdot_tn(v_s[rs[i], :], kinv[i] * pc[i]) for i in ch]
    for i in ch:
        lhs_s[2 * i * c:2 * (i + 1) * c, :] = lhs[i]
        t_s[rs[i], :] = tinv[i].astype(BF16)
        a_s[rs[i], :] = jnp.where(tril4, sb[i][c:2 * c], 0.0).astype(BF16)
        xv_s[rs[i], :] = av[i][0:c]
        o_s[rs[i], :] = av[i][c:2 * c]
        p_s[i:i + 1, :] = pc[i]
        g_s[i] = jnp.where(bd, gh[i][0:MIXW], 0.0).astype(BF16)
        d_s[i] = jnp.where(bd, vk[i] - gh[i][MIXW:2 * MIXW], 0.0)

    for i in ch:
        st = so_ref[...]
        s16 = st.astype(BF16)
        h_s[i] = s16
        so_ref[...] = st * p_s[i:i + 1, :] + d_s[i] - jnp.dot(s16, g_s[i], preferred_element_type=F32)

    r2 = [lax.dot_general(lhs_s[2 * i * c:2 * (i + 1) * c, :], h_s[i], (((1,), (1,)), ((), ())),
                          preferred_element_type=F32) for i in ch]
    u = [jnp.dot(t_s[rs[i], :], _embed(r2[i][0:c] + xv_s[rs[i], :], bd01), preferred_element_type=F32)
         for i in ch]
    au = [jnp.dot(a_s[rs[i], :], _embed(u[i], bd01), preferred_element_type=F32) for i in ch]
    for i in ch:
        o_s[rs[i], :] = r2[i][c:2 * c] + o_s[rs[i], :] - au[i]

    o = o_s[...]
    mean = _xdot_r01(o, bd01) * (1.0 / HD)
    d = o - mean
    var = _xdot_r01(d * d, bd01) * (1.0 / HD)
    o = d * lax.rsqrt(var + RWKV_GN_EPS) * lnw_ref[...] + lnb_ref[...]
    kmod_all = k * (1.0 + (a - 1.0) * ka_ref[...])
    o = o + _xdot_r01(r * kmod_all * rk_ref[...], bd01) * v
    g = _bdot(_sigmoid(xm[:, 3 * MIXW + 128:3 * MIXW + 256]), gb_ref[...])
    y_ref[...] = (o * g).astype(BF16)

    last = t_valid - 1

    @pl.when(ti == last // tt)
    def _():
        shift_ref[...] = xext[pl.ds(8 + last % tt, 1), :]


def _rwkv_call(cd, prev8, s0, lw, *, tt, c, t_valid):
    b, lp, _ = cd.shape
    consts = [lw["rwkv_mu"], lw["rwkv_w0"], lw["rwkv_wb"], lw["rwkv_a0"], lw["rwkv_ab"], lw["rwkv_gb"],
              lw["rwkv_kk"], lw["rwkv_ka"], lw["rwkv_rk"], lw["rwkv_lnw"], lw["rwkv_lnb"]]
    body = functools.partial(_rwkv_kernel, tt=tt, c=c, t_valid=t_valid)
    per_b = lambda s: pl.BlockSpec((None,) + s, lambda i, j: (i, 0, 0))
    return pl.pallas_call(
        body, grid=(b, lp // tt),
        in_specs=[pl.BlockSpec((None, tt, D_W), lambda i, j: (i, j, 0)), per_b((8, D_W)),
                  per_b((MIXW, MIXW))] + [_const_spec(x.shape) for x in consts],
        out_specs=[pl.BlockSpec((None, tt, MIXW), lambda i, j: (i, j, 0)),
                   per_b((1, D_W)), per_b((MIXW, MIXW))],
        out_shape=[jax.ShapeDtypeStruct((b, lp, MIXW), BF16),
                   jax.ShapeDtypeStruct((b, 1, D_W), F32),
                   jax.ShapeDtypeStruct((b, MIXW, MIXW), F32)],
        scratch_shapes=[pltpu.VMEM((tt + 8, D_W), F32)] + [pltpu.VMEM((tt, MIXW), F32)] * 8
        + [pltpu.VMEM((tt, MIXW), BF16)] * 2 + [pltpu.VMEM((2 * tt, MIXW), BF16),
                                                pltpu.VMEM((-(-(tt // c) // 8) * 8, MIXW), F32),
                                                pltpu.VMEM((tt // c, MIXW, MIXW), BF16),
                                                pltpu.VMEM((tt // c, MIXW, MIXW), BF16),
                                                pltpu.VMEM((tt // c, MIXW, MIXW), F32)],
        compiler_params=pltpu.CompilerParams(dimension_semantics=("parallel", "arbitrary"),
                                             vmem_limit_bytes=VMEM_LIMIT),
        name="rwkv7")(cd, prev8, s0, *consts)


def _pad_cols(w, width):
    return jnp.pad(w, ((0, 0), (0, width - w.shape[1])))


def _mla_head_cols(w, n_heads, per_head, pieces):
    blocks = []
    for h in range(n_heads):
        cols = [sign * w[:, h * per_head + lo:h * per_head + hi] for lo, hi, sign in pieces]
        blocks.append(_pad_cols(jnp.concatenate(cols, axis=1), MLA_PAD))
    return jnp.concatenate(blocks, axis=1)


def _block_diag_heads(w):
    out = jnp.zeros((MIXW, MIXW), w.dtype)
    for h in range(HEADS):
        out = out.at[h * HD:(h + 1) * HD, h * HD:(h + 1) * HD].set(w[h])
    return out


def _prep_layer_weights(p, l):
    half = MLA_ROPE // 2
    row = lambda v: v.reshape(1, -1).astype(F32)
    lw = {}
    for tag in ("1", "2"):
        lw["ffn" + tag] = dict(g_ffn=row(p["norm_ffn" + tag][l]), w1=p["ffn%s_w1" % tag][l].astype(BF16),
                               w2=p["ffn%s_w2" % tag][l].astype(BF16))
    w_in = p["w_in"][l]
    o = 0
    cq = w_in[:, o:o + 256]; o += 256
    ckv = w_in[:, o:o + 128]; o += 128
    kpe = w_in[:, o:o + MLA_ROPE]; o += MLA_ROPE
    xb_gb = w_in[:, o:o + 512]; o += 512
    qkvz = w_in[:, o:o + 1024]; o += 1024
    a_in = w_in[:, o:o + HEADS]; o += HEADS
    b_in = w_in[:, o:o + HEADS]; o += HEADS
    rw = w_in[:, o:o + 1024]
    kpe_rot = jnp.concatenate([-kpe[:, half:], kpe[:, :half]], axis=1)
    w_in_p = jnp.concatenate([
        cq, ckv, _pad_cols(kpe, 128), _pad_cols(kpe_rot, 128),
        xb_gb,
        qkvz, jnp.repeat(a_in, HD, axis=1), jnp.repeat(b_in, HD, axis=1),
        rw], axis=1)
    lw["ffn1"]["g_mix"] = row(p["norm_mix"][l])
    lw["ffn1"]["w_in"] = w_in_p.astype(BF16)
    w_out = p["w_out"][l]
    wo_a = jnp.concatenate([jnp.pad(w_out[h * HD:(h + 1) * HD], ((0, MLA_PAD - HD), (0, 0)))
                            for h in range(HEADS)], axis=0)
    lw["w_out"] = [wo_a.astype(BF16)] + [w_out[MIXW * i:MIXW * (i + 1)].astype(BF16) for i in (1, 2, 3)]
    wuq = p["mla_w_uq"][l]
    wq = _mla_head_cols(wuq, HEADS, MLA_QK, [(MLA_NOPE, MLA_QK, 1.0), (0, MLA_NOPE, 1.0)])
    wq_rot = _mla_head_cols(wuq, HEADS, MLA_QK, [(MLA_NOPE + half, MLA_QK, -1.0),
                                                (MLA_NOPE, MLA_NOPE + half, 1.0)])
    lw["mla_wq"] = jnp.concatenate([wq, wq_rot], axis=1).astype(BF16)
    wukv = p["mla_w_ukv"][l]
    zero_rope = jnp.zeros((wukv.shape[0], MLA_ROPE), wukv.dtype)
    wk = jnp.concatenate([_pad_cols(jnp.concatenate([zero_rope, wukv[:, h * 128:h * 128 + MLA_NOPE]], axis=1),
                                    MLA_PAD) for h in range(HEADS)], axis=1)
    wv = jnp.concatenate([_pad_cols(wukv[:, h * 128 + MLA_NOPE:(h + 1) * 128], MLA_PAD)
                          for h in range(HEADS)], axis=1)
    lw["mla_wkv"] = jnp.concatenate([wk, wv], axis=1).astype(BF16)
    lw["mla_gqa"] = row(p["mla_q_a_norm"][l])
    lw["mla_gkva"] = row(p["mla_kv_a_norm"][l])
    perm = lambda g: row(_pad_cols(jnp.concatenate([g[MLA_NOPE:], g[:MLA_NOPE]])[None], MLA_PAD))
    lw["mla_gq"] = perm(p["mla_q_norm"][l])
    lw["mla_gk"] = perm(p["mla_k_norm"][l])
    lw["lru_cw"] = p["lru_conv_w"][l]
    lw["lru_cb"] = row(p["lru_conv_b"][l])
    lw["lru_wa"] = _block_diag_heads(p["lru_wa"][l]).astype(BF16)
    lw["lru_ba"] = row(p["lru_ba"][l])
    lw["lru_wx"] = _block_diag_heads(p["lru_wx"][l]).astype(BF16)
    lw["lru_bx"] = row(p["lru_bx"][l])
    lw["lru_lam"] = row(p["lru_lambda"][l])
    lw["gdn_cw"] = p["gdn_conv_w"][l]
    lw["gdn_alog"] = row(jnp.repeat(p["gdn_a_log"][l], HD))
    lw["gdn_dtb"] = row(jnp.repeat(p["gdn_dt_bias"][l], HD))
    lw["gdn_gon"] = row(jnp.tile(p["gdn_o_norm"][l], HEADS))
    z64 = jnp.zeros((64, MIXW), F32)
    lw["rwkv_mu"] = row(p["rwkv_mu"][l])
    lw["rwkv_w0"] = row(p["rwkv_w0"][l])
    lw["rwkv_wb"] = jnp.concatenate([p["rwkv_w_b"][l], z64], axis=0).astype(BF16)
    lw["rwkv_a0"] = row(p["rwkv_a0"][l])
    lw["rwkv_ab"] = jnp.concatenate([z64, p["rwkv_a_b"][l]], axis=0).astype(BF16)
    lw["rwkv_gb"] = p["rwkv_g_b"][l].astype(BF16)
    lw["rwkv_kk"] = row(p["rwkv_k_k"][l])
    lw["rwkv_ka"] = row(p["rwkv_k_a"][l])
    lw["rwkv_rk"] = row(p["rwkv_r_k"][l])
    lw["rwkv_lnw"] = row(p["rwkv_ln_w"][l])
    lw["rwkv_lnb"] = row(p["rwkv_ln_b"][l])
    return lw


def _rope_tables(pos):
    inv = ROPE_THETA ** (-jnp.arange(0, MLA_ROPE, 2, dtype=F32) / MLA_ROPE)
    ang = pos.astype(F32)[:, None] * inv[None, :]
    cos, sin = jnp.cos(ang), jnp.sin(ang)
    n = pos.shape[0]
    cos_t = jnp.concatenate([cos, cos, jnp.ones((n, MLA_PAD - MLA_ROPE), F32)], axis=1)
    sin_t = jnp.concatenate([sin, sin, jnp.zeros((n, MLA_PAD - MLA_ROPE), F32)], axis=1)
    return cos_t, sin_t


def _embed_state(s):
    b = s.shape[0]
    out = jnp.zeros((b, MIXW, MIXW), F32)
    for h in range(HEADS):
        out = out.at[:, h * HD:(h + 1) * HD, h * HD:(h + 1) * HD].set(s[:, h].astype(F32))
    return out


def _extract_state(s_bd):
    return jnp.stack([s_bd[:, h * HD:(h + 1) * HD, h * HD:(h + 1) * HD] for h in range(HEADS)], axis=1)


def _hist8(rows):
    return jnp.pad(rows.astype(F32), ((0, 0), (8 - rows.shape[1], 0), (0, 0)))


def _run_group(x, lws, final_g, states, cfg):
    b, lp, d = x.shape
    n = b * lp
    t_valid, tm, tt, c = cfg["t_valid"], cfg["tm"], cfg["tt"], cfg["c"]
    cos_t, sin_t = cfg["rope"]
    xf = x.reshape(n, d)
    new_states = []
    depth = len(lws)
    for l in range(depth):
        lw = lws[l]
        st = states[l]
        xf, ca, cb, cc, cd = _token_call(xf, lw["ffn1"], post=True, tm=tm)
        q, k, v, ckv, kpe = _mla_prep_call(ca, cos_t, sin_t, lw, tm=tm)
        r3 = lambda a: a.reshape(b, lp, a.shape[-1])
        if st["mla"] is None:
            ya = _mla_prompt_call(r3(q), r3(k), r3(v), tq=cfg["tq"], tk=cfg["tk"])
        else:
            cache_ckv, cache_kpe = st["mla"]
            ya = _mla_sample_call(r3(q), r3(k), r3(v), cache_ckv, cache_kpe, l, lw, tk=cfg["tk"])
        yb, lru_conv, lru_h = _lru_call(r3(cb), st["lru_conv"], st["lru_h"], lw, tt=tt, t_valid=t_valid)
        yc, gdn_conv, gdn_s = _gdn_call(r3(cc), st["gdn_conv"], st["gdn_s"], lw, tt=tt, c=c, t_valid=t_valid)
        yd, shift, rwkv_s = _rwkv_call(r3(cd), st["rwkv_shift"], st["rwkv_s"], lw, tt=tt, c=c,
                                       t_valid=t_valid)
        ys = [ya.reshape(n, -1), yb.reshape(n, -1), yc.reshape(n, -1), yd.reshape(n, -1)]
        xf = _token_call(xf, lw["ffn2"], pre=(ys, lw["w_out"]), tm=tm,
                         final_g=final_g if l == depth - 1 else None)[0]
        new_states.append((r3(ckv)[:, :t_valid], r3(kpe)[:, :t_valid], lru_conv, lru_h[:, 0], gdn_conv,
                           _extract_state(gdn_s), shift[:, 0], _extract_state(rwkv_s)))
    stacked = [jnp.stack(t) for t in zip(*new_states)]
    return xf.reshape(b, lp, d), stacked


def _group_config(b, t_valid):
    if t_valid <= CHUNK:
        lp = t_valid
        return dict(lp=lp, t_valid=t_valid, tm=b * lp if b * lp <= 512 else lp, tt=lp, c=lp, tq=lp, tk=1024)
    lp = -(-t_valid // 384) * 384
    return dict(lp=lp, t_valid=t_valid, tm=384, tt=384, c=CHUNK, tq=384, tk=384)


def kernel(x_prompt, x_sample, cache_mla_ckv, cache_mla_kpe, state_lru_conv, state_lru_h, state_gdn_conv, state_gdn_s, state_rwkv_shift, state_rwkv_s, meta_tokens, norm_ffn1, ffn1_w1, ffn1_w2, norm_mix, w_in, mla_q_a_norm, mla_w_uq, mla_kv_a_norm, mla_w_ukv, mla_q_norm, mla_k_norm, lru_conv_w, lru_conv_b, lru_wa, lru_ba, lru_wx, lru_bx, lru_lambda, gdn_conv_w, gdn_a_log, gdn_dt_bias, gdn_o_norm, rwkv_mu, rwkv_w0, rwkv_w_b, rwkv_a0, rwkv_a_b, rwkv_g_b, rwkv_k_k, rwkv_k_a, rwkv_r_k, rwkv_ln_w, rwkv_ln_b, w_out, norm_ffn2, ffn2_w1, ffn2_w2, final_norm):
    p = dict(norm_ffn1=norm_ffn1, ffn1_w1=ffn1_w1, ffn1_w2=ffn1_w2, norm_mix=norm_mix, w_in=w_in,
             mla_q_a_norm=mla_q_a_norm, mla_w_uq=mla_w_uq, mla_kv_a_norm=mla_kv_a_norm,
             mla_w_ukv=mla_w_ukv, mla_q_norm=mla_q_norm, mla_k_norm=mla_k_norm,
             lru_conv_w=lru_conv_w, lru_conv_b=lru_conv_b, lru_wa=lru_wa, lru_ba=lru_ba,
             lru_wx=lru_wx, lru_bx=lru_bx, lru_lambda=lru_lambda, gdn_conv_w=gdn_conv_w,
             gdn_a_log=gdn_a_log, gdn_dt_bias=gdn_dt_bias, gdn_o_norm=gdn_o_norm, rwkv_mu=rwkv_mu,
             rwkv_w0=rwkv_w0, rwkv_w_b=rwkv_w_b, rwkv_a0=rwkv_a0, rwkv_a_b=rwkv_a_b, rwkv_g_b=rwkv_g_b,
             rwkv_k_k=rwkv_k_k, rwkv_k_a=rwkv_k_a, rwkv_r_k=rwkv_r_k, rwkv_ln_w=rwkv_ln_w,
             rwkv_ln_b=rwkv_ln_b, w_out=w_out, norm_ffn2=norm_ffn2, ffn2_w1=ffn2_w1, ffn2_w2=ffn2_w2)
    depth = w_in.shape[0]
    lws = [_prep_layer_weights(p, l) for l in range(depth)]
    final_g = final_norm.reshape(1, -1).astype(F32)
    d = x_prompt.shape[-1]

    bp, seq, _ = x_prompt.shape
    tp = N_META + seq
    cfg = _group_config(bp, tp)
    lp = cfg["lp"]
    x0 = jnp.concatenate([jnp.broadcast_to(meta_tokens.astype(F32)[None], (bp, N_META, d)), x_prompt,
                          jnp.zeros((bp, lp - tp, d), F32)], axis=1)
    cfg["rope"] = tuple(jnp.tile(t, (bp, 1)) for t in _rope_tables(jnp.arange(lp)))
    zero = dict(mla=None, lru_conv=jnp.zeros((bp, 8, MIXW), F32), lru_h=jnp.zeros((bp, 1, MIXW), F32),
                gdn_conv=jnp.zeros((bp, 8, 3 * MIXW), F32), gdn_s=jnp.zeros((bp, MIXW, MIXW), F32),
                rwkv_shift=jnp.zeros((bp, 8, D_W), F32), rwkv_s=jnp.zeros((bp, MIXW, MIXW), F32))
    yp, p_new = _run_group(x0, lws, final_g, [zero] * depth, cfg)

    bs, ts, _ = x_sample.shape
    past = cache_mla_ckv.shape[2]
    cfg_s = _group_config(bs, ts)
    cfg_s["rope"] = tuple(jnp.tile(t, (bs, 1)) for t in _rope_tables(past + jnp.arange(ts)))
    st_s = [dict(mla=(cache_mla_ckv, cache_mla_kpe), lru_conv=_hist8(state_lru_conv[l]),
                 lru_h=state_lru_h[l][:, None].astype(F32), gdn_conv=_hist8(state_gdn_conv[l]),
                 gdn_s=_embed_state(state_gdn_s[l]), rwkv_shift=_hist8(state_rwkv_shift[l][:, None]),
                 rwkv_s=_embed_state(state_rwkv_s[l])) for l in range(depth)]
    ys, s_new = _run_group(x_sample, lws, final_g, st_s, cfg_s)
    return (yp[:, N_META:tp], ys) + tuple(p_new) + tuple(s_new)
```

```python
import functools
import math

import jax
import jax.numpy as jnp
import numpy as np
from jax import lax
from jax.experimental import pallas as pl
from jax.experimental.pallas import tpu as pltpu

F32 = jnp.float32
BF16 = jnp.bfloat16
HIGHEST = lax.Precision.HIGHEST

EPS = 1e-6
N_META = 16
CHUNK = 64
CONV_W = 4
HEADS = 4
HD = 64
MIXW = HEADS * HD
MLA_NOPE = 64
MLA_ROPE = 32
MLA_QK = MLA_NOPE + MLA_ROPE
MLA_V = 64
MLA_PAD = 128
ROPE_THETA = 10000.0
LRU_C = 8.0
RWKV_GN_EPS = 64e-5
RWKV_DECAY_SCALE = -0.606531

VMEM_LIMIT = 56 * 1024 * 1024

A_W = 640
B_W = 512
C_W = 1536
D_W = 1024


def _bdot(a, b):
    return jnp.dot(a.astype(BF16), b.astype(BF16), preferred_element_type=F32)


def _bdot_nt(a, b):
    return lax.dot_general(a.astype(BF16), b.astype(BF16), (((1,), (1,)), ((), ())),
                           preferred_element_type=F32)


def _bdot_tn(a, b):
    return lax.dot_general(a.astype(BF16), b.astype(BF16), (((0,), (0,)), ((), ())),
                           preferred_element_type=F32)


def _split3(x):
    x1 = x.astype(BF16)
    r1 = x - x1.astype(F32)
    x2 = r1.astype(BF16)
    r2 = r1 - x2.astype(F32)
    return x1, x2, r2.astype(BF16)


def _xdot_r01(x, m01):
    x1 = x.astype(BF16)
    x2 = (x - x1.astype(F32)).astype(BF16)
    d = functools.partial(jnp.dot, preferred_element_type=F32)
    return d(x1, m01) + d(x2, m01)


def _xdot_l01(m01, x):
    x1, x2, x3 = _split3(x)
    d = functools.partial(jnp.dot, preferred_element_type=F32)
    return d(m01, x1) + d(m01, x2) + d(m01, x3)


def _sigmoid(x):
    return 0.5 * jnp.tanh(0.5 * x) + 0.5


def _silu(x):
    return x * _sigmoid(x)


def _softplus(x):
    return jnp.maximum(x, 0.0) + jnp.log1p(jnp.exp(-jnp.abs(x)))


def _gelu_tanh(x):
    return 0.5 * x * (1.0 + jnp.tanh(0.7978845608028654 * (x + 0.044715 * (x * x * x))))


def _rms_rows(x, g):
    return x * lax.rsqrt(jnp.mean(x * x, axis=-1, keepdims=True) + EPS) * g


def _block_diag_mask():
    r = lax.broadcasted_iota(jnp.int32, (MIXW, MIXW), 0) >> 6
    c = lax.broadcasted_iota(jnp.int32, (MIXW, MIXW), 1) >> 6
    return r == c


def _embed(x, bd01):
    c = x.shape[0]
    x = x.astype(BF16)
    if c < HD:
        x = jnp.concatenate([x, jnp.zeros((HD - c, MIXW), BF16)], axis=0)
    return jnp.concatenate([x] * HEADS, axis=0) * bd01


def _mm(a, b):
    return jnp.dot(a.astype(BF16), b, preferred_element_type=F32)


def _mm_nt(a, b):
    return lax.dot_general(a.astype(BF16), b, (((1,), (1,)), ((), ())), preferred_element_type=F32)


def _head_masks(c):
    row = lax.broadcasted_iota(jnp.int32, (c, MIXW), 0)
    col = lax.broadcasted_iota(jnp.int32, (c, MIXW), 1) & (HD - 1)
    return row >= col, row > col, row == col


def _tril01(c):
    r = lax.broadcasted_iota(jnp.int32, (c, c), 0)
    k = lax.broadcasted_iota(jnp.int32, (c, c), 1)
    return jnp.where(r >= k, 1.0, 0.0).astype(BF16)


def _inv_unit_lower(lms, eye4, bd01):
    c = lms[0].shape[0]
    levels = int(math.log2(c))
    eye = jnp.where(eye4, 1.0, 0.0)
    n = [-x for x in lms]
    t = [eye + x for x in n]
    if levels < 2:
        return t
    p = [_mm(x, _embed(x, bd01)) for x in n]
    for _ in range(levels - 2):
        r = [_mm(jnp.concatenate([pi, ti], axis=0), _embed(pi, bd01)) for pi, ti in zip(p, t)]
        p = [x[0:c] for x in r]
        t = [ti + x[c:2 * c] for ti, x in zip(t, r)]
    return [ti + _mm(ti, _embed(pi, bd01)) for pi, ti in zip(p, t)]


def _shift_rows(x, s, fill):
    rolled = pltpu.roll(x, s, 0)
    row = lax.broadcasted_iota(jnp.int32, x.shape, 0)
    return jnp.where(row >= s, rolled, fill)


def _causal_conv(xext_ref, w_ref, tt):
    acc = None
    for j in range(CONV_W):
        term = xext_ref[pl.ds(8 - (CONV_W - 1) + j, tt), :] * w_ref[j:j + 1, :]
        acc = term if acc is None else acc + term
    return acc


def _const_spec(shape):
    nd = len(shape)
    return pl.BlockSpec(shape, lambda *_: (0,) * nd, pipeline_mode=pl.Buffered(1))


def _token_kernel(*refs, has_pre, has_post, has_final, d_ff, ff_chunk):
    it = iter(refs)
    x_ref = next(it)
    if has_pre:
        y_refs = [next(it) for _ in range(4)]
        wo_refs = [next(it) for _ in range(4)]
    gffn_ref, w1_ref, w2_ref = next(it), next(it), next(it)
    if has_post:
        gmix_ref, win_ref = next(it), next(it)
    if has_final:
        gfin_ref = next(it)
    xo_ref = next(it)
    if has_post:
        col_refs = [next(it) for _ in range(4)]

    x = x_ref[...]
    if has_pre:
        for y_ref, wo_ref in zip(y_refs, wo_refs):
            x = x + jnp.dot(y_ref[...], wo_ref[...], preferred_element_type=F32)
    h = _rms_rows(x, gffn_ref[...]).astype(BF16)
    acc = None
    for j in range(d_ff // ff_chunk):
        lo = j * ff_chunk
        gate = jnp.dot(h, w1_ref[:, lo:lo + ff_chunk], preferred_element_type=F32)
        up = jnp.dot(h, w1_ref[:, d_ff + lo:d_ff + lo + ff_chunk], preferred_element_type=F32)
        act = (_silu(gate) * up).astype(BF16)
        part = jnp.dot(act, w2_ref[lo:lo + ff_chunk, :], preferred_element_type=F32)
        acc = part if acc is None else acc + part
    x = x + 0.5 * acc
    if has_final:
        xo_ref[...] = _rms_rows(x, gfin_ref[...])
    else:
        xo_ref[...] = x
    if has_post:
        hm = _rms_rows(x, gmix_ref[...]).astype(BF16)
        off = 0
        for c_ref in col_refs:
            wd = c_ref.shape[-1]
            c_ref[...] = jnp.dot(hm, win_ref[:, off:off + wd], preferred_element_type=F32)
            off += wd


def _token_call(x, lw, *, pre=None, post=False, final_g=None, tm):
    n, d = x.shape
    d_ff = lw["w2"].shape[0]
    ff_chunk = d_ff // 2 if (d_ff // 2) % 128 == 0 else d_ff
    row = lambda w: pl.BlockSpec((tm, w), lambda i: (i, 0))
    args, specs = [x], [row(d)]
    if pre is not None:
        ys, wos = pre
        for y in ys:
            args.append(y)
            specs.append(row(y.shape[1]))
        for w in wos:
            args.append(w)
            specs.append(_const_spec(w.shape))
    for name in ("g_ffn", "w1", "w2"):
        args.append(lw[name])
        specs.append(_const_spec(lw[name].shape))
    if post:
        for name in ("g_mix", "w_in"):
            args.append(lw[name])
            specs.append(_const_spec(lw[name].shape))
    if final_g is not None:
        args.append(final_g)
        specs.append(_const_spec(final_g.shape))
    out_shape = [jax.ShapeDtypeStruct((n, d), F32)]
    out_specs = [row(d)]
    if post:
        for wd in (A_W, B_W, C_W, D_W):
            out_shape.append(jax.ShapeDtypeStruct((n, wd), F32))
            out_specs.append(row(wd))
    body = functools.partial(_token_kernel, has_pre=pre is not None, has_post=post,
                             has_final=final_g is not None, d_ff=d_ff, ff_chunk=ff_chunk)
    return pl.pallas_call(
        body, grid=(n // tm,), in_specs=specs, out_specs=out_specs, out_shape=out_shape,
        compiler_params=pltpu.CompilerParams(dimension_semantics=("parallel",),
                                             vmem_limit_bytes=VMEM_LIMIT),
        name="token_block")(*args)


def _mla_prep_kernel(ca_ref, cos_ref, sin_ref, gqa_ref, wq_ref, gkva_ref, wkv_ref, gq_ref, gk_ref,
                     q_ref, k_ref, v_ref, ckv_ref, kpe_ref):
    cs = cos_ref[...]
    sn = sin_ref[...]
    hw = HEADS * MLA_PAD
    cqn = _rms_rows(ca_ref[:, 0:256], gqa_ref[...])
    qq = _bdot(cqn, wq_ref[...])
    scale = math.log2(math.e) / math.sqrt(MLA_QK)
    for h in range(HEADS):
        lo = h * MLA_PAD
        blk = qq[:, lo:lo + MLA_PAD] * cs + qq[:, hw + lo:hw + lo + MLA_PAD] * sn
        ms = jnp.sum(blk * blk, axis=-1, keepdims=True) * (1.0 / MLA_QK)
        q_ref[:, lo:lo + MLA_PAD] = (blk * lax.rsqrt(ms + EPS) * (gq_ref[...] * scale)).astype(BF16)
    ckv = _rms_rows(ca_ref[:, 256:384], gkva_ref[...])
    ckv_ref[...] = ckv
    kpe = ca_ref[:, 384:512] * cs + ca_ref[:, 512:640] * sn
    kpe_ref[...] = kpe[:, 0:MLA_ROPE]
    kv = _bdot(ckv, wkv_ref[...])
    ones_lane = lax.broadcasted_iota(jnp.int32, (kv.shape[0], MLA_PAD), 1) == MLA_V
    for h in range(HEADS):
        lo = h * MLA_PAD
        kb = kv[:, lo:lo + MLA_PAD] + kpe
        ms = jnp.sum(kb * kb, axis=-1, keepdims=True) * (1.0 / MLA_QK)
        k_ref[:, lo:lo + MLA_PAD] = (kb * lax.rsqrt(ms + EPS) * gk_ref[...]).astype(BF16)
        v_ref[:, lo:lo + MLA_PAD] = jnp.where(ones_lane, 1.0, kv[:, hw + lo:hw + lo + MLA_PAD]).astype(BF16)


def _mla_prep_call(ca, cos_t, sin_t, lw, *, tm):
    n = ca.shape[0]
    hw = HEADS * MLA_PAD
    row = lambda w: pl.BlockSpec((tm, w), lambda i: (i, 0))
    consts = [lw["mla_gqa"], lw["mla_wq"], lw["mla_gkva"], lw["mla_wkv"], lw["mla_gq"], lw["mla_gk"]]
    return pl.pallas_call(
        _mla_prep_kernel, grid=(n // tm,),
        in_specs=[row(A_W), row(MLA_PAD), row(MLA_PAD)] + [_const_spec(c.shape) for c in consts],
        out_specs=[row(hw), row(hw), row(hw), row(128), row(MLA_ROPE)],
        out_shape=[jax.ShapeDtypeStruct((n, hw), BF16)] * 3
        + [jax.ShapeDtypeStruct((n, 128), F32), jax.ShapeDtypeStruct((n, MLA_ROPE), F32)],
        compiler_params=pltpu.CompilerParams(dimension_semantics=("parallel",),
                                             vmem_limit_bytes=VMEM_LIMIT),
        name="mla_prep")(ca, cos_t, sin_t, *consts)


def _softmax_step(carry, s, vblk):
    m, acc = carry
    m_new = jnp.maximum(m, jnp.max(s, axis=-1, keepdims=True))
    p = jnp.exp2(s - m_new)
    acc = jnp.exp2(m - m_new) * acc + jnp.dot(p.astype(BF16), vblk, preferred_element_type=F32)
    return m_new, acc


def _softmax_init(t):
    return jnp.full((t, 1), -1e30, F32), jnp.zeros((t, MLA_PAD), F32)


def _softmax_finish(acc):
    return (acc / acc[:, MLA_V:MLA_V + 1]).astype(BF16)


def _mla_prompt_kernel(q_ref, k_ref, v_ref, o_ref, *, tq, tk, n_kb):
    q0 = pl.program_id(1) * tq
    first_chunk = (q0 - N_META) >> 6
    last_chunk = (q0 + tq - 1 - N_META) >> 6
    n_full = (N_META + CHUNK * (first_chunk + 1)) // tk
    nkb = jnp.minimum((N_META + CHUNK * (last_chunk + 1) + tk - 1) // tk, n_kb)
    qchunk = (q0 + lax.broadcasted_iota(jnp.int32, (tq, 1), 0) - N_META) >> 6
    hs = range(HEADS)
    qs = [q_ref[:, h * MLA_PAD:(h + 1) * MLA_PAD] for h in hs]

    def body(kb, state, masked):
        k0 = pl.multiple_of(kb * tk, tk)
        s = [lax.dot_general(qs[h], k_ref[pl.ds(k0, tk), h * MLA_PAD:(h + 1) * MLA_PAD],
                             (((1,), (1,)), ((), ())), preferred_element_type=F32) for h in hs]
        if masked:
            visible = ((k0 + lax.broadcasted_iota(jnp.int32, (1, tk), 1) - N_META) >> 6) <= qchunk
            s = [jnp.where(visible, x, -1e30) for x in s]
        m_new = [jnp.maximum(state[h][0], jnp.max(s[h], axis=-1, keepdims=True)) for h in hs]
        p = [jnp.exp2(s[h] - m_new[h]).astype(BF16) for h in hs]
        pv = [jnp.dot(p[h], v_ref[pl.ds(k0, tk), h * MLA_PAD:(h + 1) * MLA_PAD], preferred_element_type=F32)
              for h in hs]
        return tuple((m_new[h], jnp.exp2(state[h][0] - m_new[h]) * state[h][1] + pv[h]) for h in hs)

    state = tuple(_softmax_init(tq) for _ in hs)
    state = lax.fori_loop(0, n_full, functools.partial(body, masked=False), state)
    state = lax.fori_loop(n_full, nkb, functools.partial(body, masked=True), state)
    for h in hs:
        o_ref[:, h * MLA_PAD:(h + 1) * MLA_PAD] = _softmax_finish(state[h][1])


def _mla_prompt_call(q, k, v, *, tq, tk):
    b, lp, hw = q.shape
    body = functools.partial(_mla_prompt_kernel, tq=tq, tk=tk, n_kb=lp // tk)
    return pl.pallas_call(
        body, grid=(b, lp // tq),
        in_specs=[pl.BlockSpec((None, tq, hw), lambda i, j: (i, j, 0)),
                  pl.BlockSpec((None, lp, hw), lambda i, j: (i, 0, 0)),
                  pl.BlockSpec((None, lp, hw), lambda i, j: (i, 0, 0))],
        out_specs=pl.BlockSpec((None, tq, hw), lambda i, j: (i, j, 0)),
        out_shape=jax.ShapeDtypeStruct((b, lp, hw), BF16),
        compiler_params=pltpu.CompilerParams(dimension_semantics=("parallel", "arbitrary"),
                                             vmem_limit_bytes=VMEM_LIMIT),
        name="mla_prompt_attn")(q, k, v)


def _mla_sample_kernel(q_ref, kn_ref, vn_ref, cckv_ref, ckpe_ref, wkv_ref, gk_ref, o_ref, kpe_pad,
                       *, tk, past):
    hw = HEADS * MLA_PAD
    t = q_ref.shape[0]
    kpe_pad[...] = jnp.zeros(kpe_pad.shape, F32)
    kpe_pad[:, 0:MLA_ROPE] = ckpe_ref[...]
    qs = [q_ref[:, h * MLA_PAD:(h + 1) * MLA_PAD] for h in range(HEADS)]
    ones_lane = lax.broadcasted_iota(jnp.int32, (tk, MLA_PAD), 1) == MLA_V

    def body(c, carry):
        k0 = pl.multiple_of(c * tk, tk)
        kv = _bdot(cckv_ref[pl.ds(k0, tk), :], wkv_ref[...])
        kpe = kpe_pad[pl.ds(k0, tk), :]
        out = []
        for h in range(HEADS):
            lo = h * MLA_PAD
            kb = kv[:, lo:lo + MLA_PAD] + kpe
            ms = jnp.sum(kb * kb, axis=-1, keepdims=True) * (1.0 / MLA_QK)
            kn = (kb * lax.rsqrt(ms + EPS) * gk_ref[...]).astype(BF16)
            s = lax.dot_general(qs[h], kn, (((1,), (1,)), ((), ())), preferred_element_type=F32)
            vblk = jnp.where(ones_lane, 1.0, kv[:, hw + lo:hw + lo + MLA_PAD]).astype(BF16)
            out.append(_softmax_step(carry[h], s, vblk))
        return tuple(out)

    carry = lax.fori_loop(0, past // tk, body, tuple(_softmax_init(t) for _ in range(HEADS)))
    for h in range(HEADS):
        lo = h * MLA_PAD
        s = lax.dot_general(qs[h], kn_ref[:, lo:lo + MLA_PAD], (((1,), (1,)), ((), ())),
                            preferred_element_type=F32)
        _, acc = _softmax_step(carry[h], s, vn_ref[:, lo:lo + MLA_PAD])
        o_ref[:, lo:lo + MLA_PAD] = _softmax_finish(acc)


def _mla_sample_call(q, kn, vn, cache_ckv, cache_kpe, layer, lw, *, tk):
    b, t, hw = q.shape
    past = cache_ckv.shape[2]
    seq = lambda: pl.BlockSpec((None, t, hw), lambda i: (i, 0, 0))
    body = functools.partial(_mla_sample_kernel, tk=tk, past=past)
    return pl.pallas_call(
        body, grid=(b,),
        in_specs=[seq(), seq(), seq(),
                  pl.BlockSpec((None, None, past, cache_ckv.shape[3]), lambda i: (layer, i, 0, 0)),
                  pl.BlockSpec((None, None, past, cache_kpe.shape[3]), lambda i: (layer, i, 0, 0)),
                  _const_spec(lw["mla_wkv"].shape), _const_spec(lw["mla_gk"].shape)],
        out_specs=seq(),
        out_shape=jax.ShapeDtypeStruct((b, t, hw), BF16),
        scratch_shapes=[pltpu.VMEM((past, MLA_PAD), F32)],
        compiler_params=pltpu.CompilerParams(dimension_semantics=("arbitrary",),
                                             vmem_limit_bytes=VMEM_LIMIT),
        name="mla_sample_attn")(q, kn, vn, cache_ckv, cache_kpe, lw["mla_wkv"], lw["mla_gk"])


def _lru_kernel(cb_ref, buf_ref, h0_ref, cw_ref, cbias_ref, wa_ref, ba_ref, wx_ref, bx_ref, lam_ref,
                y_ref, nbuf_ref, hl_ref, xext, hcar, *, tt, t_valid):
    ti = pl.program_id(1)

    @pl.when(ti == 0)
    def _():
        xext[0:8, :] = buf_ref[...]
        hcar[...] = h0_ref[...]

    @pl.when(ti > 0)
    def _():
        xext[0:8, :] = xext[tt:tt + 8, :]

    xext[8:tt + 8, :] = cb_ref[:, 0:MIXW]
    xc = _causal_conv(xext, cw_ref, tt) + cbias_ref[...]
    r = _sigmoid(_bdot(xc, wa_ref[...]) + ba_ref[...])
    i = _sigmoid(_bdot(xc, wx_ref[...]) + bx_ref[...])
    log_a = -LRU_C * r * _softplus(-lam_ref[...])
    pos = ti * tt + lax.broadcasted_iota(jnp.int32, (tt, 1), 0)
    log_a = jnp.where(pos < t_valid, log_a, 0.0)
    a = jnp.exp(log_a)
    b = jnp.sqrt(-jnp.tanh(log_a) * (a * a + 1.0)) * (i * xc)
    s = 1
    while s < tt:
        b = a * _shift_rows(b, s, 0.0) + b
        a = a * _shift_rows(a, s, 1.0)
        s *= 2
    h = a * hcar[...] + b
    hcar[...] = h[tt - 1:tt, :]
    y_ref[...] = (h * _gelu_tanh(cb_ref[:, MIXW:2 * MIXW])).astype(BF16)

    last = t_valid - 1

    @pl.when(ti == last // tt)
    def _():
        r0 = last % tt
        hl_ref[...] = h[r0:r0 + 1, :]
        nbuf_ref[...] = xext[pl.ds(8 + r0 - (CONV_W - 2), CONV_W - 1), :]


def _lru_call(cb, buf8, h0, lw, *, tt, t_valid):
    b, lp, _ = cb.shape
    consts = [lw["lru_cw"], lw["lru_cb"], lw["lru_wa"], lw["lru_ba"], lw["lru_wx"], lw["lru_bx"],
              lw["lru_lam"]]
    body = functools.partial(_lru_kernel, tt=tt, t_valid=t_valid)
    per_b = lambda s: pl.BlockSpec((None,) + s, lambda i, j: (i, 0, 0))
    return pl.pallas_call(
        body, grid=(b, lp // tt),
        in_specs=[pl.BlockSpec((None, tt, B_W), lambda i, j: (i, j, 0)), per_b((8, MIXW)),
                  per_b((1, MIXW))] + [_const_spec(c.shape) for c in consts],
        out_specs=[pl.BlockSpec((None, tt, MIXW), lambda i, j: (i, j, 0)),
                   per_b((CONV_W - 1, MIXW)), per_b((1, MIXW))],
        out_shape=[jax.ShapeDtypeStruct((b, lp, MIXW), BF16),
                   jax.ShapeDtypeStruct((b, CONV_W - 1, MIXW), F32),
                   jax.ShapeDtypeStruct((b, 1, MIXW), F32)],
        scratch_shapes=[pltpu.VMEM((tt + 8, MIXW), F32), pltpu.VMEM((1, MIXW), F32)],
        compiler_params=pltpu.CompilerParams(dimension_semantics=("parallel", "arbitrary"),
                                             vmem_limit_bytes=VMEM_LIMIT),
        name="rglru")(cb, buf8, h0, *consts)


def _gdn_kernel(cc_ref, buf_ref, s0_ref, cw_ref, alog_ref, dtb_ref, gon_ref,
                y_ref, nbuf_ref, so_ref, xext, q_s, k_s, v_s, b_s, g_s, o_s, u_s, a_s, w_s, e_s, l_s,
                m_s, h_s, n_s, *, nb, tt, c, t_valid):
    ti = pl.program_id(1)
    qkw = 3 * MIXW
    bd = _block_diag_mask()
    bd01 = jnp.where(bd, 1.0, 0.0).astype(BF16)
    pos = ti * tt + lax.broadcasted_iota(jnp.int32, (tt, 1), 0)
    valid = pos < t_valid

    for s in range(nb):
        xe = xext.at[s]
        sr = slice(s * tt, (s + 1) * tt)

        @pl.when(ti == 0)
        def _():
            xe[0:8, :] = buf_ref[s]
            so_ref[s] = s0_ref[s]

        @pl.when(ti > 0)
        def _():
            xe[0:8, :] = xe[tt:tt + 8, :]

        xe[8:tt + 8, :] = cc_ref[s, :, 0:qkw]
        xs = _silu(_causal_conv(xe, cw_ref, tt))
        q = xs[:, 0:MIXW]
        k = xs[:, MIXW:2 * MIXW]
        q_s[sr, :] = q * lax.rsqrt(_xdot_r01(q * q, bd01) + EPS) * (HD ** -0.5)
        k_s[sr, :] = k * lax.rsqrt(_xdot_r01(k * k, bd01) + EPS)
        v_s[sr, :] = xs[:, 2 * MIXW:3 * MIXW]
        a_in = cc_ref[s, :, qkw + MIXW:qkw + 2 * MIXW]
        b_in = cc_ref[s, :, qkw + 2 * MIXW:qkw + 3 * MIXW]
        b_s[sr, :] = jnp.where(valid, _sigmoid(b_in), 0.0)
        g_s[sr, :] = jnp.where(valid, -jnp.exp(alog_ref[...]) * _softplus(a_in + dtb_ref[...]), 0.0)

    tril4, strict4, eye4 = _head_masks(c)
    tril01 = _tril01(c)
    n_chunks = tt // c

    ch = range(nb * n_chunks)
    rs = [slice(ci * c, (ci + 1) * c) for ci in ch]
    gc = [_xdot_l01(tril01, g_s[r, :]) for r in rs]
    gct = [jnp.sum(jnp.where(eye4, x, 0.0), axis=0, keepdims=True) for x in gc]
    decay = [jnp.where(tril4, jnp.exp(jnp.where(tril4, gc[i] - gct[i], 0.0)), 0.0) for i in ch]
    kc = [k_s[r, :] for r in rs]
    beta = [b_s[r, :] for r in rs]
    kb = [kc[i] * beta[i] for i in ch]
    sc = [_mm_nt(jnp.concatenate([kb[i], q_s[rs[i], :]], axis=0), _embed(kc[i], bd01)) for i in ch]
    tinv = _inv_unit_lower([jnp.where(strict4, sc[i][0:c] * decay[i], 0.0) for i in ch], eye4, bd01)
    attn = [jnp.where(tril4, sc[i][c:2 * c] * decay[i], 0.0).astype(BF16) for i in ch]
    egc = [jnp.exp(x) for x in gc]
    uw = [_mm(tinv[i], jnp.concatenate([_embed(v_s[rs[i], :] * beta[i], bd01),
                                        _embed(kb[i] * egc[i], bd01)], axis=1)) for i in ch]
    glast = [x[c - 1:c, :] for x in gc]
    kd = [(kc[i] * jnp.exp(glast[i] - gc[i])).astype(BF16) for i in ch]
    an = [lax.dot_general(kd[i], jnp.concatenate([uw[i][:, MIXW:], uw[i][:, :MIXW]], axis=1).astype(BF16),
                          (((0,), (0,)), ((), ())), preferred_element_type=F32) for i in ch]
    for i in ch:
        u_s[rs[i], :] = uw[i][:, 0:MIXW]
        w_s[rs[i], :] = uw[i][:, MIXW:2 * MIXW].astype(BF16)
        e_s[rs[i], :] = (q_s[rs[i], :] * egc[i]).astype(BF16)
        a_s[rs[i], :] = attn[i]
        m_s[i] = jnp.where(bd, an[i][:, 0:MIXW], 0.0).astype(BF16)
        n_s[i] = jnp.where(bd, an[i][:, MIXW:2 * MIXW], 0.0)
        l_s[i:i + 1, :] = jnp.exp(glast[i])

    for ci in range(n_chunks):
        for s in range(nb):
            i = s * n_chunks + ci
            s_bd = so_ref[s]
            s16 = s_bd.astype(BF16)
            h_s[i] = s16
            so_ref[s] = s_bd * l_s[i:i + 1, :] + n_s[i] - jnp.dot(m_s[i], s16, preferred_element_type=F32)

    r = [jnp.dot(jnp.concatenate([w_s[rs[i], :], e_s[rs[i], :]], axis=0), h_s[i], preferred_element_type=F32)
         for i in ch]
    v_new = [u_s[rs[i], :] - r[i][0:c] for i in ch]
    ov = [jnp.dot(a_s[rs[i], :], _embed(v_new[i], bd01), preferred_element_type=F32) for i in ch]
    for i in ch:
        o_s[rs[i], :] = r[i][c:2 * c] + ov[i]

    last = t_valid - 1
    for s in range(nb):
        o = o_s[s * tt:(s + 1) * tt, :]
        ms = _xdot_r01(o * o, bd01) * (1.0 / HD)
        z = cc_ref[s, :, qkw:qkw + MIXW]
        y_ref[s] = (o * lax.rsqrt(ms + EPS) * gon_ref[...] * _silu(z)).astype(BF16)

        @pl.when(ti == last // tt)
        def _():
            nbuf_ref[s] = xext[s, pl.ds(8 + last % tt - (CONV_W - 2), CONV_W - 1), :]


def _gdn_call(cc, buf8, s0, lw, *, nb, tt, c, t_valid):
    b, lp, _ = cc.shape
    qkw = 3 * MIXW
    consts = [lw["gdn_cw"], lw["gdn_alog"], lw["gdn_dtb"], lw["gdn_gon"]]
    body = functools.partial(_gdn_kernel, nb=nb, tt=tt, c=c, t_valid=t_valid)
    per_b = lambda s: pl.BlockSpec((nb,) + s, lambda i, j: (i, 0, 0))
    n_ch = nb * (tt // c)
    return pl.pallas_call(
        body, grid=(b // nb, lp // tt),
        in_specs=[pl.BlockSpec((nb, tt, C_W), lambda i, j: (i, j, 0)), per_b((8, qkw)),
                  per_b((MIXW, MIXW))] + [_const_spec(x.shape) for x in consts],
        out_specs=[pl.BlockSpec((nb, tt, MIXW), lambda i, j: (i, j, 0)),
                   per_b((CONV_W - 1, qkw)), per_b((MIXW, MIXW))],
        out_shape=[jax.ShapeDtypeStruct((b, lp, MIXW), BF16),
                   jax.ShapeDtypeStruct((b, CONV_W - 1, qkw), F32),
                   jax.ShapeDtypeStruct((b, MIXW, MIXW), F32)],
        scratch_shapes=[pltpu.VMEM((nb, tt + 8, qkw), F32)] + [pltpu.VMEM((nb * tt, MIXW), F32)] * 7
        + [pltpu.VMEM((nb * tt, MIXW), BF16)] * 3 + [pltpu.VMEM((-(-n_ch // 8) * 8, MIXW), F32),
                                                     pltpu.VMEM((n_ch, MIXW, MIXW), BF16),
                                                     pltpu.VMEM((n_ch, MIXW, MIXW), BF16),
                                                     pltpu.VMEM((n_ch, MIXW, MIXW), F32)],
        compiler_params=pltpu.CompilerParams(dimension_semantics=("parallel", "arbitrary"),
                                             vmem_limit_bytes=VMEM_LIMIT),
        name="gated_delta")(cc, buf8, s0, *consts)


def _rwkv_kernel(cd_ref, prev_ref, s0_ref, mu_ref, w0_ref, wb_ref, a0_ref, ab_ref, gb_ref, kk_ref,
                 ka_ref, rk_ref, lnw_ref, lnb_ref,
                 y_ref, shift_ref, so_ref, xext, r_s, k_s, v_s, n_s, b_s, w_s, o_s, xv_s,
                 t_s, a_s, lhs_s, p_s, g_s, h_s, d_s, rk_s, gt_s, *, nb, tt, c, t_valid):
    ti = pl.program_id(1)
    pos = ti * tt + lax.broadcasted_iota(jnp.int32, (tt, 1), 0)
    valid = pos < t_valid
    bd = _block_diag_mask()
    bd01 = jnp.where(bd, 1.0, 0.0).astype(BF16)

    for s in range(nb):
        xe = xext.at[s]
        sr = slice(s * tt, (s + 1) * tt)

        @pl.when(ti == 0)
        def _():
            xe[0:8, :] = prev_ref[s]
            so_ref[s] = s0_ref[s]

        @pl.when(ti > 0)
        def _():
            xe[0:8, :] = xe[tt:tt + 8, :]

        x = cd_ref[s]
        xe[8:tt + 8, :] = x
        xm = x + (xe[pl.ds(7, tt), :] - x) * mu_ref[...]
        r = xm[:, 0:MIXW]
        k = xm[:, MIXW:2 * MIXW]
        lo_blk = xm[:, 3 * MIXW:3 * MIXW + 128]
        logw = RWKV_DECAY_SCALE * _sigmoid(w0_ref[...] + _bdot(jnp.tanh(lo_blk), wb_ref[...]))
        a = _sigmoid(a0_ref[...] + _bdot(lo_blk, ab_ref[...]))
        kkv = k * kk_ref[...]
        kkn = kkv * lax.rsqrt(_xdot_r01(kkv * kkv, bd01) + EPS)
        kkn = jnp.where(valid, kkn, 0.0)
        kmod = k * (1.0 + (a - 1.0) * ka_ref[...])
        r_s[sr, :] = r
        k_s[sr, :] = jnp.where(valid, kmod, 0.0)
        v_s[sr, :] = xm[:, 2 * MIXW:3 * MIXW]
        n_s[sr, :] = kkn
        b_s[sr, :] = kkn * a
        w_s[sr, :] = jnp.where(valid, logw, 0.0)
        rk_s[sr, :] = _xdot_r01(r * kmod * rk_ref[...], bd01)
        gt_s[sr, :] = _bdot(_sigmoid(xm[:, 3 * MIXW + 128:3 * MIXW + 256]), gb_ref[...])

    tril4, strict4, eye4 = _head_masks(c)
    tril01 = _tril01(c)

    n_chunks = tt // c

    ch = range(nb * n_chunks)
    rs = [slice(ci * c, (ci + 1) * c) for ci in ch]
    lw = [w_s[r, :] for r in rs]
    cum = [_xdot_l01(tril01, x) for x in lw]
    ecum = [jnp.exp(x) for x in cum]
    pin = [jnp.exp(-x) for x in cum]
    pc = [x[c - 1:c, :] for x in ecum]
    kinv = [k_s[rs[i], :] * pin[i] for i in ch]
    binv = [b_s[rs[i], :] * pin[i] for i in ch]
    nd = [n_s[rs[i], :] * jnp.exp(cum[i] - lw[i]) for i in ch]
    lhs = [jnp.concatenate([nd[i], r_s[rs[i], :] * ecum[i]], axis=0).astype(BF16) for i in ch]
    sk = [_mm_nt(lhs[i], _embed(kinv[i], bd01)) for i in ch]
    sb = [_mm_nt(lhs[i], _embed(binv[i], bd01)) for i in ch]
    tinv = _inv_unit_lower([jnp.where(strict4, sb[i][0:c], 0.0) for i in ch], eye4, bd01)
    av = [_mm(jnp.concatenate([jnp.where(strict4, sk[i][0:c], 0.0), jnp.where(tril4, sk[i][c:2 * c], 0.0)],
                              axis=0), _embed(v_s[rs[i], :], bd01)) for i in ch]
    tn = [_mm(tinv[i], jnp.concatenate([_embed(nd[i], bd01), _embed(av[i][0:c], bd01)], axis=1)) for i in ch]
    bp = [(binv[i] * pc[i]).astype(BF16) for i in ch]
    gh = [lax.dot_general(tn[i].astype(BF16), bp[i], (((0,), (0,)), ((), ())), preferred_element_type=F32)
          for i in ch]
    vk = [_bdot_tn(v_s[rs[i], :], kinv[i] * pc[i]) for i in ch]
    for i in ch:
        lhs_s[2 * i * c:2 * (i + 1) * c, :] = lhs[i]
        t_s[rs[i], :] = tinv[i].astype(BF16)
        a_s[rs[i], :] = jnp.where(tril4, sb[i][c:2 * c], 0.0).astype(BF16)
        xv_s[rs[i], :] = av[i][0:c]
        o_s[rs[i], :] = av[i][c:2 * c]
        p_s[i:i + 1, :] = pc[i]
        g_s[i] = jnp.where(bd, gh[i][0:MIXW], 0.0).astype(BF16)
        d_s[i] = jnp.where(bd, vk[i] - gh[i][MIXW:2 * MIXW], 0.0)

    for ci in range(n_chunks):
        for s in range(nb):
            i = s * n_chunks + ci
            st = so_ref[s]
            s16 = st.astype(BF16)
            h_s[i] = s16
            so_ref[s] = st * p_s[i:i + 1, :] + d_s[i] - jnp.dot(s16, g_s[i], preferred_element_type=F32)

    r2 = [lax.dot_general(lhs_s[2 * i * c:2 * (i + 1) * c, :], h_s[i], (((1,), (1,)), ((), ())),
                          preferred_element_type=F32) for i in ch]
    u = [jnp.dot(t_s[rs[i], :], _embed(r2[i][0:c] + xv_s[rs[i], :], bd01), preferred_element_type=F32)
         for i in ch]
    au = [jnp.dot(a_s[rs[i], :], _embed(u[i], bd01), preferred_element_type=F32) for i in ch]
    for i in ch:
        o_s[rs[i], :] = r2[i][c:2 * c] + o_s[rs[i], :] - au[i]

    last = t_valid - 1
    for s in range(nb):
        sr = slice(s * tt, (s + 1) * tt)
        o = o_s[sr, :]
        mean = _xdot_r01(o, bd01) * (1.0 / HD)
        d = o - mean
        var = _xdot_r01(d * d, bd01) * (1.0 / HD)
        o = d * lax.rsqrt(var + RWKV_GN_EPS) * lnw_ref[...] + lnb_ref[...]
        o = o + rk_s[sr, :] * v_s[sr, :]
        y_ref[s] = (o * gt_s[sr, :]).astype(BF16)

        @pl.when(ti == last // tt)
        def _():
            shift_ref[s] = xext[s, pl.ds(8 + last % tt, 1), :]


def _rwkv_call(cd, prev8, s0, lw, *, nb, tt, c, t_valid):
    b, lp, _ = cd.shape
    consts = [lw["rwkv_mu"], lw["rwkv_w0"], lw["rwkv_wb"], lw["rwkv_a0"], lw["rwkv_ab"], lw["rwkv_gb"],
              lw["rwkv_kk"], lw["rwkv_ka"], lw["rwkv_rk"], lw["rwkv_lnw"], lw["rwkv_lnb"]]
    body = functools.partial(_rwkv_kernel, nb=nb, tt=tt, c=c, t_valid=t_valid)
    per_b = lambda s: pl.BlockSpec((nb,) + s, lambda i, j: (i, 0, 0))
    n_ch = nb * (tt // c)
    return pl.pallas_call(
        body, grid=(b // nb, lp // tt),
        in_specs=[pl.BlockSpec((nb, tt, D_W), lambda i, j: (i, j, 0)), per_b((8, D_W)),
                  per_b((MIXW, MIXW))] + [_const_spec(x.shape) for x in consts],
        out_specs=[pl.BlockSpec((nb, tt, MIXW), lambda i, j: (i, j, 0)),
                   per_b((1, D_W)), per_b((MIXW, MIXW))],
        out_shape=[jax.ShapeDtypeStruct((b, lp, MIXW), BF16),
                   jax.ShapeDtypeStruct((b, 1, D_W), F32),
                   jax.ShapeDtypeStruct((b, MIXW, MIXW), F32)],
        scratch_shapes=[pltpu.VMEM((nb, tt + 8, D_W), F32)] + [pltpu.VMEM((nb * tt, MIXW), F32)] * 8
        + [pltpu.VMEM((nb * tt, MIXW), BF16)] * 2 + [pltpu.VMEM((2 * nb * tt, MIXW), BF16),
                                                     pltpu.VMEM((-(-n_ch // 8) * 8, MIXW), F32),
                                                     pltpu.VMEM((n_ch, MIXW, MIXW), BF16),
                                                     pltpu.VMEM((n_ch, MIXW, MIXW), BF16),
                                                     pltpu.VMEM((n_ch, MIXW, MIXW), F32)]
        + [pltpu.VMEM((nb * tt, MIXW), F32)] * 2,
        compiler_params=pltpu.CompilerParams(dimension_semantics=("parallel", "arbitrary"),
                                             vmem_limit_bytes=VMEM_LIMIT),
        name="rwkv7")(cd, prev8, s0, *consts)


def _pad_cols(w, width):
    return jnp.pad(w, ((0, 0), (0, width - w.shape[1])))


def _mla_head_cols(w, n_heads, per_head, pieces):
    blocks = []
    for h in range(n_heads):
        cols = [sign * w[:, h * per_head + lo:h * per_head + hi] for lo, hi, sign in pieces]
        blocks.append(_pad_cols(jnp.concatenate(cols, axis=1), MLA_PAD))
    return jnp.concatenate(blocks, axis=1)


def _block_diag_heads(w):
    eye = jnp.eye(HEADS, dtype=w.dtype)[:, None, :, None]
    return (w[:, :, None, :] * eye).reshape(MIXW, MIXW)


def _prep_layer_weights(p, l):
    half = MLA_ROPE // 2
    row = lambda v: v.reshape(1, -1).astype(F32)
    lw = {}
    for tag in ("1", "2"):
        lw["ffn" + tag] = dict(g_ffn=row(p["norm_ffn" + tag][l]), w1=p["ffn%s_w1" % tag][l].astype(BF16),
                               w2=p["ffn%s_w2" % tag][l].astype(BF16))
    w_in = p["w_in"][l]
    o = 0
    cq = w_in[:, o:o + 256]; o += 256
    ckv = w_in[:, o:o + 128]; o += 128
    kpe = w_in[:, o:o + MLA_ROPE]; o += MLA_ROPE
    xb_gb = w_in[:, o:o + 512]; o += 512
    qkvz = w_in[:, o:o + 1024]; o += 1024
    a_in = w_in[:, o:o + HEADS]; o += HEADS
    b_in = w_in[:, o:o + HEADS]; o += HEADS
    rw = w_in[:, o:o + 1024]
    kpe_rot = jnp.concatenate([-kpe[:, half:], kpe[:, :half]], axis=1)
    w_in_p = jnp.concatenate([
        cq, ckv, _pad_cols(kpe, 128), _pad_cols(kpe_rot, 128),
        xb_gb,
        qkvz, jnp.repeat(a_in, HD, axis=1), jnp.repeat(b_in, HD, axis=1),
        rw], axis=1)
    lw["ffn1"]["g_mix"] = row(p["norm_mix"][l])
    lw["ffn1"]["w_in"] = w_in_p.astype(BF16)
    w_out = p["w_out"][l]
    wo_a = jnp.concatenate([jnp.pad(w_out[h * HD:(h + 1) * HD], ((0, MLA_PAD - HD), (0, 0)))
                            for h in range(HEADS)], axis=0)
    lw["w_out"] = [wo_a.astype(BF16)] + [w_out[MIXW * i:MIXW * (i + 1)].astype(BF16) for i in (1, 2, 3)]
    wuq = p["mla_w_uq"][l]
    wq = _mla_head_cols(wuq, HEADS, MLA_QK, [(MLA_NOPE, MLA_QK, 1.0), (0, MLA_NOPE, 1.0)])
    wq_rot = _mla_head_cols(wuq, HEADS, MLA_QK, [(MLA_NOPE + half, MLA_QK, -1.0),
                                                (MLA_NOPE, MLA_NOPE + half, 1.0)])
    lw["mla_wq"] = jnp.concatenate([wq, wq_rot], axis=1).astype(BF16)
    wukv = p["mla_w_ukv"][l]
    zero_rope = jnp.zeros((wukv.shape[0], MLA_ROPE), wukv.dtype)
    wk = jnp.concatenate([_pad_cols(jnp.concatenate([zero_rope, wukv[:, h * 128:h * 128 + MLA_NOPE]], axis=1),
                                    MLA_PAD) for h in range(HEADS)], axis=1)
    wv = jnp.concatenate([_pad_cols(wukv[:, h * 128 + MLA_NOPE:(h + 1) * 128], MLA_PAD)
                          for h in range(HEADS)], axis=1)
    lw["mla_wkv"] = jnp.concatenate([wk, wv], axis=1).astype(BF16)
    lw["mla_gqa"] = row(p["mla_q_a_norm"][l])
    lw["mla_gkva"] = row(p["mla_kv_a_norm"][l])
    perm = lambda g: row(_pad_cols(jnp.concatenate([g[MLA_NOPE:], g[:MLA_NOPE]])[None], MLA_PAD))
    lw["mla_gq"] = perm(p["mla_q_norm"][l])
    lw["mla_gk"] = perm(p["mla_k_norm"][l])
    lw["lru_cw"] = p["lru_conv_w"][l]
    lw["lru_cb"] = row(p["lru_conv_b"][l])
    lw["lru_wa"] = _block_diag_heads(p["lru_wa"][l]).astype(BF16)
    lw["lru_ba"] = row(p["lru_ba"][l])
    lw["lru_wx"] = _block_diag_heads(p["lru_wx"][l]).astype(BF16)
    lw["lru_bx"] = row(p["lru_bx"][l])
    lw["lru_lam"] = row(p["lru_lambda"][l])
    lw["gdn_cw"] = p["gdn_conv_w"][l]
    lw["gdn_alog"] = row(jnp.repeat(p["gdn_a_log"][l], HD))
    lw["gdn_dtb"] = row(jnp.repeat(p["gdn_dt_bias"][l], HD))
    lw["gdn_gon"] = row(jnp.tile(p["gdn_o_norm"][l], HEADS))
    z64 = jnp.zeros((64, MIXW), F32)
    lw["rwkv_mu"] = row(p["rwkv_mu"][l])
    lw["rwkv_w0"] = row(p["rwkv_w0"][l])
    lw["rwkv_wb"] = jnp.concatenate([p["rwkv_w_b"][l], z64], axis=0).astype(BF16)
    lw["rwkv_a0"] = row(p["rwkv_a0"][l])
    lw["rwkv_ab"] = jnp.concatenate([z64, p["rwkv_a_b"][l]], axis=0).astype(BF16)
    lw["rwkv_gb"] = p["rwkv_g_b"][l].astype(BF16)
    lw["rwkv_kk"] = row(p["rwkv_k_k"][l])
    lw["rwkv_ka"] = row(p["rwkv_k_a"][l])
    lw["rwkv_rk"] = row(p["rwkv_r_k"][l])
    lw["rwkv_lnw"] = row(p["rwkv_ln_w"][l])
    lw["rwkv_lnb"] = row(p["rwkv_ln_b"][l])
    return lw


def _rope_tables(pos):
    inv = ROPE_THETA ** (-jnp.arange(0, MLA_ROPE, 2, dtype=F32) / MLA_ROPE)
    ang = pos.astype(F32)[:, None] * inv[None, :]
    cos, sin = jnp.cos(ang), jnp.sin(ang)
    n = pos.shape[0]
    cos_t = jnp.concatenate([cos, cos, jnp.ones((n, MLA_PAD - MLA_ROPE), F32)], axis=1)
    sin_t = jnp.concatenate([sin, sin, jnp.zeros((n, MLA_PAD - MLA_ROPE), F32)], axis=1)
    return cos_t, sin_t


def _embed_state(s):
    b = s.shape[0]
    eye = jnp.eye(HEADS, dtype=F32)[None, :, None, :, None]
    return (s.astype(F32)[:, :, :, None, :] * eye).reshape(b, MIXW, MIXW)


def _extract_state(s_bd):
    b = s_bd.shape[0]
    eye = jnp.eye(HEADS, dtype=F32)[None, :, None, :, None]
    return jnp.sum(s_bd.reshape(b, HEADS, HD, HEADS, HD) * eye, axis=3)


def _hist8(rows):
    return jnp.pad(rows.astype(F32), ((0, 0), (8 - rows.shape[1], 0), (0, 0)))


def _run_group(x, lws, final_g, states, cfg):
    b, lp, d = x.shape
    n = b * lp
    t_valid, tm, tt, c = cfg["t_valid"], cfg["tm"], cfg["tt"], cfg["c"]
    cos_t, sin_t = cfg["rope"]
    xf = x.reshape(n, d)
    new_states = []
    depth = len(lws)
    for l in range(depth):
        lw = lws[l]
        st = states[l]
        xf, ca, cb, cc, cd = _token_call(xf, lw["ffn1"], post=True, tm=tm)
        q, k, v, ckv, kpe = _mla_prep_call(ca, cos_t, sin_t, lw, tm=tm)
        r3 = lambda a: a.reshape(b, lp, a.shape[-1])
        if st["mla"] is None:
            ya = _mla_prompt_call(r3(q), r3(k), r3(v), tq=cfg["tq"], tk=cfg["tk"])
        else:
            cache_ckv, cache_kpe = st["mla"]
            ya = _mla_sample_call(r3(q), r3(k), r3(v), cache_ckv, cache_kpe, l, lw, tk=cfg["tk"])
        yb, lru_conv, lru_h = _lru_call(r3(cb), st["lru_conv"], st["lru_h"], lw, tt=tt, t_valid=t_valid)
        yc, gdn_conv, gdn_s = _gdn_call(r3(cc), st["gdn_conv"], st["gdn_s"], lw, nb=cfg["nb"], tt=tt, c=c,
                                        t_valid=t_valid)
        yd, shift, rwkv_s = _rwkv_call(r3(cd), st["rwkv_shift"], st["rwkv_s"], lw, nb=cfg["nb"], tt=tt, c=c,
                                       t_valid=t_valid)
        ys = [ya.reshape(n, -1), yb.reshape(n, -1), yc.reshape(n, -1), yd.reshape(n, -1)]
        xf = _token_call(xf, lw["ffn2"], pre=(ys, lw["w_out"]), tm=tm,
                         final_g=final_g if l == depth - 1 else None)[0]
        new_states.append((r3(ckv)[:, :t_valid], r3(kpe)[:, :t_valid], lru_conv, lru_h[:, 0], gdn_conv,
                           _extract_state(gdn_s), shift[:, 0], _extract_state(rwkv_s)))
    stacked = [jnp.stack(t) for t in zip(*new_states)]
    return xf.reshape(b, lp, d), stacked


def _group_config(b, t_valid):
    divisor = lambda target: max(d for d in range(1, target + 1) if b % d == 0)
    if t_valid <= CHUNK:
        lp = t_valid
        return dict(lp=lp, t_valid=t_valid, tm=b * lp if b * lp <= 512 else lp, tt=lp, c=lp, tq=lp, tk=1024,
                    nb=divisor(8))
    lp = -(-t_valid // 384) * 384
    return dict(lp=lp, t_valid=t_valid, tm=384, tt=384, c=CHUNK, tq=384, tk=384, nb=divisor(2))


def kernel(x_prompt, x_sample, cache_mla_ckv, cache_mla_kpe, state_lru_conv, state_lru_h, state_gdn_conv, state_gdn_s, state_rwkv_shift, state_rwkv_s, meta_tokens, norm_ffn1, ffn1_w1, ffn1_w2, norm_mix, w_in, mla_q_a_norm, mla_w_uq, mla_kv_a_norm, mla_w_ukv, mla_q_norm, mla_k_norm, lru_conv_w, lru_conv_b, lru_wa, lru_ba, lru_wx, lru_bx, lru_lambda, gdn_conv_w, gdn_a_log, gdn_dt_bias, gdn_o_norm, rwkv_mu, rwkv_w0, rwkv_w_b, rwkv_a0, rwkv_a_b, rwkv_g_b, rwkv_k_k, rwkv_k_a, rwkv_r_k, rwkv_ln_w, rwkv_ln_b, w_out, norm_ffn2, ffn2_w1, ffn2_w2, final_norm):
    p = dict(norm_ffn1=norm_ffn1, ffn1_w1=ffn1_w1, ffn1_w2=ffn1_w2, norm_mix=norm_mix, w_in=w_in,
             mla_q_a_norm=mla_q_a_norm, mla_w_uq=mla_w_uq, mla_kv_a_norm=mla_kv_a_norm,
             mla_w_ukv=mla_w_ukv, mla_q_norm=mla_q_norm, mla_k_norm=mla_k_norm,
             lru_conv_w=lru_conv_w, lru_conv_b=lru_conv_b, lru_wa=lru_wa, lru_ba=lru_ba,
             lru_wx=lru_wx, lru_bx=lru_bx, lru_lambda=lru_lambda, gdn_conv_w=gdn_conv_w,
             gdn_a_log=gdn_a_log, gdn_dt_bias=gdn_dt_bias, gdn_o_norm=gdn_o_norm, rwkv_mu=rwkv_mu,
             rwkv_w0=rwkv_w0, rwkv_w_b=rwkv_w_b, rwkv_a0=rwkv_a0, rwkv_a_b=rwkv_a_b, rwkv_g_b=rwkv_g_b,
             rwkv_k_k=rwkv_k_k, rwkv_k_a=rwkv_k_a, rwkv_r_k=rwkv_r_k, rwkv_ln_w=rwkv_ln_w,
             rwkv_ln_b=rwkv_ln_b, w_out=w_out, norm_ffn2=norm_ffn2, ffn2_w1=ffn2_w1, ffn2_w2=ffn2_w2)
    depth = w_in.shape[0]
    lws = [_prep_layer_weights(p, l) for l in range(depth)]
    final_g = final_norm.reshape(1, -1).astype(F32)
    d = x_prompt.shape[-1]

    bp, seq, _ = x_prompt.shape
    tp = N_META + seq
    cfg = _group_config(bp, tp)
    lp = cfg["lp"]
    x0 = jnp.concatenate([jnp.broadcast_to(meta_tokens.astype(F32)[None], (bp, N_META, d)), x_prompt,
                          jnp.zeros((bp, lp - tp, d), F32)], axis=1)
    cfg["rope"] = tuple(jnp.tile(t, (bp, 1)) for t in _rope_tables(jnp.arange(lp)))
    zero = dict(mla=None, lru_conv=jnp.zeros((bp, 8, MIXW), F32), lru_h=jnp.zeros((bp, 1, MIXW), F32),
                gdn_conv=jnp.zeros((bp, 8, 3 * MIXW), F32), gdn_s=jnp.zeros((bp, MIXW, MIXW), F32),
                rwkv_shift=jnp.zeros((bp, 8, D_W), F32), rwkv_s=jnp.zeros((bp, MIXW, MIXW), F32))
    yp, p_new = _run_group(x0, lws, final_g, [zero] * depth, cfg)

    bs, ts, _ = x_sample.shape
    past = cache_mla_ckv.shape[2]
    cfg_s = _group_config(bs, ts)
    cfg_s["rope"] = tuple(jnp.tile(t, (bs, 1)) for t in _rope_tables(past + jnp.arange(ts)))
    st_s = [dict(mla=(cache_mla_ckv, cache_mla_kpe), lru_conv=_hist8(state_lru_conv[l]),
                 lru_h=state_lru_h[l][:, None].astype(F32), gdn_conv=_hist8(state_gdn_conv[l]),
                 gdn_s=_embed_state(state_gdn_s[l]), rwkv_shift=_hist8(state_rwkv_shift[l][:, None]),
                 rwkv_s=_embed_state(state_rwkv_s[l])) for l in range(depth)]
    ys, s_new = _run_group(x_sample, lws, final_g, st_s, cfg_s)
    return (yp[:, N_META:tp], ys) + tuple(p_new) + tuple(s_new)
```

```python
import functools
import math

import jax
import jax.numpy as jnp
import numpy as np
from jax import lax
from jax.experimental import pallas as pl
from jax.experimental.pallas import tpu as pltpu

F32 = jnp.float32
BF16 = jnp.bfloat16
HIGHEST = lax.Precision.HIGHEST

EPS = 1e-6
N_META = 16
CHUNK = 64
CONV_W = 4
HEADS = 4
HD = 64
MIXW = HEADS * HD
MLA_NOPE = 64
MLA_ROPE = 32
MLA_QK = MLA_NOPE + MLA_ROPE
MLA_V = 64
MLA_PAD = 128
ROPE_THETA = 10000.0
LRU_C = 8.0
RWKV_GN_EPS = 64e-5
RWKV_DECAY_SCALE = -0.606531

VMEM_LIMIT = 56 * 1024 * 1024
MXU_N = 256

A_W = 640
B_W = 512
C_W = 1536
D_W = 1024


def _bdot(a, b):
    return jnp.dot(a.astype(BF16), b.astype(BF16), preferred_element_type=F32)


def _bdot_nt(a, b):
    return lax.dot_general(a.astype(BF16), b.astype(BF16), (((1,), (1,)), ((), ())),
                           preferred_element_type=F32)


def _bdot_tn(a, b):
    return lax.dot_general(a.astype(BF16), b.astype(BF16), (((0,), (0,)), ((), ())),
                           preferred_element_type=F32)


def _split3(x):
    x1 = x.astype(BF16)
    r1 = x - x1.astype(F32)
    x2 = r1.astype(BF16)
    r2 = r1 - x2.astype(F32)
    return x1, x2, r2.astype(BF16)


def _xdot_r01(x, m01):
    x1 = x.astype(BF16)
    x2 = (x - x1.astype(F32)).astype(BF16)
    d = functools.partial(jnp.dot, preferred_element_type=F32)
    return d(x1, m01) + d(x2, m01)


def _xdot_l01(m01, x):
    x1, x2, x3 = _split3(x)
    d = functools.partial(jnp.dot, preferred_element_type=F32)
    return d(m01, x1) + d(m01, x2) + d(m01, x3)


def _sigmoid(x):
    return 0.5 * jnp.tanh(0.5 * x) + 0.5


def _silu(x):
    return x * _sigmoid(x)


def _softplus(x):
    return jnp.maximum(x, 0.0) + jnp.log1p(jnp.exp(-jnp.abs(x)))


def _gelu_tanh(x):
    return 0.5 * x * (1.0 + jnp.tanh(0.7978845608028654 * (x + 0.044715 * (x * x * x))))


def _rms_rows(x, g):
    return x * lax.rsqrt(jnp.mean(x * x, axis=-1, keepdims=True) + EPS) * g


def _block_diag_mask():
    r = lax.broadcasted_iota(jnp.int32, (MIXW, MIXW), 0) >> 6
    c = lax.broadcasted_iota(jnp.int32, (MIXW, MIXW), 1) >> 6
    return r == c


def _embed(x, bd01):
    c = x.shape[0]
    x = x.astype(BF16)
    if c < HD:
        x = jnp.concatenate([x, jnp.zeros((HD - c, MIXW), BF16)], axis=0)
    return jnp.concatenate([x] * HEADS, axis=0) * bd01


def _mm(a, b):
    return jnp.dot(a.astype(BF16), b, preferred_element_type=F32)


def _mm_nt(a, b):
    return lax.dot_general(a.astype(BF16), b, (((1,), (1,)), ((), ())), preferred_element_type=F32)


def _head_masks(c):
    row = lax.broadcasted_iota(jnp.int32, (c, MIXW), 0)
    col = lax.broadcasted_iota(jnp.int32, (c, MIXW), 1) & (HD - 1)
    return row >= col, row > col, row == col


def _tril01(c):
    r = lax.broadcasted_iota(jnp.int32, (c, c), 0)
    k = lax.broadcasted_iota(jnp.int32, (c, c), 1)
    return jnp.where(r >= k, 1.0, 0.0).astype(BF16)


def _inv_unit_lower(lms, eye4, bd01):
    c = lms[0].shape[0]
    levels = int(math.log2(c))
    eye = jnp.where(eye4, 1.0, 0.0)
    n = [-x for x in lms]
    t = [eye + x for x in n]
    if levels < 2:
        return t
    p = [_mm(x, _embed(x, bd01)) for x in n]
    for _ in range(levels - 2):
        r = [_mm(jnp.concatenate([pi, ti], axis=0), _embed(pi, bd01)) for pi, ti in zip(p, t)]
        p = [x[0:c] for x in r]
        t = [ti + x[c:2 * c] for ti, x in zip(t, r)]
    return [ti + _mm(ti, _embed(pi, bd01)) for pi, ti in zip(p, t)]


def _load_state(st_ref, s0_ref, s):
    st_ref[s] = jnp.zeros((MIXW, MIXW), F32)
    for h in range(HEADS):
        st_ref[s, h * HD:(h + 1) * HD, h * HD:(h + 1) * HD] = s0_ref[s, h].astype(F32)


def _store_state(so_ref, st_ref, s):
    for h in range(HEADS):
        so_ref[s, h] = st_ref[s, h * HD:(h + 1) * HD, h * HD:(h + 1) * HD]


def _shift_rows(x, s, fill):
    rolled = pltpu.roll(x, s, 0)
    row = lax.broadcasted_iota(jnp.int32, x.shape, 0)
    return jnp.where(row >= s, rolled, fill)


def _causal_conv(xext_ref, w_ref, tt):
    acc = None
    for j in range(CONV_W):
        term = xext_ref[pl.ds(8 - (CONV_W - 1) + j, tt), :] * w_ref[j:j + 1, :]
        acc = term if acc is None else acc + term
    return acc


def _const_spec(shape):
    nd = len(shape)
    return pl.BlockSpec(shape, lambda *_: (0,) * nd, pipeline_mode=pl.Buffered(1))


def _token_kernel(*refs, has_pre, has_post, has_final, d_ff, ff_chunk):
    it = iter(refs)
    x_ref = next(it)
    if has_pre:
        y_refs = [next(it) for _ in range(4)]
        wo_refs = [next(it) for _ in range(4)]
    gffn_ref, w1_ref, w2_ref = next(it), next(it), next(it)
    if has_post:
        gmix_ref, win_ref = next(it), next(it)
    if has_final:
        gfin_ref = next(it)
    xo_ref = next(it)
    if has_post:
        col_refs = [next(it) for _ in range(4)]

    x = x_ref[...]
    if has_pre:
        for y_ref, wo_ref in zip(y_refs, wo_refs):
            x = x + jnp.dot(y_ref[...], wo_ref[...], preferred_element_type=F32)
    h = _rms_rows(x, gffn_ref[...]).astype(BF16)
    acc = None
    for lo, hi in ((0, ff_chunk), (ff_chunk, d_ff)):
        gate = jnp.dot(h, w1_ref[:, lo:hi], preferred_element_type=F32)
        up = jnp.dot(h, w1_ref[:, d_ff + lo:d_ff + hi], preferred_element_type=F32)
        act = (_silu(gate) * up).astype(BF16)
        part = jnp.dot(act, w2_ref[lo:hi, :], preferred_element_type=F32)
        acc = part if acc is None else acc + part
    x = x + 0.5 * acc
    if has_final:
        xo_ref[...] = _rms_rows(x, gfin_ref[...])
    else:
        xo_ref[...] = x
    if has_post:
        hm = _rms_rows(x, gmix_ref[...]).astype(BF16)
        off = 0
        for c_ref in col_refs:
            wd = c_ref.shape[-1]
            c_ref[...] = jnp.dot(hm, win_ref[:, off:off + wd], preferred_element_type=F32)
            off += wd


def _token_call(x, lw, *, pre=None, post=False, final_g=None, tm):
    n, d = x.shape
    d_ff = lw["w2"].shape[0]
    ff_chunk = -(-(d_ff // MXU_N) // 2) * MXU_N
    row = lambda w: pl.BlockSpec((tm, w), lambda i: (i, 0))
    args, specs = [x], [row(d)]
    if pre is not None:
        ys, wos = pre
        for y in ys:
            args.append(y)
            specs.append(row(y.shape[1]))
        for w in wos:
            args.append(w)
            specs.append(_const_spec(w.shape))
    for name in ("g_ffn", "w1", "w2"):
        args.append(lw[name])
        specs.append(_const_spec(lw[name].shape))
    if post:
        for name in ("g_mix", "w_in"):
            args.append(lw[name])
            specs.append(_const_spec(lw[name].shape))
    if final_g is not None:
        args.append(final_g)
        specs.append(_const_spec(final_g.shape))
    out_shape = [jax.ShapeDtypeStruct((n, d), F32)]
    out_specs = [row(d)]
    if post:
        for wd in (A_W, B_W, C_W, D_W):
            out_shape.append(jax.ShapeDtypeStruct((n, wd), F32))
            out_specs.append(row(wd))
    body = functools.partial(_token_kernel, has_pre=pre is not None, has_post=post,
                             has_final=final_g is not None, d_ff=d_ff, ff_chunk=ff_chunk)
    return pl.pallas_call(
        body, grid=(n // tm,), in_specs=specs, out_specs=out_specs, out_shape=out_shape,
        compiler_params=pltpu.CompilerParams(dimension_semantics=("parallel",),
                                             vmem_limit_bytes=VMEM_LIMIT),
        name="token_block")(*args)


def _mla_prep_kernel(ca_ref, cos_ref, sin_ref, gqa_ref, wq_ref, gkva_ref, wkv_ref, gq_ref, gk_ref,
                     q_ref, k_ref, v_ref, ckv_ref, kpe_ref):
    cs = cos_ref[...]
    sn = sin_ref[...]
    hw = HEADS * MLA_PAD
    cqn = _rms_rows(ca_ref[:, 0:256], gqa_ref[...])
    qq = _bdot(cqn, wq_ref[...])
    scale = math.log2(math.e) / math.sqrt(MLA_QK)
    for h in range(HEADS):
        lo = h * MLA_PAD
        blk = qq[:, lo:lo + MLA_PAD] * cs + qq[:, hw + lo:hw + lo + MLA_PAD] * sn
        ms = jnp.sum(blk * blk, axis=-1, keepdims=True) * (1.0 / MLA_QK)
        q_ref[:, lo:lo + MLA_PAD] = (blk * lax.rsqrt(ms + EPS) * (gq_ref[...] * scale)).astype(BF16)
    ckv = _rms_rows(ca_ref[:, 256:384], gkva_ref[...])
    ckv_ref[...] = ckv
    kpe = ca_ref[:, 384:512] * cs + ca_ref[:, 512:640] * sn
    kpe_ref[...] = kpe[:, 0:MLA_ROPE]
    kv = _bdot(ckv, wkv_ref[...])
    ones_lane = lax.broadcasted_iota(jnp.int32, (kv.shape[0], MLA_PAD), 1) == MLA_V
    for h in range(HEADS):
        lo = h * MLA_PAD
        kb = kv[:, lo:lo + MLA_PAD] + kpe
        ms = jnp.sum(kb * kb, axis=-1, keepdims=True) * (1.0 / MLA_QK)
        k_ref[:, lo:lo + MLA_PAD] = (kb * lax.rsqrt(ms + EPS) * gk_ref[...]).astype(BF16)
        v_ref[:, lo:lo + MLA_PAD] = jnp.where(ones_lane, 1.0, kv[:, hw + lo:hw + lo + MLA_PAD]).astype(BF16)


def _mla_prep_call(ca, cos_t, sin_t, lw, *, tm):
    n = ca.shape[0]
    hw = HEADS * MLA_PAD
    row = lambda w: pl.BlockSpec((tm, w), lambda i: (i, 0))
    consts = [lw["mla_gqa"], lw["mla_wq"], lw["mla_gkva"], lw["mla_wkv"], lw["mla_gq"], lw["mla_gk"]]
    return pl.pallas_call(
        _mla_prep_kernel, grid=(n // tm,),
        in_specs=[row(A_W), row(MLA_PAD), row(MLA_PAD)] + [_const_spec(c.shape) for c in consts],
        out_specs=[row(hw), row(hw), row(hw), row(128), row(MLA_ROPE)],
        out_shape=[jax.ShapeDtypeStruct((n, hw), BF16)] * 3
        + [jax.ShapeDtypeStruct((n, 128), F32), jax.ShapeDtypeStruct((n, MLA_ROPE), F32)],
        compiler_params=pltpu.CompilerParams(dimension_semantics=("parallel",),
                                             vmem_limit_bytes=VMEM_LIMIT),
        name="mla_prep")(ca, cos_t, sin_t, *consts)


def _softmax_step(carry, s, vblk):
    m, acc = carry
    m_new = jnp.maximum(m, jnp.max(s, axis=-1, keepdims=True))
    p = jnp.exp2(s - m_new)
    acc = jnp.exp2(m - m_new) * acc + jnp.dot(p.astype(BF16), vblk, preferred_element_type=F32)
    return m_new, acc


def _softmax_init(t):
    return jnp.full((t, 1), -1e30, F32), jnp.zeros((t, MLA_PAD), F32)


def _softmax_finish(acc):
    return (acc / acc[:, MLA_V:MLA_V + 1]).astype(BF16)


def _mla_prompt_kernel(q_ref, k_ref, v_ref, o_ref, *, tq, tk, n_kb):
    q0 = pl.program_id(1) * tq
    first_chunk = (q0 - N_META) >> 6
    last_chunk = (q0 + tq - 1 - N_META) >> 6
    n_full = (N_META + CHUNK * (first_chunk + 1)) // tk
    nkb = jnp.minimum((N_META + CHUNK * (last_chunk + 1) + tk - 1) // tk, n_kb)
    qchunk = (q0 + lax.broadcasted_iota(jnp.int32, (tq, 1), 0) - N_META) >> 6
    hs = range(HEADS)
    qs = [q_ref[:, h * MLA_PAD:(h + 1) * MLA_PAD] for h in hs]

    def body(kb, state, masked):
        k0 = pl.multiple_of(kb * tk, tk)
        s = [lax.dot_general(qs[h], k_ref[pl.ds(k0, tk), h * MLA_PAD:(h + 1) * MLA_PAD],
                             (((1,), (1,)), ((), ())), preferred_element_type=F32) for h in hs]
        if masked:
            visible = ((k0 + lax.broadcasted_iota(jnp.int32, (1, tk), 1) - N_META) >> 6) <= qchunk
            s = [jnp.where(visible, x, -1e30) for x in s]
        m_new = [jnp.maximum(state[h][0], jnp.max(s[h], axis=-1, keepdims=True)) for h in hs]
        p = [jnp.exp2(s[h] - m_new[h]).astype(BF16) for h in hs]
        pv = [jnp.dot(p[h], v_ref[pl.ds(k0, tk), h * MLA_PAD:(h + 1) * MLA_PAD], preferred_element_type=F32)
              for h in hs]
        return tuple((m_new[h], jnp.exp2(state[h][0] - m_new[h]) * state[h][1] + pv[h]) for h in hs)

    state = tuple(_softmax_init(tq) for _ in hs)
    state = lax.fori_loop(0, n_full, functools.partial(body, masked=False), state)
    state = lax.fori_loop(n_full, nkb, functools.partial(body, masked=True), state)
    for h in hs:
        o_ref[:, h * MLA_PAD:(h + 1) * MLA_PAD] = _softmax_finish(state[h][1])


def _mla_prompt_call(q, k, v, *, tq, tk):
    b, lp, hw = q.shape
    body = functools.partial(_mla_prompt_kernel, tq=tq, tk=tk, n_kb=lp // tk)
    return pl.pallas_call(
        body, grid=(b, lp // tq),
        in_specs=[pl.BlockSpec((None, tq, hw), lambda i, j: (i, j, 0)),
                  pl.BlockSpec((None, lp, hw), lambda i, j: (i, 0, 0)),
                  pl.BlockSpec((None, lp, hw), lambda i, j: (i, 0, 0))],
        out_specs=pl.BlockSpec((None, tq, hw), lambda i, j: (i, j, 0)),
        out_shape=jax.ShapeDtypeStruct((b, lp, hw), BF16),
        compiler_params=pltpu.CompilerParams(dimension_semantics=("parallel", "arbitrary"),
                                             vmem_limit_bytes=VMEM_LIMIT),
        name="mla_prompt_attn")(q, k, v)


def _mla_sample_kernel(q_ref, kn_ref, vn_ref, cckv_ref, ckpet_ref, wkt_ref, wv_ref, gk_ref, o_ref):
    t = q_ref.shape[0]
    hs = range(HEADS)
    blk = [slice(h * MLA_PAD, (h + 1) * MLA_PAD) for h in hs]
    ckv = cckv_ref[...].astype(BF16)
    knt = lax.dot_general(wkt_ref[...], ckv, (((1,), (1,)), ((), ())), preferred_element_type=F32)
    kpet = ckpet_ref[...]
    ssq_pe = jnp.sum(kpet * kpet, axis=0, keepdims=True)
    kpet16 = kpet.astype(BF16)
    zpad = jnp.zeros((MLA_PAD - MLA_QK, kpet.shape[1]), BF16)
    kts = [knt[h * MLA_NOPE:(h + 1) * MLA_NOPE, :] for h in hs]
    rk = [lax.rsqrt((jnp.sum(x * x, axis=0, keepdims=True) + ssq_pe) * (1.0 / MLA_QK) + EPS) for x in kts]
    qg = [(q_ref[:, blk[h]].astype(F32) * gk_ref[...]).astype(BF16) for h in hs]
    s_c = [jnp.dot(qg[h], jnp.concatenate([kpet16, kts[h].astype(BF16), zpad], axis=0),
                   preferred_element_type=F32) * rk[h] for h in hs]
    s_n = [lax.dot_general(q_ref[:, blk[h]], kn_ref[:, blk[h]], (((1,), (1,)), ((), ())),
                           preferred_element_type=F32) for h in hs]
    m = [jnp.maximum(jnp.max(s_c[h], axis=-1, keepdims=True), jnp.max(s_n[h], axis=-1, keepdims=True))
         for h in hs]
    p_c = [jnp.exp2(s_c[h] - m[h]) for h in hs]
    p_n = [jnp.exp2(s_n[h] - m[h]) for h in hs]
    den = [jnp.sum(p_c[h], axis=-1, keepdims=True) + jnp.sum(p_n[h], axis=-1, keepdims=True) for h in hs]
    lat = jnp.dot(jnp.concatenate([x.astype(BF16) for x in p_c], axis=0), ckv, preferred_element_type=F32)
    for h in hs:
        o = jnp.dot(lat[h * t:(h + 1) * t].astype(BF16), wv_ref[:, blk[h]], preferred_element_type=F32)
        o = o + jnp.dot(p_n[h].astype(BF16), vn_ref[:, blk[h]], preferred_element_type=F32)
        o_ref[:, blk[h]] = (o / den[h]).astype(BF16)


def _mla_sample_call(q, kn, vn, cache_ckv, cache_kpe_t, layer, lw):
    b, t, hw = q.shape
    past = cache_ckv.shape[2]
    seq = lambda: pl.BlockSpec((None, t, hw), lambda i: (i, 0, 0))
    consts = [lw["mla_wkt"], lw["mla_wv"], lw["mla_gk"]]
    return pl.pallas_call(
        _mla_sample_kernel, grid=(b,),
        in_specs=[seq(), seq(), seq(),
                  pl.BlockSpec((None, None, past, cache_ckv.shape[3]), lambda i: (layer, i, 0, 0)),
                  pl.BlockSpec((None, None, cache_kpe_t.shape[2], past), lambda i: (layer, i, 0, 0))]
        + [_const_spec(c.shape) for c in consts],
        out_specs=seq(),
        out_shape=jax.ShapeDtypeStruct((b, t, hw), BF16),
        compiler_params=pltpu.CompilerParams(dimension_semantics=("arbitrary",),
                                             vmem_limit_bytes=VMEM_LIMIT),
        name="mla_sample_attn")(q, kn, vn, cache_ckv, cache_kpe_t, *consts)


def _lru_kernel(cb_ref, buf_ref, h0_ref, cw_ref, cbias_ref, wa_ref, ba_ref, wx_ref, bx_ref, lam_ref,
                y_ref, nbuf_ref, hl_ref, xext, hcar, *, tt, t_valid):
    ti = pl.program_id(1)

    @pl.when(ti == 0)
    def _():
        xext[0:8, :] = buf_ref[...]
        hcar[...] = h0_ref[...]

    @pl.when(ti > 0)
    def _():
        xext[0:8, :] = xext[tt:tt + 8, :]

    xext[8:tt + 8, :] = cb_ref[:, 0:MIXW]
    xc = _causal_conv(xext, cw_ref, tt) + cbias_ref[...]
    r = _sigmoid(_bdot(xc, wa_ref[...]) + ba_ref[...])
    i = _sigmoid(_bdot(xc, wx_ref[...]) + bx_ref[...])
    log_a = -LRU_C * r * _softplus(-lam_ref[...])
    pos = ti * tt + lax.broadcasted_iota(jnp.int32, (tt, 1), 0)
    log_a = jnp.where(pos < t_valid, log_a, 0.0)
    a = jnp.exp(log_a)
    b = jnp.sqrt(-jnp.tanh(log_a) * (a * a + 1.0)) * (i * xc)
    s = 1
    while s < tt:
        b = a * _shift_rows(b, s, 0.0) + b
        a = a * _shift_rows(a, s, 1.0)
        s *= 2
    h = a * hcar[...] + b
    hcar[...] = h[tt - 1:tt, :]
    y_ref[...] = (h * _gelu_tanh(cb_ref[:, MIXW:2 * MIXW])).astype(BF16)

    last = t_valid - 1

    @pl.when(ti == last // tt)
    def _():
        r0 = last % tt
        hl_ref[...] = h[r0:r0 + 1, :]
        nbuf_ref[...] = xext[pl.ds(8 + r0 - (CONV_W - 2), CONV_W - 1), :]


def _lru_call(cb, buf8, h0, lw, *, tt, t_valid):
    b, lp, _ = cb.shape
    consts = [lw["lru_cw"], lw["lru_cb"], lw["lru_wa"], lw["lru_ba"], lw["lru_wx"], lw["lru_bx"],
              lw["lru_lam"]]
    body = functools.partial(_lru_kernel, tt=tt, t_valid=t_valid)
    per_b = lambda s: pl.BlockSpec((None,) + s, lambda i, j: (i, 0, 0))
    return pl.pallas_call(
        body, grid=(b, lp // tt),
        in_specs=[pl.BlockSpec((None, tt, B_W), lambda i, j: (i, j, 0)), per_b((8, MIXW)),
                  per_b((1, MIXW))] + [_const_spec(c.shape) for c in consts],
        out_specs=[pl.BlockSpec((None, tt, MIXW), lambda i, j: (i, j, 0)),
                   per_b((CONV_W - 1, MIXW)), per_b((1, MIXW))],
        out_shape=[jax.ShapeDtypeStruct((b, lp, MIXW), BF16),
                   jax.ShapeDtypeStruct((b, CONV_W - 1, MIXW), F32),
                   jax.ShapeDtypeStruct((b, 1, MIXW), F32)],
        scratch_shapes=[pltpu.VMEM((tt + 8, MIXW), F32), pltpu.VMEM((1, MIXW), F32)],
        compiler_params=pltpu.CompilerParams(dimension_semantics=("parallel", "arbitrary"),
                                             vmem_limit_bytes=VMEM_LIMIT),
        name="rglru")(cb, buf8, h0, *consts)


def _gdn_kernel(cc_ref, buf_ref, s0_ref, cw_ref, alog_ref, dtb_ref, gon_ref,
                y_ref, nbuf_ref, so_ref, xext, q_s, k_s, v_s, b_s, g_s, o_s, u_s, a_s, w_s, e_s, l_s,
                m_s, h_s, n_s, st_ref, *, nb, n_tiles, tt, c, t_valid):
    ti = pl.program_id(1)
    qkw = 3 * MIXW
    bd = _block_diag_mask()
    bd01 = jnp.where(bd, 1.0, 0.0).astype(BF16)
    pos = ti * tt + lax.broadcasted_iota(jnp.int32, (tt, 1), 0)
    valid = pos < t_valid

    for s in range(nb):
        xe = xext.at[s]
        sr = slice(s * tt, (s + 1) * tt)

        @pl.when(ti == 0)
        def _():
            xe[0:8, :] = buf_ref[s]
            _load_state(st_ref, s0_ref, s)

        @pl.when(ti > 0)
        def _():
            xe[0:8, :] = xe[tt:tt + 8, :]

        xe[8:tt + 8, :] = cc_ref[s, :, 0:qkw]
        xs = _silu(_causal_conv(xe, cw_ref, tt))
        q = xs[:, 0:MIXW]
        k = xs[:, MIXW:2 * MIXW]
        q_s[sr, :] = q * lax.rsqrt(_xdot_r01(q * q, bd01) + EPS) * (HD ** -0.5)
        k_s[sr, :] = k * lax.rsqrt(_xdot_r01(k * k, bd01) + EPS)
        v_s[sr, :] = xs[:, 2 * MIXW:3 * MIXW]
        a_in = cc_ref[s, :, qkw + MIXW:qkw + 2 * MIXW]
        b_in = cc_ref[s, :, qkw + 2 * MIXW:qkw + 3 * MIXW]
        b_s[sr, :] = jnp.where(valid, _sigmoid(b_in), 0.0)
        g_s[sr, :] = jnp.where(valid, -jnp.exp(alog_ref[...]) * _softplus(a_in + dtb_ref[...]), 0.0)

    tril4, strict4, eye4 = _head_masks(c)
    tril01 = _tril01(c)
    n_chunks = tt // c

    ch = range(nb * n_chunks)
    rs = [slice(ci * c, (ci + 1) * c) for ci in ch]
    gc = [_xdot_l01(tril01, g_s[r, :]) for r in rs]
    gct = [jnp.sum(jnp.where(eye4, x, 0.0), axis=0, keepdims=True) for x in gc]
    decay = [jnp.where(tril4, jnp.exp(jnp.where(tril4, gc[i] - gct[i], 0.0)), 0.0) for i in ch]
    kc = [k_s[r, :] for r in rs]
    beta = [b_s[r, :] for r in rs]
    kb = [kc[i] * beta[i] for i in ch]
    sc = [_mm_nt(jnp.concatenate([kb[i], q_s[rs[i], :]], axis=0), _embed(kc[i], bd01)) for i in ch]
    tinv = _inv_unit_lower([jnp.where(strict4, sc[i][0:c] * decay[i], 0.0) for i in ch], eye4, bd01)
    attn = [jnp.where(tril4, sc[i][c:2 * c] * decay[i], 0.0).astype(BF16) for i in ch]
    egc = [jnp.exp(x) for x in gc]
    uw = [_mm(tinv[i], jnp.concatenate([_embed(v_s[rs[i], :] * beta[i], bd01),
                                        _embed(kb[i] * egc[i], bd01)], axis=1)) for i in ch]
    glast = [x[c - 1:c, :] for x in gc]
    kd = [(kc[i] * jnp.exp(glast[i] - gc[i])).astype(BF16) for i in ch]
    an = [lax.dot_general(kd[i], jnp.concatenate([uw[i][:, MIXW:], uw[i][:, :MIXW]], axis=1).astype(BF16),
                          (((0,), (0,)), ((), ())), preferred_element_type=F32) for i in ch]
    for i in ch:
        u_s[rs[i], :] = uw[i][:, 0:MIXW]
        w_s[rs[i], :] = uw[i][:, MIXW:2 * MIXW].astype(BF16)
        e_s[rs[i], :] = (q_s[rs[i], :] * egc[i]).astype(BF16)
        a_s[rs[i], :] = attn[i]
        m_s[i] = jnp.where(bd, an[i][:, 0:MIXW], 0.0).astype(BF16)
        n_s[i] = jnp.where(bd, an[i][:, MIXW:2 * MIXW], 0.0)
        l_s[i:i + 1, :] = jnp.exp(glast[i])

    for ci in range(n_chunks):
        for s in range(nb):
            i = s * n_chunks + ci
            s_bd = st_ref[s]
            s16 = s_bd.astype(BF16)
            h_s[i] = s16
            st_ref[s] = s_bd * l_s[i:i + 1, :] + n_s[i] - jnp.dot(m_s[i], s16, preferred_element_type=F32)

    r = [jnp.dot(jnp.concatenate([w_s[rs[i], :], e_s[rs[i], :]], axis=0), h_s[i], preferred_element_type=F32)
         for i in ch]
    v_new = [u_s[rs[i], :] - r[i][0:c] for i in ch]
    ov = [jnp.dot(a_s[rs[i], :], _embed(v_new[i], bd01), preferred_element_type=F32) for i in ch]
    for i in ch:
        o_s[rs[i], :] = r[i][c:2 * c] + ov[i]

    last = t_valid - 1
    for s in range(nb):
        o = o_s[s * tt:(s + 1) * tt, :]
        ms = _xdot_r01(o * o, bd01) * (1.0 / HD)
        z = cc_ref[s, :, qkw:qkw + MIXW]
        y_ref[s] = (o * lax.rsqrt(ms + EPS) * gon_ref[...] * _silu(z)).astype(BF16)

        @pl.when(ti == last // tt)
        def _():
            nbuf_ref[s] = xext[s, pl.ds(8 + last % tt - (CONV_W - 2), CONV_W - 1), :]

        @pl.when(ti == n_tiles - 1)
        def _():
            _store_state(so_ref, st_ref, s)


def _gdn_call(cc, buf8, s0, lw, *, nb, tt, c, t_valid):
    b, lp, _ = cc.shape
    qkw = 3 * MIXW
    consts = [lw["gdn_cw"], lw["gdn_alog"], lw["gdn_dtb"], lw["gdn_gon"]]
    body = functools.partial(_gdn_kernel, nb=nb, n_tiles=lp // tt, tt=tt, c=c, t_valid=t_valid)
    per_b = lambda s: pl.BlockSpec((nb,) + s, lambda i, j: (i,) + (0,) * len(s))
    n_ch = nb * (tt // c)
    return pl.pallas_call(
        body, grid=(b // nb, lp // tt),
        in_specs=[pl.BlockSpec((nb, tt, C_W), lambda i, j: (i, j, 0)), per_b((8, qkw)),
                  per_b((HEADS, HD, HD))] + [_const_spec(x.shape) for x in consts],
        out_specs=[pl.BlockSpec((nb, tt, MIXW), lambda i, j: (i, j, 0)),
                   per_b((CONV_W - 1, qkw)), per_b((HEADS, HD, HD))],
        out_shape=[jax.ShapeDtypeStruct((b, lp, MIXW), BF16),
                   jax.ShapeDtypeStruct((b, CONV_W - 1, qkw), F32),
                   jax.ShapeDtypeStruct((b, HEADS, HD, HD), F32)],
        scratch_shapes=[pltpu.VMEM((nb, tt + 8, qkw), F32)] + [pltpu.VMEM((nb * tt, MIXW), F32)] * 7
        + [pltpu.VMEM((nb * tt, MIXW), BF16)] * 3 + [pltpu.VMEM((-(-n_ch // 8) * 8, MIXW), F32),
                                                     pltpu.VMEM((n_ch, MIXW, MIXW), BF16),
                                                     pltpu.VMEM((n_ch, MIXW, MIXW), BF16),
                                                     pltpu.VMEM((n_ch, MIXW, MIXW), F32),
                                                     pltpu.VMEM((nb, MIXW, MIXW), F32)],
        compiler_params=pltpu.CompilerParams(dimension_semantics=("parallel", "arbitrary"),
                                             vmem_limit_bytes=VMEM_LIMIT),
        name="gated_delta")(cc, buf8, s0, *consts)


def _rwkv_kernel(cd_ref, prev_ref, s0_ref, mu_ref, w0_ref, wb_ref, a0_ref, ab_ref, gb_ref, kk_ref,
                 ka_ref, rk_ref, lnw_ref, lnb_ref,
                 y_ref, shift_ref, so_ref, xext, r_s, k_s, v_s, n_s, b_s, w_s, o_s, xv_s,
                 t_s, a_s, lhs_s, p_s, g_s, h_s, d_s, rk_s, gt_s, st_ref, *, nb, n_tiles, tt, c, t_valid):
    ti = pl.program_id(1)
    pos = ti * tt + lax.broadcasted_iota(jnp.int32, (tt, 1), 0)
    valid = pos < t_valid
    bd = _block_diag_mask()
    bd01 = jnp.where(bd, 1.0, 0.0).astype(BF16)

    for s in range(nb):
        xe = xext.at[s]
        sr = slice(s * tt, (s + 1) * tt)

        @pl.when(ti == 0)
        def _():
            xe[0:8, :] = prev_ref[s]
            _load_state(st_ref, s0_ref, s)

        @pl.when(ti > 0)
        def _():
            xe[0:8, :] = xe[tt:tt + 8, :]

        x = cd_ref[s]
        xe[8:tt + 8, :] = x
        xm = x + (xe[pl.ds(7, tt), :] - x) * mu_ref[...]
        r = xm[:, 0:MIXW]
        k = xm[:, MIXW:2 * MIXW]
        lo_blk = xm[:, 3 * MIXW:3 * MIXW + 128]
        logw = RWKV_DECAY_SCALE * _sigmoid(w0_ref[...] + _bdot(jnp.tanh(lo_blk), wb_ref[...]))
        a = _sigmoid(a0_ref[...] + _bdot(lo_blk, ab_ref[...]))
        kkv = k * kk_ref[...]
        kkn = kkv * lax.rsqrt(_xdot_r01(kkv * kkv, bd01) + EPS)
        kkn = jnp.where(valid, kkn, 0.0)
        kmod = k * (1.0 + (a - 1.0) * ka_ref[...])
        r_s[sr, :] = r
        k_s[sr, :] = jnp.where(valid, kmod, 0.0)
        v_s[sr, :] = xm[:, 2 * MIXW:3 * MIXW]
        n_s[sr, :] = kkn
        b_s[sr, :] = kkn * a
        w_s[sr, :] = jnp.where(valid, logw, 0.0)
        rk_s[sr, :] = _xdot_r01(r * kmod * rk_ref[...], bd01)
        gt_s[sr, :] = _bdot(_sigmoid(xm[:, 3 * MIXW + 128:3 * MIXW + 256]), gb_ref[...])

    tril4, strict4, eye4 = _head_masks(c)
    tril01 = _tril01(c)

    n_chunks = tt // c

    ch = range(nb * n_chunks)
    rs = [slice(ci * c, (ci + 1) * c) for ci in ch]
    lw = [w_s[r, :] for r in rs]
    cum = [_xdot_l01(tril01, x) for x in lw]
    ecum = [jnp.exp(x) for x in cum]
    pin = [jnp.exp(-x) for x in cum]
    pc = [x[c - 1:c, :] for x in ecum]
    kinv = [k_s[rs[i], :] * pin[i] for i in ch]
    binv = [b_s[rs[i], :] * pin[i] for i in ch]
    nd = [n_s[rs[i], :] * jnp.exp(cum[i] - lw[i]) for i in ch]
    lhs = [jnp.concatenate([nd[i], r_s[rs[i], :] * ecum[i]], axis=0).astype(BF16) for i in ch]
    sk = [_mm_nt(lhs[i], _embed(kinv[i], bd01)) for i in ch]
    sb = [_mm_nt(lhs[i], _embed(binv[i], bd01)) for i in ch]
    tinv = _inv_unit_lower([jnp.where(strict4, sb[i][0:c], 0.0) for i in ch], eye4, bd01)
    av = [_mm(jnp.concatenate([jnp.where(strict4, sk[i][0:c], 0.0), jnp.where(tril4, sk[i][c:2 * c], 0.0)],
                              axis=0), _embed(v_s[rs[i], :], bd01)) for i in ch]
    tn = [_mm(tinv[i], jnp.concatenate([_embed(nd[i], bd01), _embed(av[i][0:c], bd01)], axis=1)) for i in ch]
    bp = [(binv[i] * pc[i]).astype(BF16) for i in ch]
    gh = [lax.dot_general(tn[i].astype(BF16), bp[i], (((0,), (0,)), ((), ())), preferred_element_type=F32)
          for i in ch]
    vk = [_bdot_tn(v_s[rs[i], :], kinv[i] * pc[i]) for i in ch]
    for i in ch:
        lhs_s[2 * i * c:2 * (i + 1) * c, :] = lhs[i]
        t_s[rs[i], :] = tinv[i].astype(BF16)
        a_s[rs[i], :] = jnp.where(tril4, sb[i][c:2 * c], 0.0).astype(BF16)
        xv_s[rs[i], :] = av[i][0:c]
        o_s[rs[i], :] = av[i][c:2 * c]
        p_s[i:i + 1, :] = pc[i]
        g_s[i] = jnp.where(bd, gh[i][0:MIXW], 0.0).astype(BF16)
        d_s[i] = jnp.where(bd, vk[i] - gh[i][MIXW:2 * MIXW], 0.0)

    for ci in range(n_chunks):
        for s in range(nb):
            i = s * n_chunks + ci
            st = st_ref[s]
            s16 = st.astype(BF16)
            h_s[i] = s16
            st_ref[s] = st * p_s[i:i + 1, :] + d_s[i] - jnp.dot(s16, g_s[i], preferred_element_type=F32)

    r2 = [lax.dot_general(lhs_s[2 * i * c:2 * (i + 1) * c, :], h_s[i], (((1,), (1,)), ((), ())),
                          preferred_element_type=F32) for i in ch]
    u = [jnp.dot(t_s[rs[i], :], _embed(r2[i][0:c] + xv_s[rs[i], :], bd01), preferred_element_type=F32)
         for i in ch]
    au = [jnp.dot(a_s[rs[i], :], _embed(u[i], bd01), preferred_element_type=F32) for i in ch]
    for i in ch:
        o_s[rs[i], :] = r2[i][c:2 * c] + o_s[rs[i], :] - au[i]

    last = t_valid - 1
    for s in range(nb):
        sr = slice(s * tt, (s + 1) * tt)
        o = o_s[sr, :]
        mean = _xdot_r01(o, bd01) * (1.0 / HD)
        d = o - mean
        var = _xdot_r01(d * d, bd01) * (1.0 / HD)
        o = d * lax.rsqrt(var + RWKV_GN_EPS) * lnw_ref[...] + lnb_ref[...]
        o = o + rk_s[sr, :] * v_s[sr, :]
        y_ref[s] = (o * gt_s[sr, :]).astype(BF16)

        @pl.when(ti == last // tt)
        def _():
            shift_ref[s] = xext[s, pl.ds(8 + last % tt, 1), :]

        @pl.when(ti == n_tiles - 1)
        def _():
            _store_state(so_ref, st_ref, s)


def _rwkv_call(cd, prev8, s0, lw, *, nb, tt, c, t_valid):
    b, lp, _ = cd.shape
    consts = [lw["rwkv_mu"], lw["rwkv_w0"], lw["rwkv_wb"], lw["rwkv_a0"], lw["rwkv_ab"], lw["rwkv_gb"],
              lw["rwkv_kk"], lw["rwkv_ka"], lw["rwkv_rk"], lw["rwkv_lnw"], lw["rwkv_lnb"]]
    body = functools.partial(_rwkv_kernel, nb=nb, n_tiles=lp // tt, tt=tt, c=c, t_valid=t_valid)
    per_b = lambda s: pl.BlockSpec((nb,) + s, lambda i, j: (i,) + (0,) * len(s))
    n_ch = nb * (tt // c)
    return pl.pallas_call(
        body, grid=(b // nb, lp // tt),
        in_specs=[pl.BlockSpec((nb, tt, D_W), lambda i, j: (i, j, 0)), per_b((8, D_W)),
                  per_b((HEADS, HD, HD))] + [_const_spec(x.shape) for x in consts],
        out_specs=[pl.BlockSpec((nb, tt, MIXW), lambda i, j: (i, j, 0)),
                   per_b((1, D_W)), per_b((HEADS, HD, HD))],
        out_shape=[jax.ShapeDtypeStruct((b, lp, MIXW), BF16),
                   jax.ShapeDtypeStruct((b, 1, D_W), F32),
                   jax.ShapeDtypeStruct((b, HEADS, HD, HD), F32)],
        scratch_shapes=[pltpu.VMEM((nb, tt + 8, D_W), F32)] + [pltpu.VMEM((nb * tt, MIXW), F32)] * 8
        + [pltpu.VMEM((nb * tt, MIXW), BF16)] * 2 + [pltpu.VMEM((2 * nb * tt, MIXW), BF16),
                                                     pltpu.VMEM((-(-n_ch // 8) * 8, MIXW), F32),
                                                     pltpu.VMEM((n_ch, MIXW, MIXW), BF16),
                                                     pltpu.VMEM((n_ch, MIXW, MIXW), BF16),
                                                     pltpu.VMEM((n_ch, MIXW, MIXW), F32)]
        + [pltpu.VMEM((nb * tt, MIXW), F32)] * 2 + [pltpu.VMEM((nb, MIXW, MIXW), F32)],
        compiler_params=pltpu.CompilerParams(dimension_semantics=("parallel", "arbitrary"),
                                             vmem_limit_bytes=VMEM_LIMIT),
        name="rwkv7")(cd, prev8, s0, *consts)


def _pad_cols(w, width):
    return jnp.pad(w, ((0, 0), (0, width - w.shape[1])))


def _mla_head_cols(w, n_heads, per_head, pieces):
    blocks = []
    for h in range(n_heads):
        cols = [sign * w[:, h * per_head + lo:h * per_head + hi] for lo, hi, sign in pieces]
        blocks.append(_pad_cols(jnp.concatenate(cols, axis=1), MLA_PAD))
    return jnp.concatenate(blocks, axis=1)


def _block_diag_heads(w):
    eye = jnp.eye(HEADS, dtype=w.dtype)[:, None, :, None]
    return (w[:, :, None, :] * eye).reshape(MIXW, MIXW)


def _prep_layer_weights(p, l):
    half = MLA_ROPE // 2
    row = lambda v: v.reshape(1, -1).astype(F32)
    lw = {}
    for tag in ("1", "2"):
        lw["ffn" + tag] = dict(g_ffn=row(p["norm_ffn" + tag][l]), w1=p["ffn%s_w1" % tag][l].astype(BF16),
                               w2=p["ffn%s_w2" % tag][l].astype(BF16))
    w_in = p["w_in"][l]
    o = 0
    cq = w_in[:, o:o + 256]; o += 256
    ckv = w_in[:, o:o + 128]; o += 128
    kpe = w_in[:, o:o + MLA_ROPE]; o += MLA_ROPE
    xb_gb = w_in[:, o:o + 512]; o += 512
    qkvz = w_in[:, o:o + 1024]; o += 1024
    a_in = w_in[:, o:o + HEADS]; o += HEADS
    b_in = w_in[:, o:o + HEADS]; o += HEADS
    rw = w_in[:, o:o + 1024]
    kpe_rot = jnp.concatenate([-kpe[:, half:], kpe[:, :half]], axis=1)
    w_in_p = jnp.concatenate([
        cq, ckv, _pad_cols(kpe, 128), _pad_cols(kpe_rot, 128),
        xb_gb,
        qkvz, jnp.repeat(a_in, HD, axis=1), jnp.repeat(b_in, HD, axis=1),
        rw], axis=1)
    lw["ffn1"]["g_mix"] = row(p["norm_mix"][l])
    lw["ffn1"]["w_in"] = w_in_p.astype(BF16)
    w_out = p["w_out"][l]
    wo_a = jnp.concatenate([jnp.pad(w_out[h * HD:(h + 1) * HD], ((0, MLA_PAD - HD), (0, 0)))
                            for h in range(HEADS)], axis=0)
    lw["w_out"] = [wo_a.astype(BF16)] + [w_out[MIXW * i:MIXW * (i + 1)].astype(BF16) for i in (1, 2, 3)]
    wuq = p["mla_w_uq"][l]
    wq = _mla_head_cols(wuq, HEADS, MLA_QK, [(MLA_NOPE, MLA_QK, 1.0), (0, MLA_NOPE, 1.0)])
    wq_rot = _mla_head_cols(wuq, HEADS, MLA_QK, [(MLA_NOPE + half, MLA_QK, -1.0),
                                                (MLA_NOPE, MLA_NOPE + half, 1.0)])
    lw["mla_wq"] = jnp.concatenate([wq, wq_rot], axis=1).astype(BF16)
    wukv = p["mla_w_ukv"][l]
    zero_rope = jnp.zeros((wukv.shape[0], MLA_ROPE), wukv.dtype)
    wk = jnp.concatenate([_pad_cols(jnp.concatenate([zero_rope, wukv[:, h * 128:h * 128 + MLA_NOPE]], axis=1),
                                    MLA_PAD) for h in range(HEADS)], axis=1)
    wv = jnp.concatenate([_pad_cols(wukv[:, h * 128 + MLA_NOPE:(h + 1) * 128], MLA_PAD)
                          for h in range(HEADS)], axis=1)
    lw["mla_wkv"] = jnp.concatenate([wk, wv], axis=1).astype(BF16)
    lw["mla_wv"] = wv.astype(BF16)
    lw["mla_wkt"] = jnp.concatenate([wukv[:, h * 128:h * 128 + MLA_NOPE] for h in range(HEADS)],
                                    axis=1).T.astype(BF16)
    lw["mla_gqa"] = row(p["mla_q_a_norm"][l])
    lw["mla_gkva"] = row(p["mla_kv_a_norm"][l])
    perm = lambda g: row(_pad_cols(jnp.concatenate([g[MLA_NOPE:], g[:MLA_NOPE]])[None], MLA_PAD))
    lw["mla_gq"] = perm(p["mla_q_norm"][l])
    lw["mla_gk"] = perm(p["mla_k_norm"][l])
    lw["lru_cw"] = p["lru_conv_w"][l]
    lw["lru_cb"] = row(p["lru_conv_b"][l])
    lw["lru_wa"] = _block_diag_heads(p["lru_wa"][l]).astype(BF16)
    lw["lru_ba"] = row(p["lru_ba"][l])
    lw["lru_wx"] = _block_diag_heads(p["lru_wx"][l]).astype(BF16)
    lw["lru_bx"] = row(p["lru_bx"][l])
    lw["lru_lam"] = row(p["lru_lambda"][l])
    lw["gdn_cw"] = p["gdn_conv_w"][l]
    lw["gdn_alog"] = row(jnp.repeat(p["gdn_a_log"][l], HD))
    lw["gdn_dtb"] = row(jnp.repeat(p["gdn_dt_bias"][l], HD))
    lw["gdn_gon"] = row(jnp.tile(p["gdn_o_norm"][l], HEADS))
    z64 = jnp.zeros((64, MIXW), F32)
    lw["rwkv_mu"] = row(p["rwkv_mu"][l])
    lw["rwkv_w0"] = row(p["rwkv_w0"][l])
    lw["rwkv_wb"] = jnp.concatenate([p["rwkv_w_b"][l], z64], axis=0).astype(BF16)
    lw["rwkv_a0"] = row(p["rwkv_a0"][l])
    lw["rwkv_ab"] = jnp.concatenate([z64, p["rwkv_a_b"][l]], axis=0).astype(BF16)
    lw["rwkv_gb"] = p["rwkv_g_b"][l].astype(BF16)
    lw["rwkv_kk"] = row(p["rwkv_k_k"][l])
    lw["rwkv_ka"] = row(p["rwkv_k_a"][l])
    lw["rwkv_rk"] = row(p["rwkv_r_k"][l])
    lw["rwkv_lnw"] = row(p["rwkv_ln_w"][l])
    lw["rwkv_lnb"] = row(p["rwkv_ln_b"][l])
    return lw


def _rope_tables(pos):
    inv = ROPE_THETA ** (-jnp.arange(0, MLA_ROPE, 2, dtype=F32) / MLA_ROPE)
    ang = pos.astype(F32)[:, None] * inv[None, :]
    cos, sin = jnp.cos(ang), jnp.sin(ang)
    n = pos.shape[0]
    cos_t = jnp.concatenate([cos, cos, jnp.ones((n, MLA_PAD - MLA_ROPE), F32)], axis=1)
    sin_t = jnp.concatenate([sin, sin, jnp.zeros((n, MLA_PAD - MLA_ROPE), F32)], axis=1)
    return cos_t, sin_t


def _hist8(rows):
    return jnp.pad(rows.astype(F32), ((0, 0), (8 - rows.shape[1], 0), (0, 0)))


def _run_group(x, lws, final_g, states, cfg):
    b, lp, d = x.shape
    n = b * lp
    t_valid, tm, tt, c = cfg["t_valid"], cfg["tm"], cfg["tt"], cfg["c"]
    cos_t, sin_t = cfg["rope"]
    xf = x.reshape(n, d)
    new_states = []
    depth = len(lws)
    for l in range(depth):
        lw = lws[l]
        st = states[l]
        xf, ca, cb, cc, cd = _token_call(xf, lw["ffn1"], post=True, tm=tm)
        q, k, v, ckv, kpe = _mla_prep_call(ca, cos_t, sin_t, lw, tm=tm)
        r3 = lambda a: a.reshape(b, lp, a.shape[-1])
        if st["mla"] is None:
            ya = _mla_prompt_call(r3(q), r3(k), r3(v), tq=cfg["tq"], tk=cfg["tk"])
        else:
            cache_ckv, cache_kpe_t = st["mla"]
            ya = _mla_sample_call(r3(q), r3(k), r3(v), cache_ckv, cache_kpe_t, l, lw)
        yb, lru_conv, lru_h = _lru_call(r3(cb), st["lru_conv"], st["lru_h"], lw, tt=tt, t_valid=t_valid)
        yc, gdn_conv, gdn_s = _gdn_call(r3(cc), st["gdn_conv"], st["gdn_s"], lw, nb=cfg["nb"], tt=tt, c=c,
                                        t_valid=t_valid)
        yd, shift, rwkv_s = _rwkv_call(r3(cd), st["rwkv_shift"], st["rwkv_s"], lw, nb=cfg["nb"], tt=tt, c=c,
                                       t_valid=t_valid)
        ys = [ya.reshape(n, -1), yb.reshape(n, -1), yc.reshape(n, -1), yd.reshape(n, -1)]
        xf = _token_call(xf, lw["ffn2"], pre=(ys, lw["w_out"]), tm=tm,
                         final_g=final_g if l == depth - 1 else None)[0]
        new_states.append((r3(ckv)[:, :t_valid], r3(kpe)[:, :t_valid], lru_conv, lru_h[:, 0], gdn_conv,
                           gdn_s, shift[:, 0], rwkv_s))
    stacked = [jnp.stack(t) for t in zip(*new_states)]
    return xf.reshape(b, lp, d), stacked


def _group_config(b, t_valid):
    divisor = lambda target: max(d for d in range(1, target + 1) if b % d == 0)
    if t_valid <= CHUNK:
        lp = t_valid
        return dict(lp=lp, t_valid=t_valid, tm=b * lp if b * lp <= 512 else lp, tt=lp, c=lp, tq=lp, tk=1024,
                    nb=divisor(8))
    lp = -(-t_valid // 384) * 384
    return dict(lp=lp, t_valid=t_valid, tm=384, tt=384, c=CHUNK, tq=384, tk=384, nb=divisor(2))


def kernel(x_prompt, x_sample, cache_mla_ckv, cache_mla_kpe, state_lru_conv, state_lru_h, state_gdn_conv, state_gdn_s, state_rwkv_shift, state_rwkv_s, meta_tokens, norm_ffn1, ffn1_w1, ffn1_w2, norm_mix, w_in, mla_q_a_norm, mla_w_uq, mla_kv_a_norm, mla_w_ukv, mla_q_norm, mla_k_norm, lru_conv_w, lru_conv_b, lru_wa, lru_ba, lru_wx, lru_bx, lru_lambda, gdn_conv_w, gdn_a_log, gdn_dt_bias, gdn_o_norm, rwkv_mu, rwkv_w0, rwkv_w_b, rwkv_a0, rwkv_a_b, rwkv_g_b, rwkv_k_k, rwkv_k_a, rwkv_r_k, rwkv_ln_w, rwkv_ln_b, w_out, norm_ffn2, ffn2_w1, ffn2_w2, final_norm):
    p = dict(norm_ffn1=norm_ffn1, ffn1_w1=ffn1_w1, ffn1_w2=ffn1_w2, norm_mix=norm_mix, w_in=w_in,
             mla_q_a_norm=mla_q_a_norm, mla_w_uq=mla_w_uq, mla_kv_a_norm=mla_kv_a_norm,
             mla_w_ukv=mla_w_ukv, mla_q_norm=mla_q_norm, mla_k_norm=mla_k_norm,
             lru_conv_w=lru_conv_w, lru_conv_b=lru_conv_b, lru_wa=lru_wa, lru_ba=lru_ba,
             lru_wx=lru_wx, lru_bx=lru_bx, lru_lambda=lru_lambda, gdn_conv_w=gdn_conv_w,
             gdn_a_log=gdn_a_log, gdn_dt_bias=gdn_dt_bias, gdn_o_norm=gdn_o_norm, rwkv_mu=rwkv_mu,
             rwkv_w0=rwkv_w0, rwkv_w_b=rwkv_w_b, rwkv_a0=rwkv_a0, rwkv_a_b=rwkv_a_b, rwkv_g_b=rwkv_g_b,
             rwkv_k_k=rwkv_k_k, rwkv_k_a=rwkv_k_a, rwkv_r_k=rwkv_r_k, rwkv_ln_w=rwkv_ln_w,
             rwkv_ln_b=rwkv_ln_b, w_out=w_out, norm_ffn2=norm_ffn2, ffn2_w1=ffn2_w1, ffn2_w2=ffn2_w2)
    depth = w_in.shape[0]
    lws = [_prep_layer_weights(p, l) for l in range(depth)]
    final_g = final_norm.reshape(1, -1).astype(F32)
    d = x_prompt.shape[-1]

    bp, seq, _ = x_prompt.shape
    tp = N_META + seq
    cfg = _group_config(bp, tp)
    lp = cfg["lp"]
    x0 = jnp.concatenate([jnp.broadcast_to(meta_tokens.astype(F32)[None], (bp, N_META, d)), x_prompt,
                          jnp.zeros((bp, lp - tp, d), F32)], axis=1)
    cfg["rope"] = tuple(jnp.tile(t, (bp, 1)) for t in _rope_tables(jnp.arange(lp)))
    zero = dict(mla=None, lru_conv=jnp.zeros((bp, 8, MIXW), F32), lru_h=jnp.zeros((bp, 1, MIXW), F32),
                gdn_conv=jnp.zeros((bp, 8, 3 * MIXW), F32), gdn_s=jnp.zeros((bp, HEADS, HD, HD), F32),
                rwkv_shift=jnp.zeros((bp, 8, D_W), F32), rwkv_s=jnp.zeros((bp, HEADS, HD, HD), F32))
    yp, p_new = _run_group(x0, lws, final_g, [zero] * depth, cfg)

    bs, ts, _ = x_sample.shape
    past = cache_mla_ckv.shape[2]
    cfg_s = _group_config(bs, ts)
    cfg_s["rope"] = tuple(jnp.tile(t, (bs, 1)) for t in _rope_tables(past + jnp.arange(ts)))
    cache_kpe_t = jnp.swapaxes(cache_mla_kpe, 2, 3)
    st_s = [dict(mla=(cache_mla_ckv, cache_kpe_t), lru_conv=_hist8(state_lru_conv[l]),
                 lru_h=state_lru_h[l][:, None].astype(F32), gdn_conv=_hist8(state_gdn_conv[l]),
                 gdn_s=state_gdn_s[l], rwkv_shift=_hist8(state_rwkv_shift[l][:, None]),
                 rwkv_s=state_rwkv_s[l]) for l in range(depth)]
    ys, s_new = _run_group(x_sample, lws, final_g, st_s, cfg_s)
    return (yp[:, N_META:tp], ys) + tuple(p_new) + tuple(s_new)
```

```python
import functools
import math

import jax
import jax.numpy as jnp
import numpy as np
from jax import lax
from jax.experimental import pallas as pl
from jax.experimental.pallas import tpu as pltpu

F32 = jnp.float32
BF16 = jnp.bfloat16
HIGHEST = lax.Precision.HIGHEST

EPS = 1e-6
N_META = 16
CHUNK = 64
CONV_W = 4
HEADS = 4
HD = 64
MIXW = HEADS * HD
MLA_NOPE = 64
MLA_ROPE = 32
MLA_QK = MLA_NOPE + MLA_ROPE
MLA_V = 64
MLA_PAD = 128
ROPE_THETA = 10000.0
LRU_C = 8.0
RWKV_GN_EPS = 64e-5
RWKV_DECAY_SCALE = -0.606531

VMEM_LIMIT = 56 * 1024 * 1024
MXU_N = 256

A_W = 640
B_W = 512
C_W = 1536
D_W = 1024


def _bdot(a, b):
    return jnp.dot(a.astype(BF16), b.astype(BF16), preferred_element_type=F32)


def _bdot_nt(a, b):
    return lax.dot_general(a.astype(BF16), b.astype(BF16), (((1,), (1,)), ((), ())),
                           preferred_element_type=F32)


def _bdot_tn(a, b):
    return lax.dot_general(a.astype(BF16), b.astype(BF16), (((0,), (0,)), ((), ())),
                           preferred_element_type=F32)


def _split3(x):
    x1 = x.astype(BF16)
    r1 = x - x1.astype(F32)
    x2 = r1.astype(BF16)
    r2 = r1 - x2.astype(F32)
    return x1, x2, r2.astype(BF16)


def _xdot_r01(x, m01):
    return jnp.dot(x.astype(BF16), m01, preferred_element_type=F32)


def _xdot_l01(m01, x):
    x1, x2, x3 = _split3(x)
    d = functools.partial(jnp.dot, preferred_element_type=F32)
    return d(m01, x1) + d(m01, x2) + d(m01, x3)


def _sigmoid(x):
    return 0.5 * jnp.tanh(0.5 * x) + 0.5


def _silu(x):
    return x * _sigmoid(x)


def _softplus(x):
    return jnp.maximum(x, 0.0) + jnp.log(1.0 + jnp.exp(-jnp.abs(x)))


def _gelu_tanh(x):
    return 0.5 * x * (1.0 + jnp.tanh(0.7978845608028654 * (x + 0.044715 * (x * x * x))))


def _rms_rows(x, g):
    return x * lax.rsqrt(jnp.mean(x * x, axis=-1, keepdims=True) + EPS) * g


def _block_diag_mask():
    r = lax.broadcasted_iota(jnp.int32, (MIXW, MIXW), 0) >> 6
    c = lax.broadcasted_iota(jnp.int32, (MIXW, MIXW), 1) >> 6
    return r == c


def _embed(x, bd01):
    c = x.shape[0]
    x = x.astype(BF16)
    if c < HD:
        x = jnp.concatenate([x, jnp.zeros((HD - c, MIXW), BF16)], axis=0)
    return jnp.concatenate([x] * HEADS, axis=0) * bd01


def _mm(a, b):
    return jnp.dot(a.astype(BF16), b, preferred_element_type=F32)


def _mm_nt(a, b):
    return lax.dot_general(a.astype(BF16), b, (((1,), (1,)), ((), ())), preferred_element_type=F32)


def _head_masks(c):
    row = lax.broadcasted_iota(jnp.int32, (c, MIXW), 0)
    col = lax.broadcasted_iota(jnp.int32, (c, MIXW), 1) & (HD - 1)
    return row >= col, row > col, row == col


def _tril01(c):
    r = lax.broadcasted_iota(jnp.int32, (c, c), 0)
    k = lax.broadcasted_iota(jnp.int32, (c, c), 1)
    return jnp.where(r >= k, 1.0, 0.0).astype(BF16)


def _inv_unit_lower(lms, eye4, bd01):
    c = lms[0].shape[0]
    levels = int(math.log2(c))
    eye = jnp.where(eye4, 1.0, 0.0)
    n = [-x for x in lms]
    t = [eye + x for x in n]
    if levels < 2:
        return t
    p = [_mm(x, _embed(x, bd01)) for x in n]
    for _ in range(levels - 2):
        r = [_mm(jnp.concatenate([pi, ti], axis=0), _embed(pi, bd01)) for pi, ti in zip(p, t)]
        p = [x[0:c] for x in r]
        t = [ti + x[c:2 * c] for ti, x in zip(t, r)]
    return [ti + _mm(ti, _embed(pi, bd01)) for pi, ti in zip(p, t)]


def _load_state(st_ref, s0_ref, s):
    st_ref[s] = jnp.zeros((MIXW, MIXW), F32)
    for h in range(HEADS):
        st_ref[s, h * HD:(h + 1) * HD, h * HD:(h + 1) * HD] = s0_ref[s, h].astype(F32)


def _store_state(so_ref, st_ref, s):
    for h in range(HEADS):
        so_ref[s, h] = st_ref[s, h * HD:(h + 1) * HD, h * HD:(h + 1) * HD]


def _shift_rows(x, s, fill):
    rolled = pltpu.roll(x, s, 0)
    row = lax.broadcasted_iota(jnp.int32, x.shape, 0)
    return jnp.where(row >= s, rolled, fill)


def _causal_conv(xext_ref, w_ref, tt):
    acc = None
    for j in range(CONV_W):
        term = xext_ref[pl.ds(8 - (CONV_W - 1) + j, tt), :] * w_ref[j:j + 1, :]
        acc = term if acc is None else acc + term
    return acc


def _const_spec(shape):
    nd = len(shape)
    return pl.BlockSpec(shape, lambda *_: (0,) * nd, pipeline_mode=pl.Buffered(1))


def _token_kernel(*refs, has_pre, has_post, has_final, d_ff, ff_chunk):
    it = iter(refs)
    x_ref = next(it)
    if has_pre:
        y_refs = [next(it) for _ in range(4)]
        wo_refs = [next(it) for _ in range(4)]
    gffn_ref, w1_ref, w2_ref = next(it), next(it), next(it)
    if has_post:
        gmix_ref, win_ref = next(it), next(it)
        prep_in = [next(it) for _ in range(8)]
    if has_final:
        gfin_ref = next(it)
    xo_ref = next(it)
    if has_post:
        prep_out = [next(it) for _ in range(5)]
        col_refs = [next(it) for _ in range(3)]

    x = x_ref[...]
    if has_pre:
        for y_ref, wo_ref in zip(y_refs, wo_refs):
            x = x + jnp.dot(y_ref[...], wo_ref[...], preferred_element_type=F32)
    h = _rms_rows(x, gffn_ref[...]).astype(BF16)
    acc = None
    for lo, hi in ((0, ff_chunk), (ff_chunk, d_ff)):
        gate = jnp.dot(h, w1_ref[:, lo:hi], preferred_element_type=F32)
        up = jnp.dot(h, w1_ref[:, d_ff + lo:d_ff + hi], preferred_element_type=F32)
        act = (_silu(gate) * up).astype(BF16)
        part = jnp.dot(act, w2_ref[lo:hi, :], preferred_element_type=F32)
        acc = part if acc is None else acc + part
    x = x + 0.5 * acc
    if has_final:
        xo_ref[...] = _rms_rows(x, gfin_ref[...])
    else:
        xo_ref[...] = x
    if has_post:
        hm = _rms_rows(x, gmix_ref[...]).astype(BF16)
        _mla_prep(jnp.dot(hm, win_ref[:, 0:A_W], preferred_element_type=F32), *prep_in, *prep_out)
        off = A_W
        for c_ref in col_refs:
            wd = c_ref.shape[-1]
            c_ref[...] = jnp.dot(hm, win_ref[:, off:off + wd], preferred_element_type=F32)
            off += wd


def _token_call(x, lw, *, pre=None, post=None, final_g=None, tm):
    n, d = x.shape
    d_ff = lw["w2"].shape[0]
    ff_chunk = -(-(d_ff // MXU_N) // 2) * MXU_N
    row = lambda w: pl.BlockSpec((tm, w), lambda i: (i, 0))
    args, specs = [x], [row(d)]
    if pre is not None:
        ys, wos = pre
        for y in ys:
            args.append(y)
            specs.append(row(y.shape[1]))
        for w in wos:
            args.append(w)
            specs.append(_const_spec(w.shape))
    for name in ("g_ffn", "w1", "w2"):
        args.append(lw[name])
        specs.append(_const_spec(lw[name].shape))
    if post is not None:
        cos_t, sin_t, mw, (b, lp, t_valid) = post
        for name in ("g_mix", "w_in"):
            args.append(lw[name])
            specs.append(_const_spec(lw[name].shape))
        args += [cos_t, sin_t]
        specs += [row(MLA_PAD), row(MLA_PAD)]
        for name in ("mla_gqa", "mla_wq", "mla_gkva", "mla_wkv", "mla_gq", "mla_gk"):
            args.append(mw[name])
            specs.append(_const_spec(mw[name].shape))
    if final_g is not None:
        args.append(final_g)
        specs.append(_const_spec(final_g.shape))
    out_shape = [jax.ShapeDtypeStruct((n, d), F32)]
    out_specs = [row(d)]
    if post is not None:
        hw = HEADS * MLA_PAD
        out_shape += [jax.ShapeDtypeStruct((n, hw), BF16)] * 3
        out_specs += [row(hw)] * 3
        for wd in (128, MLA_ROPE):
            if lp > t_valid:
                tps = lp // tm
                out_shape.append(jax.ShapeDtypeStruct((b, t_valid, wd), F32))
                out_specs.append(pl.BlockSpec((None, tm, wd), lambda i: (i // tps, i % tps, 0)))
            else:
                out_shape.append(jax.ShapeDtypeStruct((n, wd), F32))
                out_specs.append(row(wd))
        for wd in (B_W, C_W, D_W):
            out_shape.append(jax.ShapeDtypeStruct((n, wd), F32))
            out_specs.append(row(wd))
    body = functools.partial(_token_kernel, has_pre=pre is not None, has_post=post is not None,
                             has_final=final_g is not None, d_ff=d_ff, ff_chunk=ff_chunk)
    return pl.pallas_call(
        body, grid=(n // tm,), in_specs=specs, out_specs=out_specs, out_shape=out_shape,
        compiler_params=pltpu.CompilerParams(dimension_semantics=("parallel",),
                                             vmem_limit_bytes=VMEM_LIMIT),
        name="token_block")(*args)


def _mla_prep(ca, cos_ref, sin_ref, gqa_ref, wq_ref, gkva_ref, wkv_ref, gq_ref, gk_ref,
              q_ref, k_ref, v_ref, ckv_ref, kpe_ref):
    cs = cos_ref[...]
    sn = sin_ref[...]
    hw = HEADS * MLA_PAD
    cqn = _rms_rows(ca[:, 0:256], gqa_ref[...])
    qq = _bdot(cqn, wq_ref[...])
    scale = math.log2(math.e) / math.sqrt(MLA_QK)
    for h in range(HEADS):
        lo = h * MLA_PAD
        blk = qq[:, lo:lo + MLA_PAD] * cs + qq[:, hw + lo:hw + lo + MLA_PAD] * sn
        ms = jnp.sum(blk * blk, axis=-1, keepdims=True) * (1.0 / MLA_QK)
        q_ref[:, lo:lo + MLA_PAD] = (blk * lax.rsqrt(ms + EPS) * (gq_ref[...] * scale)).astype(BF16)
    ckv = _rms_rows(ca[:, 256:384], gkva_ref[...])
    ckv_ref[...] = ckv
    kpe = ca[:, 384:512] * cs + ca[:, 512:640] * sn
    kpe_ref[...] = kpe[:, 0:MLA_ROPE]
    kv = _bdot(ckv, wkv_ref[...])
    ones_lane = lax.broadcasted_iota(jnp.int32, (kv.shape[0], MLA_PAD), 1) == MLA_V
    for h in range(HEADS):
        lo = h * MLA_PAD
        kb = kv[:, lo:lo + MLA_PAD] + kpe
        ms = jnp.sum(kb * kb, axis=-1, keepdims=True) * (1.0 / MLA_QK)
        k_ref[:, lo:lo + MLA_PAD] = (kb * lax.rsqrt(ms + EPS) * gk_ref[...]).astype(BF16)
        v_ref[:, lo:lo + MLA_PAD] = jnp.where(ones_lane, 1.0, kv[:, hw + lo:hw + lo + MLA_PAD]).astype(BF16)


def _softmax_step(carry, s, vblk):
    m, acc = carry
    m_new = jnp.maximum(m, jnp.max(s, axis=-1, keepdims=True))
    p = jnp.exp2(s - m_new)
    acc = jnp.exp2(m - m_new) * acc + jnp.dot(p.astype(BF16), vblk, preferred_element_type=F32)
    return m_new, acc


def _softmax_init(t):
    return jnp.full((t, 1), -1e30, F32), jnp.zeros((t, MLA_PAD), F32)


def _softmax_finish(acc):
    return (acc / acc[:, MLA_V:MLA_V + 1]).astype(BF16)


def _mla_prompt_kernel(q_ref, k_ref, v_ref, o_ref, *, tq, tk, n_kb):
    q0 = pl.program_id(1) * tq
    first_chunk = (q0 - N_META) >> 6
    last_chunk = (q0 + tq - 1 - N_META) >> 6
    n_full = (N_META + CHUNK * (first_chunk + 1)) // tk
    nkb = jnp.minimum((N_META + CHUNK * (last_chunk + 1)) // tk, n_kb)
    qchunk = (q0 + lax.broadcasted_iota(jnp.int32, (tq, 1), 0) - N_META) >> 6
    hs = range(HEADS)
    qs = [q_ref[:, h * MLA_PAD:(h + 1) * MLA_PAD] for h in hs]

    def body(kb, state, masked):
        k0 = pl.multiple_of(kb * tk, tk)
        s = [lax.dot_general(qs[h], k_ref[pl.ds(k0, tk), h * MLA_PAD:(h + 1) * MLA_PAD],
                             (((1,), (1,)), ((), ())), preferred_element_type=F32) for h in hs]
        if masked:
            visible = ((k0 + lax.broadcasted_iota(jnp.int32, (1, tk), 1) - N_META) >> 6) <= qchunk
            s = [jnp.where(visible, x, -1e30) for x in s]
        m_new = [jnp.maximum(state[h][0], jnp.max(s[h], axis=-1, keepdims=True)) for h in hs]
        p = [jnp.exp2(s[h] - m_new[h]).astype(BF16) for h in hs]
        pv = [jnp.dot(p[h], v_ref[pl.ds(k0, tk), h * MLA_PAD:(h + 1) * MLA_PAD], preferred_element_type=F32)
              for h in hs]
        return tuple((m_new[h], jnp.exp2(state[h][0] - m_new[h]) * state[h][1] + pv[h]) for h in hs)

    state = tuple(_softmax_init(tq) for _ in hs)
    state = lax.fori_loop(0, n_full, functools.partial(body, masked=False), state)
    state = lax.fori_loop(n_full, nkb, functools.partial(body, masked=True), state)
    lp = n_kb * tk
    k_t = pl.multiple_of(jnp.minimum(nkb * tk, lp - N_META), N_META)
    tail_chunk = (k_t + lax.broadcasted_iota(jnp.int32, (1, N_META), 1) - N_META) >> 6
    visible = (tail_chunk <= qchunk) & (nkb < n_kb)
    for h in hs:
        blk = slice(h * MLA_PAD, (h + 1) * MLA_PAD)
        s = lax.dot_general(qs[h], k_ref[pl.ds(k_t, N_META), blk], (((1,), (1,)), ((), ())),
                            preferred_element_type=F32)
        _, acc = _softmax_step(state[h], jnp.where(visible, s, -1e30), v_ref[pl.ds(k_t, N_META), blk])
        o_ref[:, blk] = _softmax_finish(acc)


def _mla_prompt_call(q, k, v, *, tq, tk):
    b, lp, hw = q.shape
    assert tq == tk and tq % CHUNK == 0 and lp % tk == 0
    body = functools.partial(_mla_prompt_kernel, tq=tq, tk=tk, n_kb=lp // tk)
    return pl.pallas_call(
        body, grid=(b, lp // tq),
        in_specs=[pl.BlockSpec((None, tq, hw), lambda i, j: (i, j, 0)),
                  pl.BlockSpec((None, lp, hw), lambda i, j: (i, 0, 0)),
                  pl.BlockSpec((None, lp, hw), lambda i, j: (i, 0, 0))],
        out_specs=pl.BlockSpec((None, tq, hw), lambda i, j: (i, j, 0)),
        out_shape=jax.ShapeDtypeStruct((b, lp, hw), BF16),
        compiler_params=pltpu.CompilerParams(dimension_semantics=("parallel", "arbitrary"),
                                             vmem_limit_bytes=VMEM_LIMIT),
        name="mla_prompt_attn")(q, k, v)


def _mla_sample_kernel(q_ref, kn_ref, vn_ref, cckv_ref, ckpet_ref, wkt_ref, wv_ref, gk_ref, o_ref):
    t = q_ref.shape[0]
    hs = range(HEADS)
    blk = [slice(h * MLA_PAD, (h + 1) * MLA_PAD) for h in hs]
    ckv = cckv_ref[...].astype(BF16)
    knt = lax.dot_general(wkt_ref[...], ckv, (((1,), (1,)), ((), ())), preferred_element_type=F32)
    kpet = ckpet_ref[...]
    ssq_pe = jnp.sum(kpet * kpet, axis=0, keepdims=True)
    kpet16 = kpet.astype(BF16)
    zpad = jnp.zeros((MLA_PAD - MLA_QK, kpet.shape[1]), BF16)
    kts = [knt[h * MLA_NOPE:(h + 1) * MLA_NOPE, :] for h in hs]
    rk = [lax.rsqrt((jnp.sum(x * x, axis=0, keepdims=True) + ssq_pe) * (1.0 / MLA_QK) + EPS) for x in kts]
    qg = [(q_ref[:, blk[h]].astype(F32) * gk_ref[...]).astype(BF16) for h in hs]
    s_c = [jnp.dot(qg[h], jnp.concatenate([kpet16, kts[h].astype(BF16), zpad], axis=0),
                   preferred_element_type=F32) * rk[h] for h in hs]
    s_n = [lax.dot_general(q_ref[:, blk[h]], kn_ref[:, blk[h]], (((1,), (1,)), ((), ())),
                           preferred_element_type=F32) for h in hs]
    m = [jnp.maximum(jnp.max(s_c[h], axis=-1, keepdims=True), jnp.max(s_n[h], axis=-1, keepdims=True))
         for h in hs]
    p_c = [jnp.exp2(s_c[h] - m[h]) for h in hs]
    p_n = [jnp.exp2(s_n[h] - m[h]) for h in hs]
    den = [jnp.sum(p_c[h], axis=-1, keepdims=True) + jnp.sum(p_n[h], axis=-1, keepdims=True) for h in hs]
    lat = jnp.dot(jnp.concatenate([x.astype(BF16) for x in p_c], axis=0), ckv, preferred_element_type=F32)
    for h in hs:
        o = jnp.dot(lat[h * t:(h + 1) * t].astype(BF16), wv_ref[:, blk[h]], preferred_element_type=F32)
        o = o + jnp.dot(p_n[h].astype(BF16), vn_ref[:, blk[h]], preferred_element_type=F32)
        o_ref[:, blk[h]] = (o / den[h]).astype(BF16)


def _mla_sample_call(q, kn, vn, cache_ckv, cache_kpe_t, layer, lw):
    b, t, hw = q.shape
    past = cache_ckv.shape[2]
    seq = lambda: pl.BlockSpec((None, t, hw), lambda i: (i, 0, 0))
    consts = [lw["mla_wkt"], lw["mla_wv"], lw["mla_gk"]]
    return pl.pallas_call(
        _mla_sample_kernel, grid=(b,),
        in_specs=[seq(), seq(), seq(),
                  pl.BlockSpec((None, None, past, cache_ckv.shape[3]), lambda i: (layer, i, 0, 0)),
                  pl.BlockSpec((None, None, cache_kpe_t.shape[2], past), lambda i: (layer, i, 0, 0))]
        + [_const_spec(c.shape) for c in consts],
        out_specs=seq(),
        out_shape=jax.ShapeDtypeStruct((b, t, hw), BF16),
        compiler_params=pltpu.CompilerParams(dimension_semantics=("arbitrary",),
                                             vmem_limit_bytes=VMEM_LIMIT),
        name="mla_sample_attn")(q, kn, vn, cache_ckv, cache_kpe_t, *consts)


def _lru_kernel(cb_ref, buf_ref, h0_ref, cw_ref, cbias_ref, wa_ref, ba_ref, wx_ref, bx_ref, lam_ref,
                y_ref, nbuf_ref, hl_ref, xext, hcar, sa, sb, sh, *, tt, t_valid):
    ti = pl.program_id(1)

    @pl.when(ti == 0)
    def _():
        xext[0:8, :] = buf_ref[...]
        hcar[...] = h0_ref[...]

    @pl.when(ti > 0)
    def _():
        xext[0:8, :] = xext[tt:tt + 8, :]

    xext[8:tt + 8, :] = cb_ref[:, 0:MIXW]
    xc = _causal_conv(xext, cw_ref, tt) + cbias_ref[...]
    r = _sigmoid(_bdot(xc, wa_ref[...]) + ba_ref[...])
    i = _sigmoid(_bdot(xc, wx_ref[...]) + bx_ref[...])
    log_a = -LRU_C * r * _softplus(-lam_ref[...])
    pos = ti * tt + lax.broadcasted_iota(jnp.int32, (tt, 1), 0)
    log_a = jnp.where(pos < t_valid, log_a, 0.0)
    a = jnp.exp(log_a)
    b2 = -jnp.tanh(log_a) * (a * a + 1.0)
    b = jnp.where(b2 > 0.0, b2 * lax.rsqrt(b2), 0.0) * (i * xc)
    g = tt // 8
    a = a.reshape(g, 8, MIXW)
    b = b.reshape(g, 8, MIXW)
    sub = lax.broadcasted_iota(jnp.int32, (g, 8, 1), 1)
    for s in (1, 2, 4):
        keep = sub >= s
        b = a * jnp.where(keep, pltpu.roll(b, s, 1), 0.0) + b
        a = a * jnp.where(keep, pltpu.roll(a, s, 1), 1.0)
    a = a.reshape(tt, MIXW)
    b = b.reshape(tt, MIXW)
    halves = [slice(0, 128), slice(128, MIXW)]
    for k, lanes in enumerate(halves):
        sa[k] = a[:, lanes]
        sb[k] = b[:, lanes]
    ag =jnp.concatenate([sa[k, pl.ds(7, g, stride=8), :] for k in range(2)], axis=1)
    bg = jnp.concatenate([sb[k, pl.ds(7, g, stride=8), :] for k in range(2)], axis=1)
    if g % 8 == 0:
        s = 1
        while s < g:
            bg = ag * _shift_rows(bg, s, 0.0) + bg
            ag = ag * _shift_rows(ag, s, 1.0)
            s *= 2
        h_end = ag * hcar[...] + bg
        h_in = _shift_rows(h_end, 1, hcar[...])
        hcar[...] = h_end[g - 1:g, :]
    else:
        states = [hcar[...]]
        for j in range(g):
            states.append(ag[j:j + 1, :] * states[-1] + bg[j:j + 1, :])
        h_in = jnp.concatenate(states[:g], axis=0)
        hcar[...] = states[g]
    for r in range(8):
        rows = pl.ds(r, g, stride=8)
        for k, lanes in enumerate(halves):
            sh[k, rows, :] = sa[k, rows, :] * h_in[:, lanes] + sb[k, rows, :]
    h = jnp.concatenate([sh[0], sh[1]], axis=1)
    y_ref[...] = (h * _gelu_tanh(cb_ref[:, MIXW:2 * MIXW])).astype(BF16)

    last = t_valid - 1

    @pl.when(ti == last // tt)
    def _():
        r0 = last % tt
        hl_ref[...] = h[r0:r0 + 1, :]
        nbuf_ref[...] = xext[pl.ds(8 + r0 - (CONV_W - 2), CONV_W - 1), :]


def _lru_call(cb, buf8, h0, lw, *, tt, t_valid):
    b, lp, _ = cb.shape
    consts = [lw["lru_cw"], lw["lru_cb"], lw["lru_wa"], lw["lru_ba"], lw["lru_wx"], lw["lru_bx"],
              lw["lru_lam"]]
    body = functools.partial(_lru_kernel, tt=tt, t_valid=t_valid)
    per_b = lambda s: pl.BlockSpec((None,) + s, lambda i, j: (i, 0, 0))
    return pl.pallas_call(
        body, grid=(b, lp // tt),
        in_specs=[pl.BlockSpec((None, tt, B_W), lambda i, j: (i, j, 0)), per_b((8, MIXW)),
                  per_b((1, MIXW))] + [_const_spec(c.shape) for c in consts],
        out_specs=[pl.BlockSpec((None, tt, MIXW), lambda i, j: (i, j, 0)),
                   per_b((CONV_W - 1, MIXW)), per_b((1, MIXW))],
        out_shape=[jax.ShapeDtypeStruct((b, lp, MIXW), BF16),
                   jax.ShapeDtypeStruct((b, CONV_W - 1, MIXW), F32),
                   jax.ShapeDtypeStruct((b, 1, MIXW), F32)],
        scratch_shapes=[pltpu.VMEM((tt + 8, MIXW), F32), pltpu.VMEM((1, MIXW), F32)]
        + [pltpu.VMEM((2, tt, 128), F32)] * 3,
        compiler_params=pltpu.CompilerParams(dimension_semantics=("parallel", "arbitrary"),
                                             vmem_limit_bytes=VMEM_LIMIT),
        name="rglru")(cb, buf8, h0, *consts)


def _gdn_kernel(cc_ref, buf_ref, s0_ref, cw_ref, alog_ref, dtb_ref, gon_ref,
                y_ref, nbuf_ref, so_ref, xext, q_s, k_s, v_s, b_s, g_s, o_s, u_s, a_s, w_s, e_s, l_s,
                m_s, h_s, n_s, st_ref, *, nb, n_tiles, tt, c, t_valid):
    ti = pl.program_id(1)
    qkw = 3 * MIXW
    bd = _block_diag_mask()
    bd01 = jnp.where(bd, 1.0, 0.0).astype(BF16)
    pos = ti * tt + lax.broadcasted_iota(jnp.int32, (tt, 1), 0)
    valid = pos < t_valid

    for s in range(nb):
        xe = xext.at[s]
        sr = slice(s * tt, (s + 1) * tt)

        @pl.when(ti == 0)
        def _():
            xe[0:8, :] = buf_ref[s]
            _load_state(st_ref, s0_ref, s)

        @pl.when(ti > 0)
        def _():
            xe[0:8, :] = xe[tt:tt + 8, :]

        xe[8:tt + 8, :] = cc_ref[s, :, 0:qkw]
        xs = _silu(_causal_conv(xe, cw_ref, tt))
        q = xs[:, 0:MIXW]
        k = xs[:, MIXW:2 * MIXW]
        q_s[sr, :] = q * lax.rsqrt(_xdot_r01(q * q, bd01) + EPS) * (HD ** -0.5)
        k_s[sr, :] = k * lax.rsqrt(_xdot_r01(k * k, bd01) + EPS)
        v_s[sr, :] = xs[:, 2 * MIXW:3 * MIXW]
        a_in = cc_ref[s, :, qkw + MIXW:qkw + 2 * MIXW]
        b_in = cc_ref[s, :, qkw + 2 * MIXW:qkw + 3 * MIXW]
        b_s[sr, :] = jnp.where(valid, _sigmoid(b_in), 0.0)
        g_s[sr, :] = jnp.where(valid, -jnp.exp(alog_ref[...]) * _softplus(a_in + dtb_ref[...]), 0.0)

    tril4, strict4, eye4 = _head_masks(c)
    tril01 = _tril01(c)
    n_chunks = tt // c

    ch = range(nb * n_chunks)
    rs = [slice(ci * c, (ci + 1) * c) for ci in ch]
    gc = [_xdot_l01(tril01, g_s[r, :]) for r in rs]
    gct = [jnp.sum(jnp.where(eye4, x, 0.0), axis=0, keepdims=True) for x in gc]
    decay = [jnp.where(tril4, jnp.exp(jnp.where(tril4, gc[i] - gct[i], 0.0)), 0.0) for i in ch]
    kc = [k_s[r, :] for r in rs]
    beta = [b_s[r, :] for r in rs]
    kb = [kc[i] * beta[i] for i in ch]
    sc = [_mm_nt(jnp.concatenate([kb[i], q_s[rs[i], :]], axis=0), _embed(kc[i], bd01)) for i in ch]
    tinv = _inv_unit_lower([jnp.where(strict4, sc[i][0:c] * decay[i], 0.0) for i in ch], eye4, bd01)
    attn = [jnp.where(tril4, sc[i][c:2 * c] * decay[i], 0.0).astype(BF16) for i in ch]
    egc = [jnp.exp(x) for x in gc]
    uw = [_mm(tinv[i], jnp.concatenate([_embed(v_s[rs[i], :] * beta[i], bd01),
                                        _embed(kb[i] * egc[i], bd01)], axis=1)) for i in ch]
    glast = [x[c - 1:c, :] for x in gc]
    kd = [(kc[i] * jnp.exp(glast[i] - gc[i])).astype(BF16) for i in ch]
    an = [lax.dot_general(kd[i], jnp.concatenate([uw[i][:, MIXW:], uw[i][:, :MIXW]], axis=1).astype(BF16),
                          (((0,), (0,)), ((), ())), preferred_element_type=F32) for i in ch]
    for i in ch:
        u_s[rs[i], :] = uw[i][:, 0:MIXW]
        w_s[rs[i], :] = uw[i][:, MIXW:2 * MIXW].astype(BF16)
        e_s[rs[i], :] = (q_s[rs[i], :] * egc[i]).astype(BF16)
        a_s[rs[i], :] = attn[i]
        m_s[i] = jnp.where(bd, an[i][:, 0:MIXW], 0.0).astype(BF16)
        n_s[i] = jnp.where(bd, an[i][:, MIXW:2 * MIXW], 0.0)
        l_s[i:i + 1, :] = jnp.exp(glast[i])

    for ci in range(n_chunks):
        for s in range(nb):
            i = s * n_chunks + ci
            s_bd = st_ref[s]
            s16 = s_bd.astype(BF16)
            h_s[i] = s16
            st_ref[s] = s_bd * l_s[i:i + 1, :] + n_s[i] - jnp.dot(m_s[i], s16, preferred_element_type=F32)

    r = [jnp.dot(jnp.concatenate([w_s[rs[i], :], e_s[rs[i], :]], axis=0), h_s[i], preferred_element_type=F32)
         for i in ch]
    v_new = [u_s[rs[i], :] - r[i][0:c] for i in ch]
    ov = [jnp.dot(a_s[rs[i], :], _embed(v_new[i], bd01), preferred_element_type=F32) for i in ch]
    for i in ch:
        o_s[rs[i], :] = r[i][c:2 * c] + ov[i]

    last = t_valid - 1
    for s in range(nb):
        o = o_s[s * tt:(s + 1) * tt, :]
        ms = _xdot_r01(o * o, bd01) * (1.0 / HD)
        z = cc_ref[s, :, qkw:qkw + MIXW]
        y_ref[s] = (o * lax.rsqrt(ms + EPS) * gon_ref[...] * _silu(z)).astype(BF16)

        @pl.when(ti == last // tt)
        def _():
            nbuf_ref[s] = xext[s, pl.ds(8 + last % tt - (CONV_W - 2), CONV_W - 1), :]

        @pl.when(ti == n_tiles - 1)
        def _():
            _store_state(so_ref, st_ref, s)


def _gdn_call(cc, buf8, s0, lw, *, nb, tt, c, t_valid):
    b, lp, _ = cc.shape
    qkw = 3 * MIXW
    consts = [lw["gdn_cw"], lw["gdn_alog"], lw["gdn_dtb"], lw["gdn_gon"]]
    body = functools.partial(_gdn_kernel, nb=nb, n_tiles=lp // tt, tt=tt, c=c, t_valid=t_valid)
    per_b = lambda s: pl.BlockSpec((nb,) + s, lambda i, j: (i,) + (0,) * len(s))
    n_ch = nb * (tt // c)
    return pl.pallas_call(
        body, grid=(b // nb, lp // tt),
        in_specs=[pl.BlockSpec((nb, tt, C_W), lambda i, j: (i, j, 0)), per_b((8, qkw)),
                  per_b((HEADS, HD, HD))] + [_const_spec(x.shape) for x in consts],
        out_specs=[pl.BlockSpec((nb, tt, MIXW), lambda i, j: (i, j, 0)),
                   per_b((CONV_W - 1, qkw)), per_b((HEADS, HD, HD))],
        out_shape=[jax.ShapeDtypeStruct((b, lp, MIXW), BF16),
                   jax.ShapeDtypeStruct((b, CONV_W - 1, qkw), F32),
                   jax.ShapeDtypeStruct((b, HEADS, HD, HD), F32)],
        scratch_shapes=[pltpu.VMEM((nb, tt + 8, qkw), F32)] + [pltpu.VMEM((nb * tt, MIXW), F32)] * 7
        + [pltpu.VMEM((nb * tt, MIXW), BF16)] * 3 + [pltpu.VMEM((-(-n_ch // 8) * 8, MIXW), F32),
                                                     pltpu.VMEM((n_ch, MIXW, MIXW), BF16),
                                                     pltpu.VMEM((n_ch, MIXW, MIXW), BF16),
                                                     pltpu.VMEM((n_ch, MIXW, MIXW), F32),
                                                     pltpu.VMEM((nb, MIXW, MIXW), F32)],
        compiler_params=pltpu.CompilerParams(dimension_semantics=("parallel", "arbitrary"),
                                             vmem_limit_bytes=VMEM_LIMIT),
        name="gated_delta")(cc, buf8, s0, *consts)


def _rwkv_kernel(cd_ref, prev_ref, s0_ref, mu_ref, w0_ref, wb_ref, a0_ref, ab_ref, gb_ref, kk_ref,
                 ka_ref, rk_ref, lnw_ref, lnb_ref,
                 y_ref, shift_ref, so_ref, xext, r_s, k_s, v_s, n_s, b_s, w_s, o_s, xv_s,
                 t_s, a_s, lhs_s, p_s, g_s, h_s, d_s, rk_s, gt_s, st_ref, *, nb, n_tiles, tt, c, t_valid):
    ti = pl.program_id(1)
    pos = ti * tt + lax.broadcasted_iota(jnp.int32, (tt, 1), 0)
    valid = pos < t_valid
    bd = _block_diag_mask()
    bd01 = jnp.where(bd, 1.0, 0.0).astype(BF16)

    for s in range(nb):
        xe = xext.at[s]
        sr = slice(s * tt, (s + 1) * tt)

        @pl.when(ti == 0)
        def _():
            xe[0:8, :] = prev_ref[s]
            _load_state(st_ref, s0_ref, s)

        @pl.when(ti > 0)
        def _():
            xe[0:8, :] = xe[tt:tt + 8, :]

        x = cd_ref[s]
        xe[8:tt + 8, :] = x
        xm = x + (xe[pl.ds(7, tt), :] - x) * mu_ref[...]
        r = xm[:, 0:MIXW]
        k = xm[:, MIXW:2 * MIXW]
        lo_blk = xm[:, 3 * MIXW:3 * MIXW + 128]
        logw = RWKV_DECAY_SCALE * _sigmoid(w0_ref[...] + _bdot(jnp.tanh(lo_blk), wb_ref[...]))
        a = _sigmoid(a0_ref[...] + _bdot(lo_blk, ab_ref[...]))
        kkv = k * kk_ref[...]
        kkn = kkv * lax.rsqrt(_xdot_r01(kkv * kkv, bd01) + EPS)
        kkn = jnp.where(valid, kkn, 0.0)
        kmod = k * (1.0 + (a - 1.0) * ka_ref[...])
        r_s[sr, :] = r
        k_s[sr, :] = jnp.where(valid, kmod, 0.0)
        v_s[sr, :] = xm[:, 2 * MIXW:3 * MIXW]
        n_s[sr, :] = kkn
        b_s[sr, :] = kkn * a
        w_s[sr, :] = jnp.where(valid, logw, 0.0)
        rk_s[sr, :] = _xdot_r01(r * kmod * rk_ref[...], bd01)
        gt_s[sr, :] = _bdot(_sigmoid(xm[:, 3 * MIXW + 128:3 * MIXW + 256]), gb_ref[...])

    tril4, strict4, eye4 = _head_masks(c)
    tril01 = _tril01(c)

    n_chunks = tt // c

    ch = range(nb * n_chunks)
    rs = [slice(ci * c, (ci + 1) * c) for ci in ch]
    lw = [w_s[r, :] for r in rs]
    cum = [_xdot_l01(tril01, x) for x in lw]
    ecum = [jnp.exp(x) for x in cum]
    pin = [jnp.exp(-x) for x in cum]
    pc = [x[c - 1:c, :] for x in ecum]
    kinv = [k_s[rs[i], :] * pin[i] for i in ch]
    binv = [b_s[rs[i], :] * pin[i] for i in ch]
    nd = [n_s[rs[i], :] * jnp.exp(cum[i] - lw[i]) for i in ch]
    lhs = [jnp.concatenate([nd[i], r_s[rs[i], :] * ecum[i]], axis=0).astype(BF16) for i in ch]
    sk = [_mm_nt(lhs[i], _embed(kinv[i], bd01)) for i in ch]
    sb = [_mm_nt(lhs[i], _embed(binv[i], bd01)) for i in ch]
    tinv = _inv_unit_lower([jnp.where(strict4, sb[i][0:c], 0.0) for i in ch], eye4, bd01)
    av = [_mm(jnp.concatenate([jnp.where(strict4, sk[i][0:c], 0.0), jnp.where(tril4, sk[i][c:2 * c], 0.0)],
                              axis=0), _embed(v_s[rs[i], :], bd01)) for i in ch]
    tn = [_mm(tinv[i], jnp.concatenate([_embed(nd[i], bd01), _embed(av[i][0:c], bd01)], axis=1)) for i in ch]
    bp = [(binv[i] * pc[i]).astype(BF16) for i in ch]
    gh = [lax.dot_general(tn[i].astype(BF16), bp[i], (((0,), (0,)), ((), ())), preferred_element_type=F32)
          for i in ch]
    vk = [_bdot_tn(v_s[rs[i], :], kinv[i] * pc[i]) for i in ch]
    for i in ch:
        lhs_s[2 * i * c:2 * (i + 1) * c, :] = lhs[i]
        t_s[rs[i], :] = tinv[i].astype(BF16)
        a_s[rs[i], :] = jnp.where(tril4, sb[i][c:2 * c], 0.0).astype(BF16)
        xv_s[rs[i], :] = av[i][0:c]
        o_s[rs[i], :] = av[i][c:2 * c]
        p_s[i:i + 1, :] = pc[i]
        g_s[i] = jnp.where(bd, gh[i][0:MIXW], 0.0).astype(BF16)
        d_s[i] = jnp.where(bd, vk[i] - gh[i][MIXW:2 * MIXW], 0.0)

    for ci in range(n_chunks):
        for s in range(nb):
            i = s * n_chunks + ci
            st = st_ref[s]
            s16 = st.astype(BF16)
            h_s[i] = s16
            st_ref[s] = st * p_s[i:i + 1, :] + d_s[i] - jnp.dot(s16, g_s[i], preferred_element_type=F32)

    r2 = [lax.dot_general(lhs_s[2 * i * c:2 * (i + 1) * c, :], h_s[i], (((1,), (1,)), ((), ())),
                          preferred_element_type=F32) for i in ch]
    u = [jnp.dot(t_s[rs[i], :], _embed(r2[i][0:c] + xv_s[rs[i], :], bd01), preferred_element_type=F32)
         for i in ch]
    au = [jnp.dot(a_s[rs[i], :], _embed(u[i], bd01), preferred_element_type=F32) for i in ch]
    for i in ch:
        o_s[rs[i], :] = r2[i][c:2 * c] + o_s[rs[i], :] - au[i]

    last = t_valid - 1
    for s in range(nb):
        sr = slice(s * tt, (s + 1) * tt)
        o = o_s[sr, :]
        mean = _xdot_r01(o, bd01) * (1.0 / HD)
        d = o - mean
        var = _xdot_r01(d * d, bd01) * (1.0 / HD)
        o = d * lax.rsqrt(var + RWKV_GN_EPS) * lnw_ref[...] + lnb_ref[...]
        o = o + rk_s[sr, :] * v_s[sr, :]
        y_ref[s] = (o * gt_s[sr, :]).astype(BF16)

        @pl.when(ti == last // tt)
        def _():
            shift_ref[s] = xext[s, pl.ds(8 + last % tt, 1), :]

        @pl.when(ti == n_tiles - 1)
        def _():
            _store_state(so_ref, st_ref, s)


def _rwkv_call(cd, prev8, s0, lw, *, nb, tt, c, t_valid):
    b, lp, _ = cd.shape
    consts = [lw["rwkv_mu"], lw["rwkv_w0"], lw["rwkv_wb"], lw["rwkv_a0"], lw["rwkv_ab"], lw["rwkv_gb"],
              lw["rwkv_kk"], lw["rwkv_ka"], lw["rwkv_rk"], lw["rwkv_lnw"], lw["rwkv_lnb"]]
    body = functools.partial(_rwkv_kernel, nb=nb, n_tiles=lp // tt, tt=tt, c=c, t_valid=t_valid)
    per_b = lambda s: pl.BlockSpec((nb,) + s, lambda i, j: (i,) + (0,) * len(s))
    n_ch = nb * (tt // c)
    return pl.pallas_call(
        body, grid=(b // nb, lp // tt),
        in_specs=[pl.BlockSpec((nb, tt, D_W), lambda i, j: (i, j, 0)), per_b((8, D_W)),
                  per_b((HEADS, HD, HD))] + [_const_spec(x.shape) for x in consts],
        out_specs=[pl.BlockSpec((nb, tt, MIXW), lambda i, j: (i, j, 0)),
                   per_b((1, D_W)), per_b((HEADS, HD, HD))],
        out_shape=[jax.ShapeDtypeStruct((b, lp, MIXW), BF16),
                   jax.ShapeDtypeStruct((b, 1, D_W), F32),
                   jax.ShapeDtypeStruct((b, HEADS, HD, HD), F32)],
        scratch_shapes=[pltpu.VMEM((nb, tt + 8, D_W), F32)] + [pltpu.VMEM((nb * tt, MIXW), F32)] * 8
        + [pltpu.VMEM((nb * tt, MIXW), BF16)] * 2 + [pltpu.VMEM((2 * nb * tt, MIXW), BF16),
                                                     pltpu.VMEM((-(-n_ch // 8) * 8, MIXW), F32),
                                                     pltpu.VMEM((n_ch, MIXW, MIXW), BF16),
                                                     pltpu.VMEM((n_ch, MIXW, MIXW), BF16),
                                                     pltpu.VMEM((n_ch, MIXW, MIXW), F32)]
        + [pltpu.VMEM((nb * tt, MIXW), F32)] * 2 + [pltpu.VMEM((nb, MIXW, MIXW), F32)],
        compiler_params=pltpu.CompilerParams(dimension_semantics=("parallel", "arbitrary"),
                                             vmem_limit_bytes=VMEM_LIMIT),
        name="rwkv7")(cd, prev8, s0, *consts)


def _pad_cols(w, width):
    return jnp.pad(w, ((0, 0), (0, width - w.shape[1])))


def _mla_head_cols(w, n_heads, per_head, pieces):
    blocks = []
    for h in range(n_heads):
        cols = [sign * w[:, h * per_head + lo:h * per_head + hi] for lo, hi, sign in pieces]
        blocks.append(_pad_cols(jnp.concatenate(cols, axis=1), MLA_PAD))
    return jnp.concatenate(blocks, axis=1)


def _block_diag_heads(w):
    eye = jnp.eye(HEADS, dtype=w.dtype)[:, None, :, None]
    return (w[:, :, None, :] * eye).reshape(MIXW, MIXW)


def _prep_layer_weights(p, l):
    half = MLA_ROPE // 2
    row = lambda v: v.reshape(1, -1).astype(F32)
    lw = {}
    for tag in ("1", "2"):
        lw["ffn" + tag] = dict(g_ffn=row(p["norm_ffn" + tag][l]), w1=p["ffn%s_w1" % tag][l].astype(BF16),
                               w2=p["ffn%s_w2" % tag][l].astype(BF16))
    w_in = p["w_in"][l]
    o = 0
    cq = w_in[:, o:o + 256]; o += 256
    ckv = w_in[:, o:o + 128]; o += 128
    kpe = w_in[:, o:o + MLA_ROPE]; o += MLA_ROPE
    xb_gb = w_in[:, o:o + 512]; o += 512
    qkvz = w_in[:, o:o + 1024]; o += 1024
    a_in = w_in[:, o:o + HEADS]; o += HEADS
    b_in = w_in[:, o:o + HEADS]; o += HEADS
    rw = w_in[:, o:o + 1024]
    kpe_rot = jnp.concatenate([-kpe[:, half:], kpe[:, :half]], axis=1)
    w_in_p = jnp.concatenate([
        cq, ckv, _pad_cols(kpe, 128), _pad_cols(kpe_rot, 128),
        xb_gb,
        qkvz, jnp.repeat(a_in, HD, axis=1), jnp.repeat(b_in, HD, axis=1),
        rw], axis=1)
    lw["ffn1"]["g_mix"] = row(p["norm_mix"][l])
    lw["ffn1"]["w_in"] = w_in_p.astype(BF16)
    w_out = p["w_out"][l]
    wo_a = jnp.concatenate([jnp.pad(w_out[h * HD:(h + 1) * HD], ((0, MLA_PAD - HD), (0, 0)))
                            for h in range(HEADS)], axis=0)
    lw["w_out"] = [wo_a.astype(BF16)] + [w_out[MIXW * i:MIXW * (i + 1)].astype(BF16) for i in (1, 2, 3)]
    wuq = p["mla_w_uq"][l]
    wq = _mla_head_cols(wuq, HEADS, MLA_QK, [(MLA_NOPE, MLA_QK, 1.0), (0, MLA_NOPE, 1.0)])
    wq_rot = _mla_head_cols(wuq, HEADS, MLA_QK, [(MLA_NOPE + half, MLA_QK, -1.0),
                                                (MLA_NOPE, MLA_NOPE + half, 1.0)])
    lw["mla_wq"] = jnp.concatenate([wq, wq_rot], axis=1).astype(BF16)
    wukv = p["mla_w_ukv"][l]
    zero_rope = jnp.zeros((wukv.shape[0], MLA_ROPE), wukv.dtype)
    wk = jnp.concatenate([_pad_cols(jnp.concatenate([zero_rope, wukv[:, h * 128:h * 128 + MLA_NOPE]], axis=1),
                                    MLA_PAD) for h in range(HEADS)], axis=1)
    wv = jnp.concatenate([_pad_cols(wukv[:, h * 128 + MLA_NOPE:(h + 1) * 128], MLA_PAD)
                          for h in range(HEADS)], axis=1)
    lw["mla_wkv"] = jnp.concatenate([wk, wv], axis=1).astype(BF16)
    lw["mla_wv"] = wv.astype(BF16)
    lw["mla_wkt"] = jnp.concatenate([wukv[:, h * 128:h * 128 + MLA_NOPE] for h in range(HEADS)],
                                    axis=1).T.astype(BF16)
    lw["mla_gqa"] = row(p["mla_q_a_norm"][l])
    lw["mla_gkva"] = row(p["mla_kv_a_norm"][l])
    perm = lambda g: row(_pad_cols(jnp.concatenate([g[MLA_NOPE:], g[:MLA_NOPE]])[None], MLA_PAD))
    lw["mla_gq"] = perm(p["mla_q_norm"][l])
    lw["mla_gk"] = perm(p["mla_k_norm"][l])
    lw["lru_cw"] = p["lru_conv_w"][l]
    lw["lru_cb"] = row(p["lru_conv_b"][l])
    lw["lru_wa"] = _block_diag_heads(p["lru_wa"][l]).astype(BF16)
    lw["lru_ba"] = row(p["lru_ba"][l])
    lw["lru_wx"] = _block_diag_heads(p["lru_wx"][l]).astype(BF16)
    lw["lru_bx"] = row(p["lru_bx"][l])
    lw["lru_lam"] = row(p["lru_lambda"][l])
    lw["gdn_cw"] = p["gdn_conv_w"][l]
    lw["gdn_alog"] = row(jnp.repeat(p["gdn_a_log"][l], HD))
    lw["gdn_dtb"] = row(jnp.repeat(p["gdn_dt_bias"][l], HD))
    lw["gdn_gon"] = row(jnp.tile(p["gdn_o_norm"][l], HEADS))
    z64 = jnp.zeros((64, MIXW), F32)
    lw["rwkv_mu"] = row(p["rwkv_mu"][l])
    lw["rwkv_w0"] = row(p["rwkv_w0"][l])
    lw["rwkv_wb"] = jnp.concatenate([p["rwkv_w_b"][l], z64], axis=0).astype(BF16)
    lw["rwkv_a0"] = row(p["rwkv_a0"][l])
    lw["rwkv_ab"] = jnp.concatenate([z64, p["rwkv_a_b"][l]], axis=0).astype(BF16)
    lw["rwkv_gb"] = p["rwkv_g_b"][l].astype(BF16)
    lw["rwkv_kk"] = row(p["rwkv_k_k"][l])
    lw["rwkv_ka"] = row(p["rwkv_k_a"][l])
    lw["rwkv_rk"] = row(p["rwkv_r_k"][l])
    lw["rwkv_lnw"] = row(p["rwkv_ln_w"][l])
    lw["rwkv_lnb"] = row(p["rwkv_ln_b"][l])
    return lw


def _rope_tables(pos):
    inv = ROPE_THETA ** (-jnp.arange(0, MLA_ROPE, 2, dtype=F32) / MLA_ROPE)
    ang = pos.astype(F32)[:, None] * inv[None, :]
    cos, sin = jnp.cos(ang), jnp.sin(ang)
    n = pos.shape[0]
    cos_t = jnp.concatenate([cos, cos, jnp.ones((n, MLA_PAD - MLA_ROPE), F32)], axis=1)
    sin_t = jnp.concatenate([sin, sin, jnp.zeros((n, MLA_PAD - MLA_ROPE), F32)], axis=1)
    return cos_t, sin_t


def _hist8(rows):
    return jnp.pad(rows.astype(F32), ((0, 0), (8 - rows.shape[1], 0), (0, 0)))


def _run_group(x, lws, final_g, states, cfg):
    b, lp, d = x.shape
    n = b * lp
    t_valid, tm, tt, c = cfg["t_valid"], cfg["tm"], cfg["tt"], cfg["c"]
    cos_t, sin_t = cfg["rope"]
    xf = x.reshape(n, d)
    new_states = []
    depth = len(lws)
    for l in range(depth):
        lw = lws[l]
        st = states[l]
        xf, q, k, v, ckv, kpe, cb, cc, cd = _token_call(xf, lw["ffn1"], tm=tm,
                                                        post=(cos_t, sin_t, lw, (b, lp, t_valid)))
        r3 = lambda a: a.reshape(b, lp, a.shape[-1])
        if st["mla"] is None:
            ya = _mla_prompt_call(r3(q), r3(k), r3(v), tq=cfg["tq"], tk=cfg["tk"])
        else:
            cache_ckv, cache_kpe_t = st["mla"]
            ya = _mla_sample_call(r3(q), r3(k), r3(v), cache_ckv, cache_kpe_t, l, lw)
        yb, lru_conv, lru_h = _lru_call(r3(cb), st["lru_conv"], st["lru_h"], lw, tt=tt, t_valid=t_valid)
        yc, gdn_conv, gdn_s = _gdn_call(r3(cc), st["gdn_conv"], st["gdn_s"], lw, nb=cfg["nb"], tt=tt, c=c,
                                        t_valid=t_valid)
        yd, shift, rwkv_s = _rwkv_call(r3(cd), st["rwkv_shift"], st["rwkv_s"], lw, nb=cfg["nb"], tt=tt, c=c,
                                       t_valid=t_valid)
        ys = [ya.reshape(n, -1), yb.reshape(n, -1), yc.reshape(n, -1), yd.reshape(n, -1)]
        xf = _token_call(xf, lw["ffn2"], pre=(ys, lw["w_out"]), tm=tm,
                         final_g=final_g if l == depth - 1 else None)[0]
        new_states.append((ckv.reshape(b, t_valid, -1), kpe.reshape(b, t_valid, -1), lru_conv, lru_h[:, 0],
                           gdn_conv, gdn_s, shift[:, 0], rwkv_s))
    stacked = [jnp.stack(t) for t in zip(*new_states)]
    return xf.reshape(b, lp, d), stacked


def _group_config(b, t_valid):
    divisor = lambda target: max(d for d in range(1, target + 1) if b % d == 0)
    if t_valid <= CHUNK:
        lp = t_valid
        return dict(lp=lp, t_valid=t_valid, tm=b * lp if b * lp <= 512 else lp, tt=lp, c=lp, tq=lp, tk=1024,
                    nb=divisor(8))
    lp = -(-t_valid // 384) * 384
    return dict(lp=lp, t_valid=t_valid, tm=384, tt=384, c=CHUNK, tq=384, tk=384, nb=divisor(2))


def kernel(x_prompt, x_sample, cache_mla_ckv, cache_mla_kpe, state_lru_conv, state_lru_h, state_gdn_conv, state_gdn_s, state_rwkv_shift, state_rwkv_s, meta_tokens, norm_ffn1, ffn1_w1, ffn1_w2, norm_mix, w_in, mla_q_a_norm, mla_w_uq, mla_kv_a_norm, mla_w_ukv, mla_q_norm, mla_k_norm, lru_conv_w, lru_conv_b, lru_wa, lru_ba, lru_wx, lru_bx, lru_lambda, gdn_conv_w, gdn_a_log, gdn_dt_bias, gdn_o_norm, rwkv_mu, rwkv_w0, rwkv_w_b, rwkv_a0, rwkv_a_b, rwkv_g_b, rwkv_k_k, rwkv_k_a, rwkv_r_k, rwkv_ln_w, rwkv_ln_b, w_out, norm_ffn2, ffn2_w1, ffn2_w2, final_norm):
    p = dict(norm_ffn1=norm_ffn1, ffn1_w1=ffn1_w1, ffn1_w2=ffn1_w2, norm_mix=norm_mix, w_in=w_in,
             mla_q_a_norm=mla_q_a_norm, mla_w_uq=mla_w_uq, mla_kv_a_norm=mla_kv_a_norm,
             mla_w_ukv=mla_w_ukv, mla_q_norm=mla_q_norm, mla_k_norm=mla_k_norm,
             lru_conv_w=lru_conv_w, lru_conv_b=lru_conv_b, lru_wa=lru_wa, lru_ba=lru_ba,
             lru_wx=lru_wx, lru_bx=lru_bx, lru_lambda=lru_lambda, gdn_conv_w=gdn_conv_w,
             gdn_a_log=gdn_a_log, gdn_dt_bias=gdn_dt_bias, gdn_o_norm=gdn_o_norm, rwkv_mu=rwkv_mu,
             rwkv_w0=rwkv_w0, rwkv_w_b=rwkv_w_b, rwkv_a0=rwkv_a0, rwkv_a_b=rwkv_a_b, rwkv_g_b=rwkv_g_b,
             rwkv_k_k=rwkv_k_k, rwkv_k_a=rwkv_k_a, rwkv_r_k=rwkv_r_k, rwkv_ln_w=rwkv_ln_w,
             rwkv_ln_b=rwkv_ln_b, w_out=w_out, norm_ffn2=norm_ffn2, ffn2_w1=ffn2_w1, ffn2_w2=ffn2_w2)
    depth = w_in.shape[0]
    lws = [_prep_layer_weights(p, l) for l in range(depth)]
    final_g = final_norm.reshape(1, -1).astype(F32)
    d = x_prompt.shape[-1]

    bp, seq, _ = x_prompt.shape
    tp = N_META + seq
    cfg = _group_config(bp, tp)
    lp = cfg["lp"]
    x0 = jnp.concatenate([jnp.broadcast_to(meta_tokens.astype(F32)[None], (bp, N_META, d)), x_prompt,
                          jnp.zeros((bp, lp - tp, d), F32)], axis=1)
    cfg["rope"] = tuple(jnp.tile(t, (bp, 1)) for t in _rope_tables(jnp.arange(lp)))
    zero = dict(mla=None, lru_conv=jnp.zeros((bp, 8, MIXW), F32), lru_h=jnp.zeros((bp, 1, MIXW), F32),
                gdn_conv=jnp.zeros((bp, 8, 3 * MIXW), F32), gdn_s=jnp.zeros((bp, HEADS, HD, HD), F32),
                rwkv_shift=jnp.zeros((bp, 8, D_W), F32), rwkv_s=jnp.zeros((bp, HEADS, HD, HD), F32))
    yp, p_new = _run_group(x0, lws, final_g, [zero] * depth, cfg)

    bs, ts, _ = x_sample.shape
    past = cache_mla_ckv.shape[2]
    cfg_s = _group_config(bs, ts)
    cfg_s["rope"] = tuple(jnp.tile(t, (bs, 1)) for t in _rope_tables(past + jnp.arange(ts)))
    cache_kpe_t = jnp.swapaxes(cache_mla_kpe, 2, 3)
    st_s = [dict(mla=(cache_mla_ckv, cache_kpe_t), lru_conv=_hist8(state_lru_conv[l]),
                 lru_h=state_lru_h[l][:, None].astype(F32), gdn_conv=_hist8(state_gdn_conv[l]),
                 gdn_s=state_gdn_s[l], rwkv_shift=_hist8(state_rwkv_shift[l][:, None]),
                 rwkv_s=state_rwkv_s[l]) for l in range(depth)]
    ys, s_new = _run_group(x_sample, lws, final_g, st_s, cfg_s)
    return (yp[:, N_META:tp], ys) + tuple(p_new) + tuple(s_new)
```

```python
import functools
import math

import jax
import jax.numpy as jnp
import numpy as np
from jax import lax
from jax.experimental import pallas as pl
from jax.experimental.pallas import tpu as pltpu

F32 = jnp.float32
BF16 = jnp.bfloat16
HIGHEST = lax.Precision.HIGHEST

EPS = 1e-6
N_META = 16
CHUNK = 64
CONV_W = 4
HEADS = 4
HD = 64
MIXW = HEADS * HD
MLA_NOPE = 64
MLA_ROPE = 32
MLA_QK = MLA_NOPE + MLA_ROPE
MLA_V = 64
MLA_PAD = 128
ROPE_THETA = 10000.0
LRU_C = 8.0
RWKV_GN_EPS = 64e-5
RWKV_DECAY_SCALE = -0.606531

VMEM_LIMIT = 56 * 1024 * 1024
MXU_N = 256

A_W = 640
B_W = 512
C_W = 1536
D_W = 1024


def _bdot(a, b):
    return jnp.dot(a.astype(BF16), b.astype(BF16), preferred_element_type=F32)


def _bdot_nt(a, b):
    return lax.dot_general(a.astype(BF16), b.astype(BF16), (((1,), (1,)), ((), ())),
                           preferred_element_type=F32)


def _bdot_tn(a, b):
    return lax.dot_general(a.astype(BF16), b.astype(BF16), (((0,), (0,)), ((), ())),
                           preferred_element_type=F32)


def _split3(x):
    x1 = x.astype(BF16)
    r1 = x - x1.astype(F32)
    x2 = r1.astype(BF16)
    r2 = r1 - x2.astype(F32)
    return x1, x2, r2.astype(BF16)


def _xdot_r01(x, m01):
    return jnp.dot(x.astype(BF16), m01, preferred_element_type=F32)


def _xdot_l01(m01, x):
    x1, x2, x3 = _split3(x)
    d = functools.partial(jnp.dot, preferred_element_type=F32)
    return d(m01, x1) + d(m01, x2) + d(m01, x3)


def _sigmoid(x):
    return 0.5 * jnp.tanh(0.5 * x) + 0.5


def _silu(x):
    return x * _sigmoid(x)


def _softplus(x):
    return jnp.maximum(x, 0.0) + jnp.log(1.0 + jnp.exp(-jnp.abs(x)))


def _gelu_tanh(x):
    return 0.5 * x * (1.0 + jnp.tanh(0.7978845608028654 * (x + 0.044715 * (x * x * x))))


def _rms_rows(x, g):
    return x * lax.rsqrt(jnp.mean(x * x, axis=-1, keepdims=True) + EPS) * g


def _block_diag_mask():
    r = lax.broadcasted_iota(jnp.int32, (MIXW, MIXW), 0) >> 6
    c = lax.broadcasted_iota(jnp.int32, (MIXW, MIXW), 1) >> 6
    return r == c


def _embed(x, bd01):
    c = x.shape[0]
    x = x.astype(BF16)
    if c < HD:
        x = jnp.concatenate([x, jnp.zeros((HD - c, MIXW), BF16)], axis=0)
    return jnp.concatenate([x] * HEADS, axis=0) * bd01


def _mm(a, b):
    return jnp.dot(a.astype(BF16), b, preferred_element_type=F32)


def _mm_nt(a, b):
    return lax.dot_general(a.astype(BF16), b, (((1,), (1,)), ((), ())), preferred_element_type=F32)


def _head_masks(c):
    row = lax.broadcasted_iota(jnp.int32, (c, MIXW), 0)
    col = lax.broadcasted_iota(jnp.int32, (c, MIXW), 1) & (HD - 1)
    return row >= col, row > col, row == col


def _tril01(c):
    r = lax.broadcasted_iota(jnp.int32, (c, c), 0)
    k = lax.broadcasted_iota(jnp.int32, (c, c), 1)
    return jnp.where(r >= k, 1.0, 0.0).astype(BF16)


def _inv_unit_lower(lms, eye4, bd01):
    c = lms[0].shape[0]
    levels = int(math.log2(c))
    eye = jnp.where(eye4, 1.0, 0.0)
    n = [-x for x in lms]
    t = [eye + x for x in n]
    if levels < 2:
        return t
    p = [_mm(x, _embed(x, bd01)) for x in n]
    for _ in range(levels - 2):
        r = [_mm(jnp.concatenate([pi, ti], axis=0), _embed(pi, bd01)) for pi, ti in zip(p, t)]
        p = [x[0:c] for x in r]
        t = [ti + x[c:2 * c] for ti, x in zip(t, r)]
    return [ti + _mm(ti, _embed(pi, bd01)) for pi, ti in zip(p, t)]


def _load_state(st_ref, s0_ref, s):
    st_ref[s] = jnp.zeros((MIXW, MIXW), F32)
    for h in range(HEADS):
        st_ref[s, h * HD:(h + 1) * HD, h * HD:(h + 1) * HD] = s0_ref[s, h].astype(F32)


def _store_state(so_ref, st_ref, s):
    for h in range(HEADS):
        so_ref[s, h] = st_ref[s, h * HD:(h + 1) * HD, h * HD:(h + 1) * HD]


def _shift_rows(x, s, fill):
    rolled = pltpu.roll(x, s, 0)
    row = lax.broadcasted_iota(jnp.int32, x.shape, 0)
    return jnp.where(row >= s, rolled, fill)


def _causal_conv(xext_ref, w_ref, tt):
    acc = None
    for j in range(CONV_W):
        term = xext_ref[pl.ds(8 - (CONV_W - 1) + j, tt), :] * w_ref[j:j + 1, :]
        acc = term if acc is None else acc + term
    return acc


def _const_spec(shape):
    nd = len(shape)
    return pl.BlockSpec(shape, lambda *_: (0,) * nd, pipeline_mode=pl.Buffered(1))


def _token_kernel(*refs, has_pre, has_post, has_final, d_ff, ff_chunk):
    it = iter(refs)
    x_ref = next(it)
    if has_pre:
        y_refs = [next(it) for _ in range(4)]
        wo_refs = [next(it) for _ in range(4)]
    gffn_ref, w1_ref, w2_ref = next(it), next(it), next(it)
    if has_post:
        gmix_ref, win_ref = next(it), next(it)
        prep_in = [next(it) for _ in range(8)]
    if has_final:
        gfin_ref = next(it)
    xo_ref = next(it)
    if has_post:
        prep_out = [next(it) for _ in range(5)]
        col_refs = [next(it) for _ in range(3)]

    x = x_ref[...]
    if has_pre:
        for y_ref, wo_ref in zip(y_refs, wo_refs):
            x = x + jnp.dot(y_ref[...], wo_ref[...], preferred_element_type=F32)
    h = _rms_rows(x, gffn_ref[...]).astype(BF16)
    acc = None
    for lo, hi in ((0, ff_chunk), (ff_chunk, d_ff)):
        gate = jnp.dot(h, w1_ref[:, lo:hi], preferred_element_type=F32)
        up = jnp.dot(h, w1_ref[:, d_ff + lo:d_ff + hi], preferred_element_type=F32)
        act = (_silu(gate) * up).astype(BF16)
        part = jnp.dot(act, w2_ref[lo:hi, :], preferred_element_type=F32)
        acc = part if acc is None else acc + part
    x = x + 0.5 * acc
    if has_final:
        xo_ref[...] = _rms_rows(x, gfin_ref[...])
    else:
        xo_ref[...] = x
    if has_post:
        hm = _rms_rows(x, gmix_ref[...]).astype(BF16)
        _mla_prep(jnp.dot(hm, win_ref[:, 0:A_W], preferred_element_type=F32), *prep_in, *prep_out)
        off = A_W
        for c_ref in col_refs:
            wd = c_ref.shape[-1]
            c_ref[...] = jnp.dot(hm, win_ref[:, off:off + wd], preferred_element_type=F32)
            off += wd


def _token_call(x, lw, *, pre=None, post=None, final_g=None, tm):
    n, d = x.shape
    d_ff = lw["w2"].shape[1]
    ff_chunk = -(-(d_ff // MXU_N) // 2) * MXU_N
    row = lambda w: pl.BlockSpec((tm, w), lambda i: (i, 0))
    args, specs = [x], [row(d)]
    if pre is not None:
        ys, wos = pre
        for y in ys:
            args.append(y)
            specs.append(row(y.shape[1]))
        for w in wos:
            args.append(w)
            specs.append(_const_spec(w.shape))
    args.append(lw["g_ffn"])
    specs.append(_const_spec(lw["g_ffn"].shape))
    layer = lw["layer"]
    for name in ("w1", "w2"):
        args.append(lw[name])
        specs.append(pl.BlockSpec((None,) + lw[name].shape[1:], lambda i: (layer, 0, 0),
                                  pipeline_mode=pl.Buffered(1)))
    if post is not None:
        cos_t, sin_t, mw, (b, lp, t_valid) = post
        for name in ("g_mix", "w_in"):
            args.append(lw[name])
            specs.append(_const_spec(lw[name].shape))
        args += [cos_t, sin_t]
        specs += [row(MLA_PAD), row(MLA_PAD)]
        for name in ("mla_gqa", "mla_wq", "mla_gkva", "mla_wkv", "mla_gq", "mla_gk"):
            args.append(mw[name])
            specs.append(_const_spec(mw[name].shape))
    if final_g is not None:
        args.append(final_g)
        specs.append(_const_spec(final_g.shape))
    out_shape = [jax.ShapeDtypeStruct((n, d), F32)]
    out_specs = [row(d)]
    if post is not None:
        hw = HEADS * MLA_PAD
        out_shape += [jax.ShapeDtypeStruct((n, hw), BF16)] * 3
        out_specs += [row(hw)] * 3
        for wd in (128, MLA_ROPE):
            if lp > t_valid:
                tps = lp // tm
                out_shape.append(jax.ShapeDtypeStruct((b, t_valid, wd), F32))
                out_specs.append(pl.BlockSpec((None, tm, wd), lambda i: (i // tps, i % tps, 0)))
            else:
                out_shape.append(jax.ShapeDtypeStruct((n, wd), F32))
                out_specs.append(row(wd))
        for wd in (B_W, C_W, D_W):
            out_shape.append(jax.ShapeDtypeStruct((n, wd), F32))
            out_specs.append(row(wd))
    body = functools.partial(_token_kernel, has_pre=pre is not None, has_post=post is not None,
                             has_final=final_g is not None, d_ff=d_ff, ff_chunk=ff_chunk)
    return pl.pallas_call(
        body, grid=(n // tm,), in_specs=specs, out_specs=out_specs, out_shape=out_shape,
        compiler_params=pltpu.CompilerParams(dimension_semantics=("parallel",),
                                             vmem_limit_bytes=VMEM_LIMIT),
        name="token_block")(*args)


def _mla_prep(ca, cos_ref, sin_ref, gqa_ref, wq_ref, gkva_ref, wkv_ref, gq_ref, gk_ref,
              q_ref, k_ref, v_ref, ckv_ref, kpe_ref):
    cs = cos_ref[...]
    sn = sin_ref[...]
    hw = HEADS * MLA_PAD
    cqn = _rms_rows(ca[:, 0:256], gqa_ref[...])
    qq = _bdot(cqn, wq_ref[...])
    scale = math.log2(math.e) / math.sqrt(MLA_QK)
    for h in range(HEADS):
        lo = h * MLA_PAD
        blk = qq[:, lo:lo + MLA_PAD] * cs + qq[:, hw + lo:hw + lo + MLA_PAD] * sn
        ms = jnp.sum(blk * blk, axis=-1, keepdims=True) * (1.0 / MLA_QK)
        q_ref[:, lo:lo + MLA_PAD] = (blk * lax.rsqrt(ms + EPS) * (gq_ref[...] * scale)).astype(BF16)
    ckv = _rms_rows(ca[:, 256:384], gkva_ref[...])
    ckv_ref[...] = ckv
    kpe = ca[:, 384:512] * cs + ca[:, 512:640] * sn
    kpe_ref[...] = kpe[:, 0:MLA_ROPE]
    kv = _bdot(ckv, wkv_ref[...])
    ones_lane = lax.broadcasted_iota(jnp.int32, (kv.shape[0], MLA_PAD), 1) == MLA_V
    for h in range(HEADS):
        lo = h * MLA_PAD
        kb = kv[:, lo:lo + MLA_PAD] + kpe
        ms = jnp.sum(kb * kb, axis=-1, keepdims=True) * (1.0 / MLA_QK)
        k_ref[:, lo:lo + MLA_PAD] = (kb * lax.rsqrt(ms + EPS) * gk_ref[...]).astype(BF16)
        v_ref[:, lo:lo + MLA_PAD] = jnp.where(ones_lane, 1.0, kv[:, hw + lo:hw + lo + MLA_PAD]).astype(BF16)


def _softmax_step(carry, s, vblk):
    m, acc = carry
    m_new = jnp.maximum(m, jnp.max(s, axis=-1, keepdims=True))
    p = jnp.exp2(s - m_new)
    acc = jnp.exp2(m - m_new) * acc + jnp.dot(p.astype(BF16), vblk, preferred_element_type=F32)
    return m_new, acc


def _softmax_init(t):
    return jnp.full((t, 1), -1e30, F32), jnp.zeros((t, MLA_PAD), F32)


def _softmax_finish(acc):
    return (acc / acc[:, MLA_V:MLA_V + 1]).astype(BF16)


def _mla_prompt_kernel(q_ref, k_ref, v_ref, o_ref, *, tq, tk, n_kb):
    q0 = pl.program_id(1) * tq
    first_chunk = (q0 - N_META) >> 6
    last_chunk = (q0 + tq - 1 - N_META) >> 6
    n_full = (N_META + CHUNK * (first_chunk + 1)) // tk
    nkb = jnp.minimum((N_META + CHUNK * (last_chunk + 1)) // tk, n_kb)
    qchunk = (q0 + lax.broadcasted_iota(jnp.int32, (tq, 1), 0) - N_META) >> 6
    hs = range(HEADS)
    qs = [q_ref[:, h * MLA_PAD:(h + 1) * MLA_PAD] for h in hs]

    def body(it, state, masked, nblk):
        k0s = [pl.multiple_of((it * nblk + j) * tk, tk) for j in range(nblk)]
        blk = [slice(h * MLA_PAD, (h + 1) * MLA_PAD) for h in hs]
        s = [[lax.dot_general(qs[h], k_ref[pl.ds(k0, tk), blk[h]], (((1,), (1,)), ((), ())),
                              preferred_element_type=F32) for k0 in k0s] for h in hs]
        if masked:
            vis = [((k0 + lax.broadcasted_iota(jnp.int32, (1, tk), 1) - N_META) >> 6) <= qchunk for k0 in k0s]
            s = [[jnp.where(vis[j], s[h][j], -1e30) for j in range(nblk)] for h in hs]
        m_new = []
        for h in hs:
            m = state[h][0]
            for x in s[h]:
                m = jnp.maximum(m, jnp.max(x, axis=-1, keepdims=True))
            m_new.append(m)
        p = [[jnp.exp2(x - m_new[h]).astype(BF16) for x in s[h]] for h in hs]
        pv = []
        for h in hs:
            acc = None
            for j, k0 in enumerate(k0s):
                part = jnp.dot(p[h][j], v_ref[pl.ds(k0, tk), blk[h]], preferred_element_type=F32)
                acc = part if acc is None else acc + part
            pv.append(acc)
        return tuple((m_new[h], jnp.exp2(state[h][0] - m_new[h]) * state[h][1] + pv[h]) for h in hs)

    state = tuple(_softmax_init(tq) for _ in hs)
    n_pair = n_full // 2
    state = lax.fori_loop(0, n_pair, functools.partial(body, masked=False, nblk=2), state)
    state = lax.fori_loop(2 * n_pair, nkb, functools.partial(body, masked=True, nblk=1), state)
    lp = n_kb * tk
    k_t = pl.multiple_of(jnp.minimum(nkb * tk, lp - N_META), N_META)
    tail_chunk = (k_t + lax.broadcasted_iota(jnp.int32, (1, N_META), 1) - N_META) >> 6
    visible = (tail_chunk <= qchunk) & (nkb < n_kb)
    for h in hs:
        blk = slice(h * MLA_PAD, (h + 1) * MLA_PAD)
        s = lax.dot_general(qs[h], k_ref[pl.ds(k_t, N_META), blk], (((1,), (1,)), ((), ())),
                            preferred_element_type=F32)
        _, acc = _softmax_step(state[h], jnp.where(visible, s, -1e30), v_ref[pl.ds(k_t, N_META), blk])
        o_ref[:, blk] = _softmax_finish(acc)


def _mla_prompt_call(q, k, v, *, tq, tk):
    b, lp, hw = q.shape
    assert tq == tk and tq % CHUNK == 0 and lp % tk == 0
    body = functools.partial(_mla_prompt_kernel, tq=tq, tk=tk, n_kb=lp // tk)
    return pl.pallas_call(
        body, grid=(b, lp // tq),
        in_specs=[pl.BlockSpec((None, tq, hw), lambda i, j: (i, j, 0)),
                  pl.BlockSpec((None, lp, hw), lambda i, j: (i, 0, 0)),
                  pl.BlockSpec((None, lp, hw), lambda i, j: (i, 0, 0))],
        out_specs=pl.BlockSpec((None, tq, hw), lambda i, j: (i, j, 0)),
        out_shape=jax.ShapeDtypeStruct((b, lp, hw), BF16),
        compiler_params=pltpu.CompilerParams(dimension_semantics=("parallel", "arbitrary"),
                                             vmem_limit_bytes=VMEM_LIMIT),
        name="mla_prompt_attn")(q, k, v)


def _mla_sample_kernel(q_ref, kn_ref, vn_ref, cckv_ref, ckpet_ref, wkt_ref, wv_ref, gk_ref, o_ref):
    t = q_ref.shape[0]
    hs = range(HEADS)
    blk = [slice(h * MLA_PAD, (h + 1) * MLA_PAD) for h in hs]
    ckv = cckv_ref[...].astype(BF16)
    knt = lax.dot_general(wkt_ref[...], ckv, (((1,), (1,)), ((), ())), preferred_element_type=F32)
    kpet = ckpet_ref[...]
    ssq_pe = jnp.sum(kpet * kpet, axis=0, keepdims=True)
    kpet16 = kpet.astype(BF16)
    zpad = jnp.zeros((MLA_PAD - MLA_QK, kpet.shape[1]), BF16)
    kts = [knt[h * MLA_NOPE:(h + 1) * MLA_NOPE, :] for h in hs]
    rk = [lax.rsqrt((jnp.sum(x * x, axis=0, keepdims=True) + ssq_pe) * (1.0 / MLA_QK) + EPS) for x in kts]
    qg = [(q_ref[:, blk[h]].astype(F32) * gk_ref[...]).astype(BF16) for h in hs]
    s_c = [jnp.dot(qg[h], jnp.concatenate([kpet16, kts[h].astype(BF16), zpad], axis=0),
                   preferred_element_type=F32) * rk[h] for h in hs]
    s_n = [lax.dot_general(q_ref[:, blk[h]], kn_ref[:, blk[h]], (((1,), (1,)), ((), ())),
                           preferred_element_type=F32) for h in hs]
    m = [jnp.maximum(jnp.max(s_c[h], axis=-1, keepdims=True), jnp.max(s_n[h], axis=-1, keepdims=True))
         for h in hs]
    p_c = [jnp.exp2(s_c[h] - m[h]) for h in hs]
    p_n = [jnp.exp2(s_n[h] - m[h]) for h in hs]
    den = [jnp.sum(p_c[h], axis=-1, keepdims=True) + jnp.sum(p_n[h], axis=-1, keepdims=True) for h in hs]
    lat = jnp.dot(jnp.concatenate([x.astype(BF16) for x in p_c], axis=0), ckv, preferred_element_type=F32)
    for h in hs:
        o = jnp.dot(lat[h * t:(h + 1) * t].astype(BF16), wv_ref[:, blk[h]], preferred_element_type=F32)
        o = o + jnp.dot(p_n[h].astype(BF16), vn_ref[:, blk[h]], preferred_element_type=F32)
        o_ref[:, blk[h]] = (o / den[h]).astype(BF16)


def _mla_sample_call(q, kn, vn, cache_ckv, cache_kpe_t, layer, lw):
    b, t, hw = q.shape
    past = cache_ckv.shape[2]
    seq = lambda: pl.BlockSpec((None, t, hw), lambda i: (i, 0, 0))
    consts = [lw["mla_wkt"], lw["mla_wv"], lw["mla_gk"]]
    return pl.pallas_call(
        _mla_sample_kernel, grid=(b,),
        in_specs=[seq(), seq(), seq(),
                  pl.BlockSpec((None, None, past, cache_ckv.shape[3]), lambda i: (layer, i, 0, 0)),
                  pl.BlockSpec((None, None, cache_kpe_t.shape[2], past), lambda i: (layer, i, 0, 0))]
        + [_const_spec(c.shape) for c in consts],
        out_specs=seq(),
        out_shape=jax.ShapeDtypeStruct((b, t, hw), BF16),
        compiler_params=pltpu.CompilerParams(dimension_semantics=("arbitrary",),
                                             vmem_limit_bytes=VMEM_LIMIT),
        name="mla_sample_attn")(q, kn, vn, cache_ckv, cache_kpe_t, *consts)


def _lru_kernel(cb_ref, buf_ref, h0_ref, cw_ref, cbias_ref, wa_ref, ba_ref, wx_ref, bx_ref, lam_ref,
                y_ref, nbuf_ref, hl_ref, xext, hcar, sa, sb, sh, *, tt, t_valid):
    ti = pl.program_id(1)

    @pl.when(ti == 0)
    def _():
        xext[0:8, :] = buf_ref[...]
        hcar[...] = h0_ref[...]

    @pl.when(ti > 0)
    def _():
        xext[0:8, :] = xext[tt:tt + 8, :]

    xext[8:tt + 8, :] = cb_ref[:, 0:MIXW]
    xc = _causal_conv(xext, cw_ref, tt) + cbias_ref[...]
    r = _sigmoid(_bdot(xc, wa_ref[...]) + ba_ref[...])
    i = _sigmoid(_bdot(xc, wx_ref[...]) + bx_ref[...])
    log_a = -LRU_C * r * _softplus(-lam_ref[...])
    pos = ti * tt + lax.broadcasted_iota(jnp.int32, (tt, 1), 0)
    log_a = jnp.where(pos < t_valid, log_a, 0.0)
    a = jnp.exp(log_a)
    b2 = -jnp.tanh(log_a) * (a * a + 1.0)
    b = jnp.where(b2 > 0.0, b2 * lax.rsqrt(b2), 0.0) * (i * xc)
    g = tt // 8
    a = a.reshape(g, 8, MIXW)
    b = b.reshape(g, 8, MIXW)
    sub = lax.broadcasted_iota(jnp.int32, (g, 8, 1), 1)
    for s in (1, 2, 4):
        keep = sub >= s
        b = a * jnp.where(keep, pltpu.roll(b, s, 1), 0.0) + b
        a = a * jnp.where(keep, pltpu.roll(a, s, 1), 1.0)
    a = a.reshape(tt, MIXW)
    b = b.reshape(tt, MIXW)
    halves = [slice(0, 128), slice(128, MIXW)]
    for k, lanes in enumerate(halves):
        sa[k] = a[:, lanes]
        sb[k] = b[:, lanes]
    ag =jnp.concatenate([sa[k, pl.ds(7, g, stride=8), :] for k in range(2)], axis=1)
    bg = jnp.concatenate([sb[k, pl.ds(7, g, stride=8), :] for k in range(2)], axis=1)
    if g % 8 == 0:
        s = 1
        while s < g:
            bg = ag * _shift_rows(bg, s, 0.0) + bg
            ag = ag * _shift_rows(ag, s, 1.0)
            s *= 2
        h_end = ag * hcar[...] + bg
        h_in = _shift_rows(h_end, 1, hcar[...])
        hcar[...] = h_end[g - 1:g, :]
    else:
        states = [hcar[...]]
        for j in range(g):
            states.append(ag[j:j + 1, :] * states[-1] + bg[j:j + 1, :])
        h_in = jnp.concatenate(states[:g], axis=0)
        hcar[...] = states[g]
    for r in range(8):
        rows = pl.ds(r, g, stride=8)
        for k, lanes in enumerate(halves):
            sh[k, rows, :] = sa[k, rows, :] * h_in[:, lanes] + sb[k, rows, :]
    h = jnp.concatenate([sh[0], sh[1]], axis=1)
    y_ref[...] = (h * _gelu_tanh(cb_ref[:, MIXW:2 * MIXW])).astype(BF16)

    last = t_valid - 1

    @pl.when(ti == last // tt)
    def _():
        r0 = last % tt
        hl_ref[...] = h[r0:r0 + 1, :]
        nbuf_ref[...] = xext[pl.ds(8 + r0 - (CONV_W - 2), CONV_W - 1), :]


def _lru_call(cb, buf8, h0, lw, *, tt, t_valid):
    b, lp, _ = cb.shape
    consts = [lw["lru_cw"], lw["lru_cb"], lw["lru_wa"], lw["lru_ba"], lw["lru_wx"], lw["lru_bx"],
              lw["lru_lam"]]
    body = functools.partial(_lru_kernel, tt=tt, t_valid=t_valid)
    per_b = lambda s: pl.BlockSpec((None,) + s, lambda i, j: (i, 0, 0))
    return pl.pallas_call(
        body, grid=(b, lp // tt),
        in_specs=[pl.BlockSpec((None, tt, B_W), lambda i, j: (i, j, 0)), per_b((8, MIXW)),
                  per_b((1, MIXW))] + [_const_spec(c.shape) for c in consts],
        out_specs=[pl.BlockSpec((None, tt, MIXW), lambda i, j: (i, j, 0)),
                   per_b((CONV_W - 1, MIXW)), per_b((1, MIXW))],
        out_shape=[jax.ShapeDtypeStruct((b, lp, MIXW), BF16),
                   jax.ShapeDtypeStruct((b, CONV_W - 1, MIXW), F32),
                   jax.ShapeDtypeStruct((b, 1, MIXW), F32)],
        scratch_shapes=[pltpu.VMEM((tt + 8, MIXW), F32), pltpu.VMEM((1, MIXW), F32)]
        + [pltpu.VMEM((2, tt, 128), F32)] * 3,
        compiler_params=pltpu.CompilerParams(dimension_semantics=("parallel", "arbitrary"),
                                             vmem_limit_bytes=VMEM_LIMIT),
        name="rglru")(cb, buf8, h0, *consts)


def _gdn_kernel(cc_ref, buf_ref, s0_ref, cw_ref, alog_ref, dtb_ref, gon_ref,
                y_ref, nbuf_ref, so_ref, xext, q_s, k_s, v_s, b_s, g_s, o_s, u_s, a_s, w_s, e_s, l_s,
                m_s, h_s, n_s, st_ref, *, nb, n_tiles, tt, c, t_valid):
    ti = pl.program_id(1)
    qkw = 3 * MIXW
    bd = _block_diag_mask()
    bd01 = jnp.where(bd, 1.0, 0.0).astype(BF16)
    pos = ti * tt + lax.broadcasted_iota(jnp.int32, (tt, 1), 0)
    valid = pos < t_valid

    for s in range(nb):
        xe = xext.at[s]
        sr = slice(s * tt, (s + 1) * tt)

        @pl.when(ti == 0)
        def _():
            xe[0:8, :] = buf_ref[s]
            _load_state(st_ref, s0_ref, s)

        @pl.when(ti > 0)
        def _():
            xe[0:8, :] = xe[tt:tt + 8, :]

        xe[8:tt + 8, :] = cc_ref[s, :, 0:qkw]
        xs = _silu(_causal_conv(xe, cw_ref, tt))
        q = xs[:, 0:MIXW]
        k = xs[:, MIXW:2 * MIXW]
        q_s[sr, :] = q * lax.rsqrt(_xdot_r01(q * q, bd01) + EPS) * (HD ** -0.5)
        k_s[sr, :] = k * lax.rsqrt(_xdot_r01(k * k, bd01) + EPS)
        v_s[sr, :] = xs[:, 2 * MIXW:3 * MIXW]
        a_in = cc_ref[s, :, qkw + MIXW:qkw + 2 * MIXW]
        b_in = cc_ref[s, :, qkw + 2 * MIXW:qkw + 3 * MIXW]
        b_s[sr, :] = jnp.where(valid, _sigmoid(b_in), 0.0)
        g_s[sr, :] = jnp.where(valid, -jnp.exp(alog_ref[...]) * _softplus(a_in + dtb_ref[...]), 0.0)

    tril4, strict4, eye4 = _head_masks(c)
    tril01 = _tril01(c)
    n_chunks = tt // c

    ch = range(nb * n_chunks)
    rs = [slice(ci * c, (ci + 1) * c) for ci in ch]
    gc = [_xdot_l01(tril01, g_s[r, :]) for r in rs]
    gct = [jnp.sum(jnp.where(eye4, x, 0.0), axis=0, keepdims=True) for x in gc]
    decay = [jnp.where(tril4, jnp.exp(jnp.where(tril4, gc[i] - gct[i], 0.0)), 0.0) for i in ch]
    kc = [k_s[r, :] for r in rs]
    beta = [b_s[r, :] for r in rs]
    kb = [kc[i] * beta[i] for i in ch]
    sc = [_mm_nt(jnp.concatenate([kb[i], q_s[rs[i], :]], axis=0), _embed(kc[i], bd01)) for i in ch]
    tinv = _inv_unit_lower([jnp.where(strict4, sc[i][0:c] * decay[i], 0.0) for i in ch], eye4, bd01)
    attn = [jnp.where(tril4, sc[i][c:2 * c] * decay[i], 0.0).astype(BF16) for i in ch]
    egc = [jnp.exp(x) for x in gc]
    uw = [_mm(tinv[i], jnp.concatenate([_embed(v_s[rs[i], :] * beta[i], bd01),
                                        _embed(kb[i] * egc[i], bd01)], axis=1)) for i in ch]
    glast = [x[c - 1:c, :] for x in gc]
    kd = [(kc[i] * jnp.exp(glast[i] - gc[i])).astype(BF16) for i in ch]
    an = [lax.dot_general(kd[i], jnp.concatenate([uw[i][:, MIXW:], uw[i][:, :MIXW]], axis=1).astype(BF16),
                          (((0,), (0,)), ((), ())), preferred_element_type=F32) for i in ch]
    for i in ch:
        u_s[rs[i], :] = uw[i][:, 0:MIXW]
        w_s[rs[i], :] = uw[i][:, MIXW:2 * MIXW].astype(BF16)
        e_s[rs[i], :] = (q_s[rs[i], :] * egc[i]).astype(BF16)
        a_s[rs[i], :] = attn[i]
        m_s[i] = jnp.where(bd, an[i][:, 0:MIXW], 0.0).astype(BF16)
        n_s[i] = jnp.where(bd, an[i][:, MIXW:2 * MIXW], 0.0)
        l_s[i:i + 1, :] = jnp.exp(glast[i])

    for ci in range(n_chunks):
        for s in range(nb):
            i = s * n_chunks + ci
            s_bd = st_ref[s]
            s16 = s_bd.astype(BF16)
            h_s[i] = s16
            st_ref[s] = s_bd * l_s[i:i + 1, :] + n_s[i] - jnp.dot(m_s[i], s16, preferred_element_type=F32)

    r = [jnp.dot(jnp.concatenate([w_s[rs[i], :], e_s[rs[i], :]], axis=0), h_s[i], preferred_element_type=F32)
         for i in ch]
    v_new = [u_s[rs[i], :] - r[i][0:c] for i in ch]
    ov = [jnp.dot(a_s[rs[i], :], _embed(v_new[i], bd01), preferred_element_type=F32) for i in ch]
    for i in ch:
        o_s[rs[i], :] = r[i][c:2 * c] + ov[i]

    last = t_valid - 1
    for s in range(nb):
        o = o_s[s * tt:(s + 1) * tt, :]
        ms = _xdot_r01(o * o, bd01) * (1.0 / HD)
        z = cc_ref[s, :, qkw:qkw + MIXW]
        y_ref[s] = (o * lax.rsqrt(ms + EPS) * gon_ref[...] * _silu(z)).astype(BF16)

        @pl.when(ti == last // tt)
        def _():
            nbuf_ref[s] = xext[s, pl.ds(8 + last % tt - (CONV_W - 2), CONV_W - 1), :]

        @pl.when(ti == n_tiles - 1)
        def _():
            _store_state(so_ref, st_ref, s)


def _gdn_call(cc, buf8, s0, lw, *, nb, tt, c, t_valid):
    b, lp, _ = cc.shape
    qkw = 3 * MIXW
    consts = [lw["gdn_cw"], lw["gdn_alog"], lw["gdn_dtb"], lw["gdn_gon"]]
    body = functools.partial(_gdn_kernel, nb=nb, n_tiles=lp // tt, tt=tt, c=c, t_valid=t_valid)
    per_b = lambda s: pl.BlockSpec((nb,) + s, lambda i, j: (i,) + (0,) * len(s))
    n_ch = nb * (tt // c)
    return pl.pallas_call(
        body, grid=(b // nb, lp // tt),
        in_specs=[pl.BlockSpec((nb, tt, C_W), lambda i, j: (i, j, 0)), per_b((8, qkw)),
                  per_b((HEADS, HD, HD))] + [_const_spec(x.shape) for x in consts],
        out_specs=[pl.BlockSpec((nb, tt, MIXW), lambda i, j: (i, j, 0)),
                   per_b((CONV_W - 1, qkw)), per_b((HEADS, HD, HD))],
        out_shape=[jax.ShapeDtypeStruct((b, lp, MIXW), BF16),
                   jax.ShapeDtypeStruct((b, CONV_W - 1, qkw), F32),
                   jax.ShapeDtypeStruct((b, HEADS, HD, HD), F32)],
        scratch_shapes=[pltpu.VMEM((nb, tt + 8, qkw), F32)] + [pltpu.VMEM((nb * tt, MIXW), F32)] * 7
        + [pltpu.VMEM((nb * tt, MIXW), BF16)] * 3 + [pltpu.VMEM((-(-n_ch // 8) * 8, MIXW), F32),
                                                     pltpu.VMEM((n_ch, MIXW, MIXW), BF16),
                                                     pltpu.VMEM((n_ch, MIXW, MIXW), BF16),
                                                     pltpu.VMEM((n_ch, MIXW, MIXW), F32),
                                                     pltpu.VMEM((nb, MIXW, MIXW), F32)],
        compiler_params=pltpu.CompilerParams(dimension_semantics=("parallel", "arbitrary"),
                                             vmem_limit_bytes=VMEM_LIMIT),
        name="gated_delta")(cc, buf8, s0, *consts)


def _rwkv_kernel(cd_ref, prev_ref, s0_ref, mu_ref, w0_ref, wb_ref, a0_ref, ab_ref, gb_ref, kk_ref,
                 ka_ref, rk_ref, lnw_ref, lnb_ref,
                 y_ref, shift_ref, so_ref, xext, r_s, k_s, v_s, n_s, b_s, w_s, o_s, xv_s,
                 t_s, a_s, lhs_s, p_s, g_s, h_s, d_s, rk_s, gt_s, st_ref, *, nb, n_tiles, tt, c, t_valid):
    ti = pl.program_id(1)
    pos = ti * tt + lax.broadcasted_iota(jnp.int32, (tt, 1), 0)
    valid = pos < t_valid
    bd = _block_diag_mask()
    bd01 = jnp.where(bd, 1.0, 0.0).astype(BF16)

    for s in range(nb):
        xe = xext.at[s]
        sr = slice(s * tt, (s + 1) * tt)

        @pl.when(ti == 0)
        def _():
            xe[0:8, :] = prev_ref[s]
            _load_state(st_ref, s0_ref, s)

        @pl.when(ti > 0)
        def _():
            xe[0:8, :] = xe[tt:tt + 8, :]

        x = cd_ref[s]
        xe[8:tt + 8, :] = x
        xm = x + (xe[pl.ds(7, tt), :] - x) * mu_ref[...]
        r = xm[:, 0:MIXW]
        k = xm[:, MIXW:2 * MIXW]
        lo_blk = xm[:, 3 * MIXW:3 * MIXW + 128]
        logw = RWKV_DECAY_SCALE * _sigmoid(w0_ref[...] + _bdot(jnp.tanh(lo_blk), wb_ref[...]))
        a = _sigmoid(a0_ref[...] + _bdot(lo_blk, ab_ref[...]))
        kkv = k * kk_ref[...]
        kkn = kkv * lax.rsqrt(_xdot_r01(kkv * kkv, bd01) + EPS)
        kkn = jnp.where(valid, kkn, 0.0)
        kmod = k * (1.0 + (a - 1.0) * ka_ref[...])
        r_s[sr, :] = r
        k_s[sr, :] = jnp.where(valid, kmod, 0.0)
        v_s[sr, :] = xm[:, 2 * MIXW:3 * MIXW]
        n_s[sr, :] = kkn
        b_s[sr, :] = kkn * a
        w_s[sr, :] = jnp.where(valid, logw, 0.0)
        rk_s[sr, :] = _xdot_r01(r * kmod * rk_ref[...], bd01)
        gt_s[sr, :] = _bdot(_sigmoid(xm[:, 3 * MIXW + 128:3 * MIXW + 256]), gb_ref[...])

    tril4, strict4, eye4 = _head_masks(c)
    tril01 = _tril01(c)

    n_chunks = tt // c

    ch = range(nb * n_chunks)
    rs = [slice(ci * c, (ci + 1) * c) for ci in ch]
    lw = [w_s[r, :] for r in rs]
    cum = [_xdot_l01(tril01, x) for x in lw]
    ecum = [jnp.exp(x) for x in cum]
    pin = [jnp.exp(-x) for x in cum]
    pc = [x[c - 1:c, :] for x in ecum]
    kinv = [k_s[rs[i], :] * pin[i] for i in ch]
    binv = [b_s[rs[i], :] * pin[i] for i in ch]
    nd = [n_s[rs[i], :] * jnp.exp(cum[i] - lw[i]) for i in ch]
    lhs = [jnp.concatenate([nd[i], r_s[rs[i], :] * ecum[i]], axis=0).astype(BF16) for i in ch]
    sk = [_mm_nt(lhs[i], _embed(kinv[i], bd01)) for i in ch]
    sb = [_mm_nt(lhs[i], _embed(binv[i], bd01)) for i in ch]
    tinv = _inv_unit_lower([jnp.where(strict4, sb[i][0:c], 0.0) for i in ch], eye4, bd01)
    av = [_mm(jnp.concatenate([jnp.where(strict4, sk[i][0:c], 0.0), jnp.where(tril4, sk[i][c:2 * c], 0.0)],
                              axis=0), _embed(v_s[rs[i], :], bd01)) for i in ch]
    tn = [_mm(tinv[i], jnp.concatenate([_embed(nd[i], bd01), _embed(av[i][0:c], bd01)], axis=1)) for i in ch]
    bp = [(binv[i] * pc[i]).astype(BF16) for i in ch]
    gh = [lax.dot_general(tn[i].astype(BF16), bp[i], (((0,), (0,)), ((), ())), preferred_element_type=F32)
          for i in ch]
    vk = [_bdot_tn(v_s[rs[i], :], kinv[i] * pc[i]) for i in ch]
    for i in ch:
        lhs_s[2 * i * c:2 * (i + 1) * c, :] = lhs[i]
        t_s[rs[i], :] = tinv[i].astype(BF16)
        a_s[rs[i], :] = jnp.where(tril4, sb[i][c:2 * c], 0.0).astype(BF16)
        xv_s[rs[i], :] = av[i][0:c]
        o_s[rs[i], :] = av[i][c:2 * c]
        p_s[i:i + 1, :] = pc[i]
        g_s[i] = jnp.where(bd, gh[i][0:MIXW], 0.0).astype(BF16)
        d_s[i] = jnp.where(bd, vk[i] - gh[i][MIXW:2 * MIXW], 0.0)

    for ci in range(n_chunks):
        for s in range(nb):
            i = s * n_chunks + ci
            st = st_ref[s]
            s16 = st.astype(BF16)
            h_s[i] = s16
            st_ref[s] = st * p_s[i:i + 1, :] + d_s[i] - jnp.dot(s16, g_s[i], preferred_element_type=F32)

    r2 = [lax.dot_general(lhs_s[2 * i * c:2 * (i + 1) * c, :], h_s[i], (((1,), (1,)), ((), ())),
                          preferred_element_type=F32) for i in ch]
    u = [jnp.dot(t_s[rs[i], :], _embed(r2[i][0:c] + xv_s[rs[i], :], bd01), preferred_element_type=F32)
         for i in ch]
    au = [jnp.dot(a_s[rs[i], :], _embed(u[i], bd01), preferred_element_type=F32) for i in ch]
    for i in ch:
        o_s[rs[i], :] = r2[i][c:2 * c] + o_s[rs[i], :] - au[i]

    last = t_valid - 1
    for s in range(nb):
        sr = slice(s * tt, (s + 1) * tt)
        o = o_s[sr, :]
        mean = _xdot_r01(o, bd01) * (1.0 / HD)
        d = o - mean
        var = _xdot_r01(d * d, bd01) * (1.0 / HD)
        o = d * lax.rsqrt(var + RWKV_GN_EPS) * lnw_ref[...] + lnb_ref[...]
        o = o + rk_s[sr, :] * v_s[sr, :]
        y_ref[s] = (o * gt_s[sr, :]).astype(BF16)

        @pl.when(ti == last // tt)
        def _():
            shift_ref[s] = xext[s, pl.ds(8 + last % tt, 1), :]

        @pl.when(ti == n_tiles - 1)
        def _():
            _store_state(so_ref, st_ref, s)


def _rwkv_call(cd, prev8, s0, lw, *, nb, tt, c, t_valid):
    b, lp, _ = cd.shape
    consts = [lw["rwkv_mu"], lw["rwkv_w0"], lw["rwkv_wb"], lw["rwkv_a0"], lw["rwkv_ab"], lw["rwkv_gb"],
              lw["rwkv_kk"], lw["rwkv_ka"], lw["rwkv_rk"], lw["rwkv_lnw"], lw["rwkv_lnb"]]
    body = functools.partial(_rwkv_kernel, nb=nb, n_tiles=lp // tt, tt=tt, c=c, t_valid=t_valid)
    per_b = lambda s: pl.BlockSpec((nb,) + s, lambda i, j: (i,) + (0,) * len(s))
    n_ch = nb * (tt // c)
    return pl.pallas_call(
        body, grid=(b // nb, lp // tt),
        in_specs=[pl.BlockSpec((nb, tt, D_W), lambda i, j: (i, j, 0)), per_b((8, D_W)),
                  per_b((HEADS, HD, HD))] + [_const_spec(x.shape) for x in consts],
        out_specs=[pl.BlockSpec((nb, tt, MIXW), lambda i, j: (i, j, 0)),
                   per_b((1, D_W)), per_b((HEADS, HD, HD))],
        out_shape=[jax.ShapeDtypeStruct((b, lp, MIXW), BF16),
                   jax.ShapeDtypeStruct((b, 1, D_W), F32),
                   jax.ShapeDtypeStruct((b, HEADS, HD, HD), F32)],
        scratch_shapes=[pltpu.VMEM((nb, tt + 8, D_W), F32)] + [pltpu.VMEM((nb * tt, MIXW), F32)] * 8
        + [pltpu.VMEM((nb * tt, MIXW), BF16)] * 2 + [pltpu.VMEM((2 * nb * tt, MIXW), BF16),
                                                     pltpu.VMEM((-(-n_ch // 8) * 8, MIXW), F32),
                                                     pltpu.VMEM((n_ch, MIXW, MIXW), BF16),
                                                     pltpu.VMEM((n_ch, MIXW, MIXW), BF16),
                                                     pltpu.VMEM((n_ch, MIXW, MIXW), F32)]
        + [pltpu.VMEM((nb * tt, MIXW), F32)] * 2 + [pltpu.VMEM((nb, MIXW, MIXW), F32)],
        compiler_params=pltpu.CompilerParams(dimension_semantics=("parallel", "arbitrary"),
                                             vmem_limit_bytes=VMEM_LIMIT),
        name="rwkv7")(cd, prev8, s0, *consts)


def _pad_cols(w, width):
    return jnp.pad(w, ((0, 0), (0, width - w.shape[1])))


def _mla_head_cols(w, n_heads, per_head, pieces):
    blocks = []
    for h in range(n_heads):
        cols = [sign * w[:, h * per_head + lo:h * per_head + hi] for lo, hi, sign in pieces]
        blocks.append(_pad_cols(jnp.concatenate(cols, axis=1), MLA_PAD))
    return jnp.concatenate(blocks, axis=1)


def _block_diag_heads(w):
    eye = jnp.eye(HEADS, dtype=w.dtype)[:, None, :, None]
    return (w[:, :, None, :] * eye).reshape(MIXW, MIXW)


def _prep_layer_weights(p, l):
    half = MLA_ROPE // 2
    row = lambda v: v.reshape(1, -1).astype(F32)
    lw = {}
    for tag in ("1", "2"):
        lw["ffn" + tag] = dict(g_ffn=row(p["norm_ffn" + tag][l]), layer=l, w1=p["ffn%s_w1_bf16" % tag],
                               w2=p["ffn%s_w2_bf16" % tag])
    w_in = p["w_in"][l]
    o = 0
    cq = w_in[:, o:o + 256]; o += 256
    ckv = w_in[:, o:o + 128]; o += 128
    kpe = w_in[:, o:o + MLA_ROPE]; o += MLA_ROPE
    xb_gb = w_in[:, o:o + 512]; o += 512
    qkvz = w_in[:, o:o + 1024]; o += 1024
    a_in = w_in[:, o:o + HEADS]; o += HEADS
    b_in = w_in[:, o:o + HEADS]; o += HEADS
    rw = w_in[:, o:o + 1024]
    kpe_rot = jnp.concatenate([-kpe[:, half:], kpe[:, :half]], axis=1)
    w_in_p = jnp.concatenate([
        cq, ckv, _pad_cols(kpe, 128), _pad_cols(kpe_rot, 128),
        xb_gb,
        qkvz, jnp.repeat(a_in, HD, axis=1), jnp.repeat(b_in, HD, axis=1),
        rw], axis=1)
    lw["ffn1"]["g_mix"] = row(p["norm_mix"][l])
    lw["ffn1"]["w_in"] = w_in_p.astype(BF16)
    w_out = p["w_out"][l]
    wo_a = jnp.concatenate([jnp.pad(w_out[h * HD:(h + 1) * HD], ((0, MLA_PAD - HD), (0, 0)))
                            for h in range(HEADS)], axis=0)
    lw["w_out"] = [wo_a.astype(BF16)] + [w_out[MIXW * i:MIXW * (i + 1)].astype(BF16) for i in (1, 2, 3)]
    wuq = p["mla_w_uq"][l]
    wq = _mla_head_cols(wuq, HEADS, MLA_QK, [(MLA_NOPE, MLA_QK, 1.0), (0, MLA_NOPE, 1.0)])
    wq_rot = _mla_head_cols(wuq, HEADS, MLA_QK, [(MLA_NOPE + half, MLA_QK, -1.0),
                                                (MLA_NOPE, MLA_NOPE + half, 1.0)])
    lw["mla_wq"] = jnp.concatenate([wq, wq_rot], axis=1).astype(BF16)
    wukv = p["mla_w_ukv"][l]
    zero_rope = jnp.zeros((wukv.shape[0], MLA_ROPE), wukv.dtype)
    wk = jnp.concatenate([_pad_cols(jnp.concatenate([zero_rope, wukv[:, h * 128:h * 128 + MLA_NOPE]], axis=1),
                                    MLA_PAD) for h in range(HEADS)], axis=1)
    wv = jnp.concatenate([_pad_cols(wukv[:, h * 128 + MLA_NOPE:(h + 1) * 128], MLA_PAD)
                          for h in range(HEADS)], axis=1)
    lw["mla_wkv"] = jnp.concatenate([wk, wv], axis=1).astype(BF16)
    lw["mla_wv"] = wv.astype(BF16)
    lw["mla_wkt"] = jnp.concatenate([wukv[:, h * 128:h * 128 + MLA_NOPE] for h in range(HEADS)],
                                    axis=1).T.astype(BF16)
    lw["mla_gqa"] = row(p["mla_q_a_norm"][l])
    lw["mla_gkva"] = row(p["mla_kv_a_norm"][l])
    perm = lambda g: row(_pad_cols(jnp.concatenate([g[MLA_NOPE:], g[:MLA_NOPE]])[None], MLA_PAD))
    lw["mla_gq"] = perm(p["mla_q_norm"][l])
    lw["mla_gk"] = perm(p["mla_k_norm"][l])
    lw["lru_cw"] = p["lru_conv_w"][l]
    lw["lru_cb"] = row(p["lru_conv_b"][l])
    lw["lru_wa"] = _block_diag_heads(p["lru_wa"][l]).astype(BF16)
    lw["lru_ba"] = row(p["lru_ba"][l])
    lw["lru_wx"] = _block_diag_heads(p["lru_wx"][l]).astype(BF16)
    lw["lru_bx"] = row(p["lru_bx"][l])
    lw["lru_lam"] = row(p["lru_lambda"][l])
    lw["gdn_cw"] = p["gdn_conv_w"][l]
    lw["gdn_alog"] = row(jnp.repeat(p["gdn_a_log"][l], HD))
    lw["gdn_dtb"] = row(jnp.repeat(p["gdn_dt_bias"][l], HD))
    lw["gdn_gon"] = row(jnp.tile(p["gdn_o_norm"][l], HEADS))
    z64 = jnp.zeros((64, MIXW), F32)
    lw["rwkv_mu"] = row(p["rwkv_mu"][l])
    lw["rwkv_w0"] = row(p["rwkv_w0"][l])
    lw["rwkv_wb"] = jnp.concatenate([p["rwkv_w_b"][l], z64], axis=0).astype(BF16)
    lw["rwkv_a0"] = row(p["rwkv_a0"][l])
    lw["rwkv_ab"] = jnp.concatenate([z64, p["rwkv_a_b"][l]], axis=0).astype(BF16)
    lw["rwkv_gb"] = p["rwkv_g_b"][l].astype(BF16)
    lw["rwkv_kk"] = row(p["rwkv_k_k"][l])
    lw["rwkv_ka"] = row(p["rwkv_k_a"][l])
    lw["rwkv_rk"] = row(p["rwkv_r_k"][l])
    lw["rwkv_lnw"] = row(p["rwkv_ln_w"][l])
    lw["rwkv_lnb"] = row(p["rwkv_ln_b"][l])
    return lw


def _rope_tables(pos):
    inv = ROPE_THETA ** (-jnp.arange(0, MLA_ROPE, 2, dtype=F32) / MLA_ROPE)
    ang = pos.astype(F32)[:, None] * inv[None, :]
    cos, sin = jnp.cos(ang), jnp.sin(ang)
    n = pos.shape[0]
    cos_t = jnp.concatenate([cos, cos, jnp.ones((n, MLA_PAD - MLA_ROPE), F32)], axis=1)
    sin_t = jnp.concatenate([sin, sin, jnp.zeros((n, MLA_PAD - MLA_ROPE), F32)], axis=1)
    return cos_t, sin_t


def _hist8(rows):
    return jnp.pad(rows.astype(F32), ((0, 0), (8 - rows.shape[1], 0), (0, 0)))


def _run_group(x, lws, final_g, states, cfg):
    b, lp, d = x.shape
    n = b * lp
    t_valid, tm, tt, c = cfg["t_valid"], cfg["tm"], cfg["tt"], cfg["c"]
    cos_t, sin_t = cfg["rope"]
    xf = x.reshape(n, d)
    new_states = []
    depth = len(lws)
    for l in range(depth):
        lw = lws[l]
        st = states[l]
        xf, q, k, v, ckv, kpe, cb, cc, cd = _token_call(xf, lw["ffn1"], tm=tm,
                                                        post=(cos_t, sin_t, lw, (b, lp, t_valid)))
        r3 = lambda a: a.reshape(b, lp, a.shape[-1])
        if st["mla"] is None:
            ya = _mla_prompt_call(r3(q), r3(k), r3(v), tq=cfg["tq"], tk=cfg["tk"])
        else:
            cache_ckv, cache_kpe_t = st["mla"]
            ya = _mla_sample_call(r3(q), r3(k), r3(v), cache_ckv, cache_kpe_t, l, lw)
        yb, lru_conv, lru_h = _lru_call(r3(cb), st["lru_conv"], st["lru_h"], lw, tt=tt, t_valid=t_valid)
        yc, gdn_conv, gdn_s = _gdn_call(r3(cc), st["gdn_conv"], st["gdn_s"], lw, nb=cfg["nb"], tt=tt, c=c,
                                        t_valid=t_valid)
        yd, shift, rwkv_s = _rwkv_call(r3(cd), st["rwkv_shift"], st["rwkv_s"], lw, nb=cfg["nb"], tt=tt, c=c,
                                       t_valid=t_valid)
        ys = [ya.reshape(n, -1), yb.reshape(n, -1), yc.reshape(n, -1), yd.reshape(n, -1)]
        xf = _token_call(xf, lw["ffn2"], pre=(ys, lw["w_out"]), tm=tm,
                         final_g=final_g if l == depth - 1 else None)[0]
        new_states.append((ckv.reshape(b, t_valid, -1), kpe.reshape(b, t_valid, -1), lru_conv, lru_h[:, 0],
                           gdn_conv, gdn_s, shift[:, 0], rwkv_s))
    stacked = [jnp.stack(t) for t in zip(*new_states)]
    return xf.reshape(b, lp, d), stacked


def _group_config(b, t_valid):
    divisor = lambda target: max(d for d in range(1, target + 1) if b % d == 0)
    if t_valid <= CHUNK:
        lp = t_valid
        return dict(lp=lp, t_valid=t_valid, tm=b * lp if b * lp <= 512 else lp, tt=lp, c=lp, tq=lp, tk=1024,
                    nb=divisor(8))
    lp = -(-t_valid // 384) * 384
    return dict(lp=lp, t_valid=t_valid, tm=384, tt=384, c=CHUNK, tq=384, tk=384, nb=divisor(2))


def kernel(x_prompt, x_sample, cache_mla_ckv, cache_mla_kpe, state_lru_conv, state_lru_h, state_gdn_conv, state_gdn_s, state_rwkv_shift, state_rwkv_s, meta_tokens, norm_ffn1, ffn1_w1, ffn1_w2, norm_mix, w_in, mla_q_a_norm, mla_w_uq, mla_kv_a_norm, mla_w_ukv, mla_q_norm, mla_k_norm, lru_conv_w, lru_conv_b, lru_wa, lru_ba, lru_wx, lru_bx, lru_lambda, gdn_conv_w, gdn_a_log, gdn_dt_bias, gdn_o_norm, rwkv_mu, rwkv_w0, rwkv_w_b, rwkv_a0, rwkv_a_b, rwkv_g_b, rwkv_k_k, rwkv_k_a, rwkv_r_k, rwkv_ln_w, rwkv_ln_b, w_out, norm_ffn2, ffn2_w1, ffn2_w2, final_norm):
    p = dict(norm_ffn1=norm_ffn1, ffn1_w1=ffn1_w1, ffn1_w2=ffn1_w2, norm_mix=norm_mix, w_in=w_in,
             mla_q_a_norm=mla_q_a_norm, mla_w_uq=mla_w_uq, mla_kv_a_norm=mla_kv_a_norm,
             mla_w_ukv=mla_w_ukv, mla_q_norm=mla_q_norm, mla_k_norm=mla_k_norm,
             lru_conv_w=lru_conv_w, lru_conv_b=lru_conv_b, lru_wa=lru_wa, lru_ba=lru_ba,
             lru_wx=lru_wx, lru_bx=lru_bx, lru_lambda=lru_lambda, gdn_conv_w=gdn_conv_w,
             gdn_a_log=gdn_a_log, gdn_dt_bias=gdn_dt_bias, gdn_o_norm=gdn_o_norm, rwkv_mu=rwkv_mu,
             rwkv_w0=rwkv_w0, rwkv_w_b=rwkv_w_b, rwkv_a0=rwkv_a0, rwkv_a_b=rwkv_a_b, rwkv_g_b=rwkv_g_b,
             rwkv_k_k=rwkv_k_k, rwkv_k_a=rwkv_k_a, rwkv_r_k=rwkv_r_k, rwkv_ln_w=rwkv_ln_w,
             rwkv_ln_b=rwkv_ln_b, w_out=w_out, norm_ffn2=norm_ffn2, ffn2_w1=ffn2_w1, ffn2_w2=ffn2_w2)
    depth = w_in.shape[0]
    for name in ("ffn1_w1", "ffn1_w2", "ffn2_w1", "ffn2_w2"):
        p[name + "_bf16"] = p[name].astype(BF16)
    lws = [_prep_layer_weights(p, l) for l in range(depth)]
    final_g = final_norm.reshape(1, -1).astype(F32)
    d = x_prompt.shape[-1]

    bp, seq, _ = x_prompt.shape
    tp = N_META + seq
    cfg = _group_config(bp, tp)
    lp = cfg["lp"]
    x0 = jnp.concatenate([jnp.broadcast_to(meta_tokens.astype(F32)[None], (bp, N_META, d)), x_prompt,
                          jnp.zeros((bp, lp - tp, d), F32)], axis=1)
    cfg["rope"] = tuple(jnp.tile(t, (bp, 1)) for t in _rope_tables(jnp.arange(lp)))
    zero = dict(mla=None, lru_conv=jnp.zeros((bp, 8, MIXW), F32), lru_h=jnp.zeros((bp, 1, MIXW), F32),
                gdn_conv=jnp.zeros((bp, 8, 3 * MIXW), F32), gdn_s=jnp.zeros((bp, HEADS, HD, HD), F32),
                rwkv_shift=jnp.zeros((bp, 8, D_W), F32), rwkv_s=jnp.zeros((bp, HEADS, HD, HD), F32))
    yp, p_new = _run_group(x0, lws, final_g, [zero] * depth, cfg)

    bs, ts, _ = x_sample.shape
    past = cache_mla_ckv.shape[2]
    cfg_s = _group_config(bs, ts)
    cfg_s["rope"] = tuple(jnp.tile(t, (bs, 1)) for t in _rope_tables(past + jnp.arange(ts)))
    cache_kpe_t = jnp.swapaxes(cache_mla_kpe, 2, 3)
    st_s = [dict(mla=(cache_mla_ckv, cache_kpe_t), lru_conv=_hist8(state_lru_conv[l]),
                 lru_h=state_lru_h[l][:, None].astype(F32), gdn_conv=_hist8(state_gdn_conv[l]),
                 gdn_s=state_gdn_s[l], rwkv_shift=_hist8(state_rwkv_shift[l][:, None]),
                 rwkv_s=state_rwkv_s[l]) for l in range(depth)]
    ys, s_new = _run_group(x_sample, lws, final_g, st_s, cfg_s)
    return (yp[:, N_META:tp], ys) + tuple(p_new) + tuple(s_new)
```

```python
import functools
import math

import jax
import jax.numpy as jnp
import numpy as np
from jax import lax
from jax.experimental import pallas as pl
from jax.experimental.pallas import tpu as pltpu

F32 = jnp.float32
BF16 = jnp.bfloat16

EPS = 1e-6
N_META = 16
CHUNK = 64
CONV_W = 4
HEADS = 4
HD = 64
MIXW = HEADS * HD
MLA_NOPE = 64
MLA_ROPE = 32
MLA_QK = MLA_NOPE + MLA_ROPE
MLA_V = 64
MLA_PAD = 128
ROPE_THETA = 10000.0
LRU_C = 8.0
RWKV_GN_EPS = 64e-5
RWKV_DECAY_SCALE = -0.606531

VMEM_LIMIT = 56 * 1024 * 1024
MXU_N = 256

A_W = 640
B_W = 512
C_W = 1536
D_W = 1024


def _bdot(a, b):
    return jnp.dot(a.astype(BF16), b.astype(BF16), preferred_element_type=F32)


def _bdot_tn(a, b):
    return lax.dot_general(a.astype(BF16), b.astype(BF16), (((0,), (0,)), ((), ())),
                           preferred_element_type=F32)


def _split3(x):
    x1 = x.astype(BF16)
    r1 = x - x1.astype(F32)
    x2 = r1.astype(BF16)
    r2 = r1 - x2.astype(F32)
    return x1, x2, r2.astype(BF16)


def _xdot_r01(x, m01):
    return jnp.dot(x.astype(BF16), m01, preferred_element_type=F32)


def _xdot_l01(m01, x):
    x1, x2, x3 = _split3(x)
    d = functools.partial(jnp.dot, preferred_element_type=F32)
    return d(m01, x1) + d(m01, x2) + d(m01, x3)


def _sigmoid(x):
    return 0.5 * jnp.tanh(0.5 * x) + 0.5


def _silu(x):
    return x * _sigmoid(x)


def _softplus(x):
    return jnp.maximum(x, 0.0) + jnp.log(1.0 + jnp.exp(-jnp.abs(x)))


def _gelu_tanh(x):
    return 0.5 * x * (1.0 + jnp.tanh(0.7978845608028654 * (x + 0.044715 * (x * x * x))))


def _rms_rows(x, g):
    return x * lax.rsqrt(jnp.mean(x * x, axis=-1, keepdims=True) + EPS) * g


def _block_diag_mask():
    r = lax.broadcasted_iota(jnp.int32, (MIXW, MIXW), 0) >> 6
    c = lax.broadcasted_iota(jnp.int32, (MIXW, MIXW), 1) >> 6
    return r == c


def _embed(x, bd01):
    c = x.shape[0]
    x = x.astype(BF16)
    if c < HD:
        x = jnp.concatenate([x, jnp.zeros((HD - c, MIXW), BF16)], axis=0)
    return jnp.concatenate([x] * HEADS, axis=0) * bd01


def _mm(a, b):
    return jnp.dot(a.astype(BF16), b, preferred_element_type=F32)


def _mm_nt(a, b):
    return lax.dot_general(a.astype(BF16), b, (((1,), (1,)), ((), ())), preferred_element_type=F32)


def _head_masks(c):
    row = lax.broadcasted_iota(jnp.int32, (c, MIXW), 0)
    col = lax.broadcasted_iota(jnp.int32, (c, MIXW), 1) & (HD - 1)
    return row >= col, row > col, row == col


def _tril01(c):
    r = lax.broadcasted_iota(jnp.int32, (c, c), 0)
    k = lax.broadcasted_iota(jnp.int32, (c, c), 1)
    return jnp.where(r >= k, 1.0, 0.0).astype(BF16)


def _inv_unit_lower(lms, eye4, bd01):
    c = lms[0].shape[0]
    levels = int(math.log2(c))
    eye = jnp.where(eye4, 1.0, 0.0)
    n = [-x for x in lms]
    t = [eye + x for x in n]
    if levels < 2:
        return t
    p = [_mm(x, _embed(x, bd01)) for x in n]
    for _ in range(levels - 2):
        r = [_mm(jnp.concatenate([pi, ti], axis=0), _embed(pi, bd01)) for pi, ti in zip(p, t)]
        p = [x[0:c] for x in r]
        t = [ti + x[c:2 * c] for ti, x in zip(t, r)]
    return [ti + _mm(ti, _embed(pi, bd01)) for pi, ti in zip(p, t)]


def _load_state(st_ref, s0_ref, s):
    st_ref[s] = jnp.zeros((MIXW, MIXW), F32)
    for h in range(HEADS):
        st_ref[s, h * HD:(h + 1) * HD, h * HD:(h + 1) * HD] = s0_ref[s, h].astype(F32)


def _store_state(so_ref, st_ref, s):
    for h in range(HEADS):
        so_ref[s, h] = st_ref[s, h * HD:(h + 1) * HD, h * HD:(h + 1) * HD]


def _shift_rows(x, s, fill):
    rolled = pltpu.roll(x, s, 0)
    row = lax.broadcasted_iota(jnp.int32, x.shape, 0)
    return jnp.where(row >= s, rolled, fill)


def _causal_conv(xext_ref, w_ref, tt):
    acc = None
    for j in range(CONV_W):
        term = xext_ref[pl.ds(8 - (CONV_W - 1) + j, tt), :] * w_ref[j:j + 1, :]
        acc = term if acc is None else acc + term
    return acc


def _const_spec(shape):
    nd = len(shape)
    return pl.BlockSpec(shape, lambda *_: (0,) * nd, pipeline_mode=pl.Buffered(1))


def _token_kernel(*refs, has_pre, has_post, has_final, d_ff, ff_chunk):
    it = iter(refs)
    x_ref = next(it)
    if has_pre:
        y_refs = [next(it) for _ in range(4)]
        wo_refs = [next(it) for _ in range(4)]
    gffn_ref, w1_ref, w2_ref = next(it), next(it), next(it)
    if has_post:
        gmix_ref, win_ref = next(it), next(it)
        prep_in = [next(it) for _ in range(8)]
    if has_final:
        gfin_ref = next(it)
    xo_ref = next(it)
    if has_post:
        prep_out = [next(it) for _ in range(5)]
        col_refs = [next(it) for _ in range(3)]

    x = x_ref[...]
    if has_pre:
        for y_ref, wo_ref in zip(y_refs, wo_refs):
            x = x + jnp.dot(y_ref[...], wo_ref[...], preferred_element_type=F32)
    h = _rms_rows(x, gffn_ref[...]).astype(BF16)
    acc = None
    for lo, hi in ((0, ff_chunk), (ff_chunk, d_ff)):
        gate = jnp.dot(h, w1_ref[:, lo:hi], preferred_element_type=F32)
        up = jnp.dot(h, w1_ref[:, d_ff + lo:d_ff + hi], preferred_element_type=F32)
        act = (_silu(gate) * up).astype(BF16)
        part = jnp.dot(act, w2_ref[lo:hi, :], preferred_element_type=F32)
        acc = part if acc is None else acc + part
    x = x + 0.5 * acc
    if has_final:
        xo_ref[...] = _rms_rows(x, gfin_ref[...])
    else:
        xo_ref[...] = x
    if has_post:
        hm = _rms_rows(x, gmix_ref[...]).astype(BF16)
        _mla_prep(jnp.dot(hm, win_ref[:, 0:A_W], preferred_element_type=F32), *prep_in, *prep_out)
        off = A_W
        for c_ref in col_refs:
            wd = c_ref.shape[-1]
            c_ref[...] = jnp.dot(hm, win_ref[:, off:off + wd], preferred_element_type=F32)
            off += wd


def _token_call(x, lw, *, pre=None, post=None, final_g=None, tm):
    n, d = x.shape
    d_ff = lw["w2"].shape[1]
    ff_chunk = -(-(d_ff // MXU_N) // 2) * MXU_N
    row = lambda w: pl.BlockSpec((tm, w), lambda i: (i, 0))
    args, specs = [x], [row(d)]
    if pre is not None:
        ys, wos = pre
        for y in ys:
            args.append(y)
            specs.append(row(y.shape[1]))
        for w in wos:
            args.append(w)
            specs.append(_const_spec(w.shape))
    args.append(lw["g_ffn"])
    specs.append(_const_spec(lw["g_ffn"].shape))
    layer = lw["layer"]
    for name in ("w1", "w2"):
        args.append(lw[name])
        specs.append(pl.BlockSpec((None,) + lw[name].shape[1:], lambda i: (layer, 0, 0),
                                  pipeline_mode=pl.Buffered(1)))
    if post is not None:
        cos_t, sin_t, mw, (b, lp, t_valid) = post
        for name in ("g_mix", "w_in"):
            args.append(lw[name])
            specs.append(_const_spec(lw[name].shape))
        args += [cos_t, sin_t]
        specs += [row(MLA_PAD), row(MLA_PAD)]
        for name in ("mla_gqa", "mla_wq", "mla_gkva", "mla_wkv", "mla_gq", "mla_gk"):
            args.append(mw[name])
            specs.append(_const_spec(mw[name].shape))
    if final_g is not None:
        args.append(final_g)
        specs.append(_const_spec(final_g.shape))
    out_shape = [jax.ShapeDtypeStruct((n, d), F32)]
    out_specs = [row(d)]
    if post is not None:
        hw = HEADS * MLA_PAD
        out_shape += [jax.ShapeDtypeStruct((n, hw), BF16)] * 3
        out_specs += [row(hw)] * 3
        for wd in (128, MLA_ROPE):
            if lp > t_valid:
                tps = lp // tm
                out_shape.append(jax.ShapeDtypeStruct((b, t_valid, wd), F32))
                out_specs.append(pl.BlockSpec((None, tm, wd), lambda i: (i // tps, i % tps, 0)))
            else:
                out_shape.append(jax.ShapeDtypeStruct((n, wd), F32))
                out_specs.append(row(wd))
        for wd in (B_W, C_W, D_W):
            out_shape.append(jax.ShapeDtypeStruct((n, wd), F32))
            out_specs.append(row(wd))
    body = functools.partial(_token_kernel, has_pre=pre is not None, has_post=post is not None,
                             has_final=final_g is not None, d_ff=d_ff, ff_chunk=ff_chunk)
    return pl.pallas_call(
        body, grid=(n // tm,), in_specs=specs, out_specs=out_specs, out_shape=out_shape,
        compiler_params=pltpu.CompilerParams(dimension_semantics=("parallel",),
                                             vmem_limit_bytes=VMEM_LIMIT),
        name="token_block")(*args)


def _mla_prep(ca, cos_ref, sin_ref, gqa_ref, wq_ref, gkva_ref, wkv_ref, gq_ref, gk_ref,
              q_ref, k_ref, v_ref, ckv_ref, kpe_ref):
    cs = cos_ref[...]
    sn = sin_ref[...]
    hw = HEADS * MLA_PAD
    cqn = _rms_rows(ca[:, 0:256], gqa_ref[...])
    qq = _bdot(cqn, wq_ref[...])
    scale = math.log2(math.e) / math.sqrt(MLA_QK)
    for h in range(HEADS):
        lo = h * MLA_PAD
        blk = qq[:, lo:lo + MLA_PAD] * cs + qq[:, hw + lo:hw + lo + MLA_PAD] * sn
        ms = jnp.sum(blk * blk, axis=-1, keepdims=True) * (1.0 / MLA_QK)
        q_ref[:, lo:lo + MLA_PAD] = (blk * lax.rsqrt(ms + EPS) * (gq_ref[...] * scale)).astype(BF16)
    ckv = _rms_rows(ca[:, 256:384], gkva_ref[...])
    ckv_ref[...] = ckv
    kpe = ca[:, 384:512] * cs + ca[:, 512:640] * sn
    kpe_ref[...] = kpe[:, 0:MLA_ROPE]
    kv = _bdot(ckv, wkv_ref[...])
    ones_lane = lax.broadcasted_iota(jnp.int32, (kv.shape[0], MLA_PAD), 1) == MLA_V
    for h in range(HEADS):
        lo = h * MLA_PAD
        kb = kv[:, lo:lo + MLA_PAD] + kpe
        ms = jnp.sum(kb * kb, axis=-1, keepdims=True) * (1.0 / MLA_QK)
        k_ref[:, lo:lo + MLA_PAD] = (kb * lax.rsqrt(ms + EPS) * gk_ref[...]).astype(BF16)
        v_ref[:, lo:lo + MLA_PAD] = jnp.where(ones_lane, 1.0, kv[:, hw + lo:hw + lo + MLA_PAD]).astype(BF16)


def _softmax_step(carry, s, vblk):
    m, acc = carry
    m_new = jnp.maximum(m, jnp.max(s, axis=-1, keepdims=True))
    p = jnp.exp2(s - m_new)
    acc = jnp.exp2(m - m_new) * acc + jnp.dot(p.astype(BF16), vblk, preferred_element_type=F32)
    return m_new, acc


def _softmax_init(t):
    return jnp.full((t, 1), -1e30, F32), jnp.zeros((t, MLA_PAD), F32)


def _softmax_finish(acc):
    return (acc / acc[:, MLA_V:MLA_V + 1]).astype(BF16)


def _mla_prompt_kernel(q_ref, k_ref, v_ref, o_ref, *, tq, tk, n_kb):
    q0 = pl.program_id(1) * tq
    first_chunk = (q0 - N_META) >> 6
    last_chunk = (q0 + tq - 1 - N_META) >> 6
    n_full = (N_META + CHUNK * (first_chunk + 1)) // tk
    nkb = jnp.minimum((N_META + CHUNK * (last_chunk + 1)) // tk, n_kb)
    qchunk = (q0 + lax.broadcasted_iota(jnp.int32, (tq, 1), 0) - N_META) >> 6
    hs = range(HEADS)
    qs = [q_ref[:, h * MLA_PAD:(h + 1) * MLA_PAD] for h in hs]

    def body(it, state, masked, nblk):
        k0s = [pl.multiple_of((it * nblk + j) * tk, tk) for j in range(nblk)]
        blk = [slice(h * MLA_PAD, (h + 1) * MLA_PAD) for h in hs]
        s = [[lax.dot_general(qs[h], k_ref[pl.ds(k0, tk), blk[h]], (((1,), (1,)), ((), ())),
                              preferred_element_type=F32) for k0 in k0s] for h in hs]
        if masked:
            vis = [((k0 + lax.broadcasted_iota(jnp.int32, (1, tk), 1) - N_META) >> 6) <= qchunk for k0 in k0s]
            s = [[jnp.where(vis[j], s[h][j], -1e30) for j in range(nblk)] for h in hs]
        m_new = []
        for h in hs:
            m = state[h][0]
            for x in s[h]:
                m = jnp.maximum(m, jnp.max(x, axis=-1, keepdims=True))
            m_new.append(m)
        p = [[jnp.exp2(x - m_new[h]).astype(BF16) for x in s[h]] for h in hs]
        pv = []
        for h in hs:
            acc = None
            for j, k0 in enumerate(k0s):
                part = jnp.dot(p[h][j], v_ref[pl.ds(k0, tk), blk[h]], preferred_element_type=F32)
                acc = part if acc is None else acc + part
            pv.append(acc)
        return tuple((m_new[h], jnp.exp2(state[h][0] - m_new[h]) * state[h][1] + pv[h]) for h in hs)

    state = tuple(_softmax_init(tq) for _ in hs)
    n_pair = n_full // 2
    state = lax.fori_loop(0, n_pair, functools.partial(body, masked=False, nblk=2), state)
    state = lax.fori_loop(2 * n_pair, nkb, functools.partial(body, masked=True, nblk=1), state)
    lp = n_kb * tk
    k_t = pl.multiple_of(jnp.minimum(nkb * tk, lp - N_META), N_META)
    tail_chunk = (k_t + lax.broadcasted_iota(jnp.int32, (1, N_META), 1) - N_META) >> 6
    visible = (tail_chunk <= qchunk) & (nkb < n_kb)
    for h in hs:
        blk = slice(h * MLA_PAD, (h + 1) * MLA_PAD)
        s = lax.dot_general(qs[h], k_ref[pl.ds(k_t, N_META), blk], (((1,), (1,)), ((), ())),
                            preferred_element_type=F32)
        _, acc = _softmax_step(state[h], jnp.where(visible, s, -1e30), v_ref[pl.ds(k_t, N_META), blk])
        o_ref[:, blk] = _softmax_finish(acc)


def _mla_prompt_call(q, k, v, *, tq, tk):
    b, lp, hw = q.shape
    assert tq == tk and tq % CHUNK == 0 and lp % tk == 0
    body = functools.partial(_mla_prompt_kernel, tq=tq, tk=tk, n_kb=lp // tk)
    return pl.pallas_call(
        body, grid=(b, lp // tq),
        in_specs=[pl.BlockSpec((None, tq, hw), lambda i, j: (i, j, 0)),
                  pl.BlockSpec((None, lp, hw), lambda i, j: (i, 0, 0)),
                  pl.BlockSpec((None, lp, hw), lambda i, j: (i, 0, 0))],
        out_specs=pl.BlockSpec((None, tq, hw), lambda i, j: (i, j, 0)),
        out_shape=jax.ShapeDtypeStruct((b, lp, hw), BF16),
        compiler_params=pltpu.CompilerParams(dimension_semantics=("parallel", "arbitrary"),
                                             vmem_limit_bytes=VMEM_LIMIT),
        name="mla_prompt_attn")(q, k, v)


def _mla_sample_kernel(q_ref, kn_ref, vn_ref, cckv_ref, ckpet_ref, wkt_ref, wv_ref, gk_ref, o_ref):
    t = q_ref.shape[0]
    hs = range(HEADS)
    blk = [slice(h * MLA_PAD, (h + 1) * MLA_PAD) for h in hs]
    ckv = cckv_ref[...].astype(BF16)
    knt = lax.dot_general(wkt_ref[...], ckv, (((1,), (1,)), ((), ())), preferred_element_type=F32)
    kpet = ckpet_ref[...]
    ssq_pe = jnp.sum(kpet * kpet, axis=0, keepdims=True)
    kpet16 = kpet.astype(BF16)
    zpad = jnp.zeros((MLA_PAD - MLA_QK, kpet.shape[1]), BF16)
    kts = [knt[h * MLA_NOPE:(h + 1) * MLA_NOPE, :] for h in hs]
    rk = [lax.rsqrt((jnp.sum(x * x, axis=0, keepdims=True) + ssq_pe) * (1.0 / MLA_QK) + EPS) for x in kts]
    qg = [(q_ref[:, blk[h]].astype(F32) * gk_ref[...]).astype(BF16) for h in hs]
    s_c = [jnp.dot(qg[h], jnp.concatenate([kpet16, kts[h].astype(BF16), zpad], axis=0),
                   preferred_element_type=F32) * rk[h] for h in hs]
    s_n = [lax.dot_general(q_ref[:, blk[h]], kn_ref[:, blk[h]], (((1,), (1,)), ((), ())),
                           preferred_element_type=F32) for h in hs]
    m = [jnp.maximum(jnp.max(s_c[h], axis=-1, keepdims=True), jnp.max(s_n[h], axis=-1, keepdims=True))
         for h in hs]
    p_c = [jnp.exp2(s_c[h] - m[h]) for h in hs]
    p_n = [jnp.exp2(s_n[h] - m[h]) for h in hs]
    den = [jnp.sum(p_c[h], axis=-1, keepdims=True) + jnp.sum(p_n[h], axis=-1, keepdims=True) for h in hs]
    lat = jnp.dot(jnp.concatenate([x.astype(BF16) for x in p_c], axis=0), ckv, preferred_element_type=F32)
    for h in hs:
        o = jnp.dot(lat[h * t:(h + 1) * t].astype(BF16), wv_ref[:, blk[h]], preferred_element_type=F32)
        o = o + jnp.dot(p_n[h].astype(BF16), vn_ref[:, blk[h]], preferred_element_type=F32)
        o_ref[:, blk[h]] = (o / den[h]).astype(BF16)


def _mla_sample_call(q, kn, vn, cache_ckv, cache_kpe_t, layer, lw):
    b, t, hw = q.shape
    past = cache_ckv.shape[2]
    seq = lambda: pl.BlockSpec((None, t, hw), lambda i: (i, 0, 0))
    consts = [lw["mla_wkt"], lw["mla_wv"], lw["mla_gk"]]
    return pl.pallas_call(
        _mla_sample_kernel, grid=(b,),
        in_specs=[seq(), seq(), seq(),
                  pl.BlockSpec((None, None, past, cache_ckv.shape[3]), lambda i: (layer, i, 0, 0)),
                  pl.BlockSpec((None, None, cache_kpe_t.shape[2], past), lambda i: (layer, i, 0, 0))]
        + [_const_spec(c.shape) for c in consts],
        out_specs=seq(),
        out_shape=jax.ShapeDtypeStruct((b, t, hw), BF16),
        compiler_params=pltpu.CompilerParams(dimension_semantics=("arbitrary",),
                                             vmem_limit_bytes=VMEM_LIMIT),
        name="mla_sample_attn")(q, kn, vn, cache_ckv, cache_kpe_t, *consts)


def _lru_kernel(cb_ref, buf_ref, h0_ref, cw_ref, cbias_ref, wa_ref, ba_ref, wx_ref, bx_ref, lam_ref,
                y_ref, nbuf_ref, hl_ref, xext, hcar, sa, sb, sh, *, tt, t_valid):
    ti = pl.program_id(1)

    @pl.when(ti == 0)
    def _():
        xext[0:8, :] = buf_ref[...]
        hcar[...] = h0_ref[...]

    @pl.when(ti > 0)
    def _():
        xext[0:8, :] = xext[tt:tt + 8, :]

    xext[8:tt + 8, :] = cb_ref[:, 0:MIXW]
    xc = _causal_conv(xext, cw_ref, tt) + cbias_ref[...]
    r = _sigmoid(_bdot(xc, wa_ref[...]) + ba_ref[...])
    i = _sigmoid(_bdot(xc, wx_ref[...]) + bx_ref[...])
    log_a = -LRU_C * r * _softplus(-lam_ref[...])
    pos = ti * tt + lax.broadcasted_iota(jnp.int32, (tt, 1), 0)
    log_a = jnp.where(pos < t_valid, log_a, 0.0)
    a = jnp.exp(log_a)
    b2 = -jnp.tanh(log_a) * (a * a + 1.0)
    pos2 = b2 > 0.0
    b = jnp.where(pos2, b2 * lax.rsqrt(jnp.where(pos2, b2, 1.0)), 0.0) * (i * xc)
    g = tt // 8
    a = a.reshape(g, 8, MIXW)
    b = b.reshape(g, 8, MIXW)
    sub = lax.broadcasted_iota(jnp.int32, (g, 8, 1), 1)
    for s in (1, 2, 4):
        keep = sub >= s
        b = a * jnp.where(keep, pltpu.roll(b, s, 1), 0.0) + b
        a = a * jnp.where(keep, pltpu.roll(a, s, 1), 1.0)
    a = a.reshape(tt, MIXW)
    b = b.reshape(tt, MIXW)
    halves = [slice(0, 128), slice(128, MIXW)]
    for k, lanes in enumerate(halves):
        sa[k] = a[:, lanes]
        sb[k] = b[:, lanes]
    ag =jnp.concatenate([sa[k, pl.ds(7, g, stride=8), :] for k in range(2)], axis=1)
    bg = jnp.concatenate([sb[k, pl.ds(7, g, stride=8), :] for k in range(2)], axis=1)
    if g % 8 == 0:
        s = 1
        while s < g:
            bg = ag * _shift_rows(bg, s, 0.0) + bg
            ag = ag * _shift_rows(ag, s, 1.0)
            s *= 2
        h_end = ag * hcar[...] + bg
        h_in = _shift_rows(h_end, 1, hcar[...])
        hcar[...] = h_end[g - 1:g, :]
    else:
        states = [hcar[...]]
        for j in range(g):
            states.append(ag[j:j + 1, :] * states[-1] + bg[j:j + 1, :])
        h_in = jnp.concatenate(states[:g], axis=0)
        hcar[...] = states[g]
    for r in range(8):
        rows = pl.ds(r, g, stride=8)
        for k, lanes in enumerate(halves):
            sh[k, rows, :] = sa[k, rows, :] * h_in[:, lanes] + sb[k, rows, :]
    h = jnp.concatenate([sh[0], sh[1]], axis=1)
    y_ref[...] = (h * _gelu_tanh(cb_ref[:, MIXW:2 * MIXW])).astype(BF16)

    last = t_valid - 1

    @pl.when(ti == last // tt)
    def _():
        r0 = last % tt
        hl_ref[...] = h[r0:r0 + 1, :]
        nbuf_ref[...] = xext[pl.ds(8 + r0 - (CONV_W - 2), CONV_W - 1), :]


def _lru_call(cb, buf8, h0, lw, *, tt, t_valid):
    b, lp, _ = cb.shape
    consts = [lw["lru_cw"], lw["lru_cb"], lw["lru_wa"], lw["lru_ba"], lw["lru_wx"], lw["lru_bx"],
              lw["lru_lam"]]
    body = functools.partial(_lru_kernel, tt=tt, t_valid=t_valid)
    per_b = lambda s: pl.BlockSpec((None,) + s, lambda i, j: (i, 0, 0))
    return pl.pallas_call(
        body, grid=(b, lp // tt),
        in_specs=[pl.BlockSpec((None, tt, B_W), lambda i, j: (i, j, 0)), per_b((8, MIXW)),
                  per_b((1, MIXW))] + [_const_spec(c.shape) for c in consts],
        out_specs=[pl.BlockSpec((None, tt, MIXW), lambda i, j: (i, j, 0)),
                   per_b((CONV_W - 1, MIXW)), per_b((1, MIXW))],
        out_shape=[jax.ShapeDtypeStruct((b, lp, MIXW), BF16),
                   jax.ShapeDtypeStruct((b, CONV_W - 1, MIXW), F32),
                   jax.ShapeDtypeStruct((b, 1, MIXW), F32)],
        scratch_shapes=[pltpu.VMEM((tt + 8, MIXW), F32), pltpu.VMEM((1, MIXW), F32)]
        + [pltpu.VMEM((2, tt, 128), F32)] * 3,
        compiler_params=pltpu.CompilerParams(dimension_semantics=("parallel", "arbitrary"),
                                             vmem_limit_bytes=VMEM_LIMIT),
        name="rglru")(cb, buf8, h0, *consts)


def _gdn_kernel(cc_ref, buf_ref, s0_ref, cw_ref, alog_ref, dtb_ref, gon_ref,
                y_ref, nbuf_ref, so_ref, xext, q_s, k_s, v_s, b_s, g_s, o_s, u_s, a_s, w_s, e_s, l_s,
                m_s, h_s, n_s, st_ref, *, nb, n_tiles, tt, c, t_valid):
    ti = pl.program_id(1)
    qkw = 3 * MIXW
    bd = _block_diag_mask()
    bd01 = jnp.where(bd, 1.0, 0.0).astype(BF16)
    pos = ti * tt + lax.broadcasted_iota(jnp.int32, (tt, 1), 0)
    valid = pos < t_valid

    for s in range(nb):
        xe = xext.at[s]
        sr = slice(s * tt, (s + 1) * tt)

        @pl.when(ti == 0)
        def _():
            xe[0:8, :] = buf_ref[s]
            _load_state(st_ref, s0_ref, s)

        @pl.when(ti > 0)
        def _():
            xe[0:8, :] = xe[tt:tt + 8, :]

        xe[8:tt + 8, :] = cc_ref[s, :, 0:qkw]
        xs = _silu(_causal_conv(xe, cw_ref, tt))
        q = xs[:, 0:MIXW]
        k = xs[:, MIXW:2 * MIXW]
        q_s[sr, :] = q * lax.rsqrt(_xdot_r01(q * q, bd01) + EPS) * (HD ** -0.5)
        k_s[sr, :] = k * lax.rsqrt(_xdot_r01(k * k, bd01) + EPS)
        v_s[sr, :] = xs[:, 2 * MIXW:3 * MIXW]
        a_in = cc_ref[s, :, qkw + MIXW:qkw + 2 * MIXW]
        b_in = cc_ref[s, :, qkw + 2 * MIXW:qkw + 3 * MIXW]
        b_s[sr, :] = jnp.where(valid, _sigmoid(b_in), 0.0)
        g_s[sr, :] = jnp.where(valid, -jnp.exp(alog_ref[...]) * _softplus(a_in + dtb_ref[...]), 0.0)

    tril4, strict4, eye4 = _head_masks(c)
    tril01 = _tril01(c)
    n_chunks = tt // c

    ch = range(nb * n_chunks)
    rs = [slice(ci * c, (ci + 1) * c) for ci in ch]
    gc = [_xdot_l01(tril01, g_s[r, :]) for r in rs]
    gct = [jnp.sum(jnp.where(eye4, x, 0.0), axis=0, keepdims=True) for x in gc]
    decay = [jnp.where(tril4, jnp.exp(jnp.where(tril4, gc[i] - gct[i], 0.0)), 0.0) for i in ch]
    kc = [k_s[r, :] for r in rs]
    beta = [b_s[r, :] for r in rs]
    kb = [kc[i] * beta[i] for i in ch]
    sc = [_mm_nt(jnp.concatenate([kb[i], q_s[rs[i], :]], axis=0), _embed(kc[i], bd01)) for i in ch]
    tinv = _inv_unit_lower([jnp.where(strict4, sc[i][0:c] * decay[i], 0.0) for i in ch], eye4, bd01)
    attn = [jnp.where(tril4, sc[i][c:2 * c] * decay[i], 0.0).astype(BF16) for i in ch]
    egc = [jnp.exp(x) for x in gc]
    uw = [_mm(tinv[i], jnp.concatenate([_embed(v_s[rs[i], :] * beta[i], bd01),
                                        _embed(kb[i] * egc[i], bd01)], axis=1)) for i in ch]
    glast = [x[c - 1:c, :] for x in gc]
    kd = [(kc[i] * jnp.exp(glast[i] - gc[i])).astype(BF16) for i in ch]
    an = [lax.dot_general(kd[i], jnp.concatenate([uw[i][:, MIXW:], uw[i][:, :MIXW]], axis=1).astype(BF16),
                          (((0,), (0,)), ((), ())), preferred_element_type=F32) for i in ch]
    for i in ch:
        u_s[rs[i], :] = uw[i][:, 0:MIXW]
        w_s[rs[i], :] = uw[i][:, MIXW:2 * MIXW].astype(BF16)
        e_s[rs[i], :] = (q_s[rs[i], :] * egc[i]).astype(BF16)
        a_s[rs[i], :] = attn[i]
        m_s[i] = jnp.where(bd, an[i][:, 0:MIXW], 0.0).astype(BF16)
        n_s[i] = jnp.where(bd, an[i][:, MIXW:2 * MIXW], 0.0)
        l_s[i:i + 1, :] = jnp.exp(glast[i])

    for ci in range(n_chunks):
        for s in range(nb):
            i = s * n_chunks + ci
            s_bd = st_ref[s]
            s16 = s_bd.astype(BF16)
            h_s[i] = s16
            st_ref[s] = s_bd * l_s[i:i + 1, :] + n_s[i] - jnp.dot(m_s[i], s16, preferred_element_type=F32)

    r = [jnp.dot(jnp.concatenate([w_s[rs[i], :], e_s[rs[i], :]], axis=0), h_s[i], preferred_element_type=F32)
         for i in ch]
    v_new = [u_s[rs[i], :] - r[i][0:c] for i in ch]
    ov = [jnp.dot(a_s[rs[i], :], _embed(v_new[i], bd01), preferred_element_type=F32) for i in ch]
    for i in ch:
        o_s[rs[i], :] = r[i][c:2 * c] + ov[i]

    last = t_valid - 1
    for s in range(nb):
        o = o_s[s * tt:(s + 1) * tt, :]
        ms = _xdot_r01(o * o, bd01) * (1.0 / HD)
        z = cc_ref[s, :, qkw:qkw + MIXW]
        y_ref[s] = (o * lax.rsqrt(ms + EPS) * gon_ref[...] * _silu(z)).astype(BF16)

        @pl.when(ti == last // tt)
        def _():
            nbuf_ref[s] = xext[s, pl.ds(8 + last % tt - (CONV_W - 2), CONV_W - 1), :]

        @pl.when(ti == n_tiles - 1)
        def _():
            _store_state(so_ref, st_ref, s)


def _gdn_call(cc, buf8, s0, lw, *, nb, tt, c, t_valid):
    b, lp, _ = cc.shape
    qkw = 3 * MIXW
    consts = [lw["gdn_cw"], lw["gdn_alog"], lw["gdn_dtb"], lw["gdn_gon"]]
    body = functools.partial(_gdn_kernel, nb=nb, n_tiles=lp // tt, tt=tt, c=c, t_valid=t_valid)
    per_b = lambda s: pl.BlockSpec((nb,) + s, lambda i, j: (i,) + (0,) * len(s))
    n_ch = nb * (tt // c)
    return pl.pallas_call(
        body, grid=(b // nb, lp // tt),
        in_specs=[pl.BlockSpec((nb, tt, C_W), lambda i, j: (i, j, 0)), per_b((8, qkw)),
                  per_b((HEADS, HD, HD))] + [_const_spec(x.shape) for x in consts],
        out_specs=[pl.BlockSpec((nb, tt, MIXW), lambda i, j: (i, j, 0)),
                   per_b((CONV_W - 1, qkw)), per_b((HEADS, HD, HD))],
        out_shape=[jax.ShapeDtypeStruct((b, lp, MIXW), BF16),
                   jax.ShapeDtypeStruct((b, CONV_W - 1, qkw), F32),
                   jax.ShapeDtypeStruct((b, HEADS, HD, HD), F32)],
        scratch_shapes=[pltpu.VMEM((nb, tt + 8, qkw), F32)] + [pltpu.VMEM((nb * tt, MIXW), F32)] * 7
        + [pltpu.VMEM((nb * tt, MIXW), BF16)] * 3 + [pltpu.VMEM((-(-n_ch // 8) * 8, MIXW), F32),
                                                     pltpu.VMEM((n_ch, MIXW, MIXW), BF16),
                                                     pltpu.VMEM((n_ch, MIXW, MIXW), BF16),
                                                     pltpu.VMEM((n_ch, MIXW, MIXW), F32),
                                                     pltpu.VMEM((nb, MIXW, MIXW), F32)],
        compiler_params=pltpu.CompilerParams(dimension_semantics=("parallel", "arbitrary"),
                                             vmem_limit_bytes=VMEM_LIMIT),
        name="gated_delta")(cc, buf8, s0, *consts)


def _rwkv_kernel(cd_ref, prev_ref, s0_ref, mu_ref, w0_ref, wb_ref, a0_ref, ab_ref, gb_ref, kk_ref,
                 ka_ref, rk_ref, lnw_ref, lnb_ref,
                 y_ref, shift_ref, so_ref, xext, r_s, k_s, v_s, n_s, b_s, w_s, o_s, xv_s,
                 t_s, a_s, lhs_s, p_s, g_s, h_s, d_s, rk_s, gt_s, st_ref, *, nb, n_tiles, tt, c, t_valid):
    ti = pl.program_id(1)
    pos = ti * tt + lax.broadcasted_iota(jnp.int32, (tt, 1), 0)
    valid = pos < t_valid
    bd = _block_diag_mask()
    bd01 = jnp.where(bd, 1.0, 0.0).astype(BF16)

    for s in range(nb):
        xe = xext.at[s]
        sr = slice(s * tt, (s + 1) * tt)

        @pl.when(ti == 0)
        def _():
            xe[0:8, :] = prev_ref[s]
            _load_state(st_ref, s0_ref, s)

        @pl.when(ti > 0)
        def _():
            xe[0:8, :] = xe[tt:tt + 8, :]

        x = cd_ref[s]
        xe[8:tt + 8, :] = x
        xm = x + (xe[pl.ds(7, tt), :] - x) * mu_ref[...]
        r = xm[:, 0:MIXW]
        k = xm[:, MIXW:2 * MIXW]
        lo_blk = xm[:, 3 * MIXW:3 * MIXW + 128]
        logw = RWKV_DECAY_SCALE * _sigmoid(w0_ref[...] + _bdot(jnp.tanh(lo_blk), wb_ref[...]))
        a = _sigmoid(a0_ref[...] + _bdot(lo_blk, ab_ref[...]))
        kkv = k * kk_ref[...]
        kkn = kkv * lax.rsqrt(_xdot_r01(kkv * kkv, bd01) + EPS)
        kkn = jnp.where(valid, kkn, 0.0)
        kmod = k * (1.0 + (a - 1.0) * ka_ref[...])
        r_s[sr, :] = r
        k_s[sr, :] = jnp.where(valid, kmod, 0.0)
        v_s[sr, :] = xm[:, 2 * MIXW:3 * MIXW]
        n_s[sr, :] = kkn
        b_s[sr, :] = kkn * a
        w_s[sr, :] = jnp.where(valid, logw, 0.0)
        rk_s[sr, :] = _xdot_r01(r * kmod * rk_ref[...], bd01)
        gt_s[sr, :] = _bdot(_sigmoid(xm[:, 3 * MIXW + 128:3 * MIXW + 256]), gb_ref[...])

    tril4, strict4, eye4 = _head_masks(c)
    tril01 = _tril01(c)

    n_chunks = tt // c

    ch = range(nb * n_chunks)
    rs = [slice(ci * c, (ci + 1) * c) for ci in ch]
    lw = [w_s[r, :] for r in rs]
    cum = [_xdot_l01(tril01, x) for x in lw]
    ecum = [jnp.exp(x) for x in cum]
    pin = [jnp.exp(-x) for x in cum]
    pc = [x[c - 1:c, :] for x in ecum]
    kinv = [k_s[rs[i], :] * pin[i] for i in ch]
    binv = [b_s[rs[i], :] * pin[i] for i in ch]
    nd = [n_s[rs[i], :] * jnp.exp(cum[i] - lw[i]) for i in ch]
    lhs = [jnp.concatenate([nd[i], r_s[rs[i], :] * ecum[i]], axis=0).astype(BF16) for i in ch]
    sk = [_mm_nt(lhs[i], _embed(kinv[i], bd01)) for i in ch]
    sb = [_mm_nt(lhs[i], _embed(binv[i], bd01)) for i in ch]
    tinv = _inv_unit_lower([jnp.where(strict4, sb[i][0:c], 0.0) for i in ch], eye4, bd01)
    av = [_mm(jnp.concatenate([jnp.where(strict4, sk[i][0:c], 0.0), jnp.where(tril4, sk[i][c:2 * c], 0.0)],
                              axis=0), _embed(v_s[rs[i], :], bd01)) for i in ch]
    tn = [_mm(tinv[i], jnp.concatenate([_embed(nd[i], bd01), _embed(av[i][0:c], bd01)], axis=1)) for i in ch]
    bp = [(binv[i] * pc[i]).astype(BF16) for i in ch]
    gh = [lax.dot_general(tn[i].astype(BF16), bp[i], (((0,), (0,)), ((), ())), preferred_element_type=F32)
          for i in ch]
    vk = [_bdot_tn(v_s[rs[i], :], kinv[i] * pc[i]) for i in ch]
    for i in ch:
        lhs_s[2 * i * c:2 * (i + 1) * c, :] = lhs[i]
        t_s[rs[i], :] = tinv[i].astype(BF16)
        a_s[rs[i], :] = jnp.where(tril4, sb[i][c:2 * c], 0.0).astype(BF16)
        xv_s[rs[i], :] = av[i][0:c]
        o_s[rs[i], :] = av[i][c:2 * c]
        p_s[i:i + 1, :] = pc[i]
        g_s[i] = jnp.where(bd, gh[i][0:MIXW], 0.0).astype(BF16)
        d_s[i] = jnp.where(bd, vk[i] - gh[i][MIXW:2 * MIXW], 0.0)

    for ci in range(n_chunks):
        for s in range(nb):
            i = s * n_chunks + ci
            st = st_ref[s]
            s16 = st.astype(BF16)
            h_s[i] = s16
            st_ref[s] = st * p_s[i:i + 1, :] + d_s[i] - jnp.dot(s16, g_s[i], preferred_element_type=F32)

    r2 = [lax.dot_general(lhs_s[2 * i * c:2 * (i + 1) * c, :], h_s[i], (((1,), (1,)), ((), ())),
                          preferred_element_type=F32) for i in ch]
    u = [jnp.dot(t_s[rs[i], :], _embed(r2[i][0:c] + xv_s[rs[i], :], bd01), preferred_element_type=F32)
         for i in ch]
    au = [jnp.dot(a_s[rs[i], :], _embed(u[i], bd01), preferred_element_type=F32) for i in ch]
    for i in ch:
        o_s[rs[i], :] = r2[i][c:2 * c] + o_s[rs[i], :] - au[i]

    last = t_valid - 1
    for s in range(nb):
        sr = slice(s * tt, (s + 1) * tt)
        o = o_s[sr, :]
        mean = _xdot_r01(o, bd01) * (1.0 / HD)
        d = o - mean
        var = _xdot_r01(d * d, bd01) * (1.0 / HD)
        o = d * lax.rsqrt(var + RWKV_GN_EPS) * lnw_ref[...] + lnb_ref[...]
        o = o + rk_s[sr, :] * v_s[sr, :]
        y_ref[s] = (o * gt_s[sr, :]).astype(BF16)

        @pl.when(ti == last // tt)
        def _():
            shift_ref[s] = xext[s, pl.ds(8 + last % tt, 1), :]

        @pl.when(ti == n_tiles - 1)
        def _():
            _store_state(so_ref, st_ref, s)


def _rwkv_call(cd, prev8, s0, lw, *, nb, tt, c, t_valid):
    b, lp, _ = cd.shape
    consts = [lw["rwkv_mu"], lw["rwkv_w0"], lw["rwkv_wb"], lw["rwkv_a0"], lw["rwkv_ab"], lw["rwkv_gb"],
              lw["rwkv_kk"], lw["rwkv_ka"], lw["rwkv_rk"], lw["rwkv_lnw"], lw["rwkv_lnb"]]
    body = functools.partial(_rwkv_kernel, nb=nb, n_tiles=lp // tt, tt=tt, c=c, t_valid=t_valid)
    per_b = lambda s: pl.BlockSpec((nb,) + s, lambda i, j: (i,) + (0,) * len(s))
    n_ch = nb * (tt // c)
    return pl.pallas_call(
        body, grid=(b // nb, lp // tt),
        in_specs=[pl.BlockSpec((nb, tt, D_W), lambda i, j: (i, j, 0)), per_b((8, D_W)),
                  per_b((HEADS, HD, HD))] + [_const_spec(x.shape) for x in consts],
        out_specs=[pl.BlockSpec((nb, tt, MIXW), lambda i, j: (i, j, 0)),
                   per_b((1, D_W)), per_b((HEADS, HD, HD))],
        out_shape=[jax.ShapeDtypeStruct((b, lp, MIXW), BF16),
                   jax.ShapeDtypeStruct((b, 1, D_W), F32),
                   jax.ShapeDtypeStruct((b, HEADS, HD, HD), F32)],
        scratch_shapes=[pltpu.VMEM((nb, tt + 8, D_W), F32)] + [pltpu.VMEM((nb * tt, MIXW), F32)] * 8
        + [pltpu.VMEM((nb * tt, MIXW), BF16)] * 2 + [pltpu.VMEM((2 * nb * tt, MIXW), BF16),
                                                     pltpu.VMEM((-(-n_ch // 8) * 8, MIXW), F32),
                                                     pltpu.VMEM((n_ch, MIXW, MIXW), BF16),
                                                     pltpu.VMEM((n_ch, MIXW, MIXW), BF16),
                                                     pltpu.VMEM((n_ch, MIXW, MIXW), F32)]
        + [pltpu.VMEM((nb * tt, MIXW), F32)] * 2 + [pltpu.VMEM((nb, MIXW, MIXW), F32)],
        compiler_params=pltpu.CompilerParams(dimension_semantics=("parallel", "arbitrary"),
                                             vmem_limit_bytes=VMEM_LIMIT),
        name="rwkv7")(cd, prev8, s0, *consts)


def _pad_cols(w, width):
    return jnp.pad(w, ((0, 0), (0, width - w.shape[1])))


def _mla_head_cols(w, n_heads, per_head, pieces):
    blocks = []
    for h in range(n_heads):
        cols = [sign * w[:, h * per_head + lo:h * per_head + hi] for lo, hi, sign in pieces]
        blocks.append(_pad_cols(jnp.concatenate(cols, axis=1), MLA_PAD))
    return jnp.concatenate(blocks, axis=1)


def _block_diag_heads(w):
    eye = jnp.eye(HEADS, dtype=w.dtype)[:, None, :, None]
    return (w[:, :, None, :] * eye).reshape(MIXW, MIXW)


def _prep_layer_weights(p, l):
    half = MLA_ROPE // 2
    row = lambda v: v.reshape(1, -1).astype(F32)
    lw = {}
    for tag in ("1", "2"):
        lw["ffn" + tag] = dict(g_ffn=row(p["norm_ffn" + tag][l]), layer=l, w1=p["ffn%s_w1_bf16" % tag],
                               w2=p["ffn%s_w2_bf16" % tag])
    w_in = p["w_in"][l]
    o = 0
    cq = w_in[:, o:o + 256]; o += 256
    ckv = w_in[:, o:o + 128]; o += 128
    kpe = w_in[:, o:o + MLA_ROPE]; o += MLA_ROPE
    xb_gb = w_in[:, o:o + 512]; o += 512
    qkvz = w_in[:, o:o + 1024]; o += 1024
    a_in = w_in[:, o:o + HEADS]; o += HEADS
    b_in = w_in[:, o:o + HEADS]; o += HEADS
    rw = w_in[:, o:o + 1024]
    kpe_rot = jnp.concatenate([-kpe[:, half:], kpe[:, :half]], axis=1)
    w_in_p = jnp.concatenate([
        cq, ckv, _pad_cols(kpe, 128), _pad_cols(kpe_rot, 128),
        xb_gb,
        qkvz, jnp.repeat(a_in, HD, axis=1), jnp.repeat(b_in, HD, axis=1),
        rw], axis=1)
    lw["ffn1"]["g_mix"] = row(p["norm_mix"][l])
    lw["ffn1"]["w_in"] = w_in_p.astype(BF16)
    w_out = p["w_out"][l]
    wo_a = jnp.concatenate([jnp.pad(w_out[h * HD:(h + 1) * HD], ((0, MLA_PAD - HD), (0, 0)))
                            for h in range(HEADS)], axis=0)
    lw["w_out"] = [wo_a.astype(BF16)] + [w_out[MIXW * i:MIXW * (i + 1)].astype(BF16) for i in (1, 2, 3)]
    wuq = p["mla_w_uq"][l]
    wq = _mla_head_cols(wuq, HEADS, MLA_QK, [(MLA_NOPE, MLA_QK, 1.0), (0, MLA_NOPE, 1.0)])
    wq_rot = _mla_head_cols(wuq, HEADS, MLA_QK, [(MLA_NOPE + half, MLA_QK, -1.0),
                                                (MLA_NOPE, MLA_NOPE + half, 1.0)])
    lw["mla_wq"] = jnp.concatenate([wq, wq_rot], axis=1).astype(BF16)
    wukv = p["mla_w_ukv"][l]
    zero_rope = jnp.zeros((wukv.shape[0], MLA_ROPE), wukv.dtype)
    wk = jnp.concatenate([_pad_cols(jnp.concatenate([zero_rope, wukv[:, h * 128:h * 128 + MLA_NOPE]], axis=1),
                                    MLA_PAD) for h in range(HEADS)], axis=1)
    wv = jnp.concatenate([_pad_cols(wukv[:, h * 128 + MLA_NOPE:(h + 1) * 128], MLA_PAD)
                          for h in range(HEADS)], axis=1)
    lw["mla_wkv"] = jnp.concatenate([wk, wv], axis=1).astype(BF16)
    lw["mla_wv"] = wv.astype(BF16)
    lw["mla_wkt"] = jnp.concatenate([wukv[:, h * 128:h * 128 + MLA_NOPE] for h in range(HEADS)],
                                    axis=1).T.astype(BF16)
    lw["mla_gqa"] = row(p["mla_q_a_norm"][l])
    lw["mla_gkva"] = row(p["mla_kv_a_norm"][l])
    perm = lambda g: row(_pad_cols(jnp.concatenate([g[MLA_NOPE:], g[:MLA_NOPE]])[None], MLA_PAD))
    lw["mla_gq"] = perm(p["mla_q_norm"][l])
    lw["mla_gk"] = perm(p["mla_k_norm"][l])
    lw["lru_cw"] = p["lru_conv_w"][l]
    lw["lru_cb"] = row(p["lru_conv_b"][l])
    lw["lru_wa"] = _block_diag_heads(p["lru_wa"][l]).astype(BF16)
    lw["lru_ba"] = row(p["lru_ba"][l])
    lw["lru_wx"] = _block_diag_heads(p["lru_wx"][l]).astype(BF16)
    lw["lru_bx"] = row(p["lru_bx"][l])
    lw["lru_lam"] = row(p["lru_lambda"][l])
    lw["gdn_cw"] = p["gdn_conv_w"][l]
    lw["gdn_alog"] = row(jnp.repeat(p["gdn_a_log"][l], HD))
    lw["gdn_dtb"] = row(jnp.repeat(p["gdn_dt_bias"][l], HD))
    lw["gdn_gon"] = row(jnp.tile(p["gdn_o_norm"][l], HEADS))
    z64 = jnp.zeros((64, MIXW), F32)
    lw["rwkv_mu"] = row(p["rwkv_mu"][l])
    lw["rwkv_w0"] = row(p["rwkv_w0"][l])
    lw["rwkv_wb"] = jnp.concatenate([p["rwkv_w_b"][l], z64], axis=0).astype(BF16)
    lw["rwkv_a0"] = row(p["rwkv_a0"][l])
    lw["rwkv_ab"] = jnp.concatenate([z64, p["rwkv_a_b"][l]], axis=0).astype(BF16)
    lw["rwkv_gb"] = p["rwkv_g_b"][l].astype(BF16)
    lw["rwkv_kk"] = row(p["rwkv_k_k"][l])
    lw["rwkv_ka"] = row(p["rwkv_k_a"][l])
    lw["rwkv_rk"] = row(p["rwkv_r_k"][l])
    lw["rwkv_lnw"] = row(p["rwkv_ln_w"][l])
    lw["rwkv_lnb"] = row(p["rwkv_ln_b"][l])
    return lw


def _rope_tables(pos):
    inv = ROPE_THETA ** (-jnp.arange(0, MLA_ROPE, 2, dtype=F32) / MLA_ROPE)
    ang = pos.astype(F32)[:, None] * inv[None, :]
    cos, sin = jnp.cos(ang), jnp.sin(ang)
    n = pos.shape[0]
    cos_t = jnp.concatenate([cos, cos, jnp.ones((n, MLA_PAD - MLA_ROPE), F32)], axis=1)
    sin_t = jnp.concatenate([sin, sin, jnp.zeros((n, MLA_PAD - MLA_ROPE), F32)], axis=1)
    return cos_t, sin_t


def _hist8(rows):
    return jnp.pad(rows.astype(F32), ((0, 0), (8 - rows.shape[1], 0), (0, 0)))


def _run_group(x, lws, final_g, states, cfg):
    b, lp, d = x.shape
    n = b * lp
    t_valid, tm, tt, c = cfg["t_valid"], cfg["tm"], cfg["tt"], cfg["c"]
    cos_t, sin_t = cfg["rope"]
    xf = x.reshape(n, d)
    new_states = []
    depth = len(lws)
    for l in range(depth):
        lw = lws[l]
        st = states[l]
        xf, q, k, v, ckv, kpe, cb, cc, cd = _token_call(xf, lw["ffn1"], tm=tm,
                                                        post=(cos_t, sin_t, lw, (b, lp, t_valid)))
        r3 = lambda a: a.reshape(b, lp, a.shape[-1])
        if st["mla"] is None:
            ya = _mla_prompt_call(r3(q), r3(k), r3(v), tq=cfg["tq"], tk=cfg["tk"])
        else:
            cache_ckv, cache_kpe_t = st["mla"]
            ya = _mla_sample_call(r3(q), r3(k), r3(v), cache_ckv, cache_kpe_t, l, lw)
        yb, lru_conv, lru_h = _lru_call(r3(cb), st["lru_conv"], st["lru_h"], lw, tt=tt, t_valid=t_valid)
        yc, gdn_conv, gdn_s = _gdn_call(r3(cc), st["gdn_conv"], st["gdn_s"], lw, nb=cfg["nb"], tt=tt, c=c,
                                        t_valid=t_valid)
        yd, shift, rwkv_s = _rwkv_call(r3(cd), st["rwkv_shift"], st["rwkv_s"], lw, nb=cfg["nb"], tt=tt, c=c,
                                       t_valid=t_valid)
        ys = [ya.reshape(n, -1), yb.reshape(n, -1), yc.reshape(n, -1), yd.reshape(n, -1)]
        xf = _token_call(xf, lw["ffn2"], pre=(ys, lw["w_out"]), tm=tm,
                         final_g=final_g if l == depth - 1 else None)[0]
        new_states.append((ckv.reshape(b, t_valid, -1), kpe.reshape(b, t_valid, -1), lru_conv, lru_h[:, 0],
                           gdn_conv, gdn_s, shift[:, 0], rwkv_s))
    stacked = [jnp.stack(t) for t in zip(*new_states)]
    return xf.reshape(b, lp, d), stacked


def _group_config(b, t_valid):
    divisor = lambda target: max(d for d in range(1, target + 1) if b % d == 0)
    if t_valid <= CHUNK:
        lp = t_valid
        return dict(lp=lp, t_valid=t_valid, tm=b * lp if b * lp <= 512 else lp, tt=lp, c=lp, tq=lp, tk=1024,
                    nb=divisor(8))
    lp = -(-t_valid // 384) * 384
    return dict(lp=lp, t_valid=t_valid, tm=384, tt=384, c=CHUNK, tq=384, tk=384, nb=divisor(2))


def kernel(x_prompt, x_sample, cache_mla_ckv, cache_mla_kpe, state_lru_conv, state_lru_h, state_gdn_conv, state_gdn_s, state_rwkv_shift, state_rwkv_s, meta_tokens, norm_ffn1, ffn1_w1, ffn1_w2, norm_mix, w_in, mla_q_a_norm, mla_w_uq, mla_kv_a_norm, mla_w_ukv, mla_q_norm, mla_k_norm, lru_conv_w, lru_conv_b, lru_wa, lru_ba, lru_wx, lru_bx, lru_lambda, gdn_conv_w, gdn_a_log, gdn_dt_bias, gdn_o_norm, rwkv_mu, rwkv_w0, rwkv_w_b, rwkv_a0, rwkv_a_b, rwkv_g_b, rwkv_k_k, rwkv_k_a, rwkv_r_k, rwkv_ln_w, rwkv_ln_b, w_out, norm_ffn2, ffn2_w1, ffn2_w2, final_norm):
    p = dict(norm_ffn1=norm_ffn1, ffn1_w1=ffn1_w1, ffn1_w2=ffn1_w2, norm_mix=norm_mix, w_in=w_in,
             mla_q_a_norm=mla_q_a_norm, mla_w_uq=mla_w_uq, mla_kv_a_norm=mla_kv_a_norm,
             mla_w_ukv=mla_w_ukv, mla_q_norm=mla_q_norm, mla_k_norm=mla_k_norm,
             lru_conv_w=lru_conv_w, lru_conv_b=lru_conv_b, lru_wa=lru_wa, lru_ba=lru_ba,
             lru_wx=lru_wx, lru_bx=lru_bx, lru_lambda=lru_lambda, gdn_conv_w=gdn_conv_w,
             gdn_a_log=gdn_a_log, gdn_dt_bias=gdn_dt_bias, gdn_o_norm=gdn_o_norm, rwkv_mu=rwkv_mu,
             rwkv_w0=rwkv_w0, rwkv_w_b=rwkv_w_b, rwkv_a0=rwkv_a0, rwkv_a_b=rwkv_a_b, rwkv_g_b=rwkv_g_b,
             rwkv_k_k=rwkv_k_k, rwkv_k_a=rwkv_k_a, rwkv_r_k=rwkv_r_k, rwkv_ln_w=rwkv_ln_w,
             rwkv_ln_b=rwkv_ln_b, w_out=w_out, norm_ffn2=norm_ffn2, ffn2_w1=ffn2_w1, ffn2_w2=ffn2_w2)
    depth = w_in.shape[0]
    for name in ("ffn1_w1", "ffn1_w2", "ffn2_w1", "ffn2_w2"):
        p[name + "_bf16"] = p[name].astype(BF16)
    lws = [_prep_layer_weights(p, l) for l in range(depth)]
    final_g = final_norm.reshape(1, -1).astype(F32)
    d = x_prompt.shape[-1]

    bp, seq, _ = x_prompt.shape
    tp = N_META + seq
    cfg = _group_config(bp, tp)
    lp = cfg["lp"]
    x0 = jnp.concatenate([jnp.broadcast_to(meta_tokens.astype(F32)[None], (bp, N_META, d)), x_prompt,
                          jnp.zeros((bp, lp - tp, d), F32)], axis=1)
    cfg["rope"] = tuple(jnp.tile(t, (bp, 1)) for t in _rope_tables(jnp.arange(lp)))
    zero = dict(mla=None, lru_conv=jnp.zeros((bp, 8, MIXW), F32), lru_h=jnp.zeros((bp, 1, MIXW), F32),
                gdn_conv=jnp.zeros((bp, 8, 3 * MIXW), F32), gdn_s=jnp.zeros((bp, HEADS, HD, HD), F32),
                rwkv_shift=jnp.zeros((bp, 8, D_W), F32), rwkv_s=jnp.zeros((bp, HEADS, HD, HD), F32))
    yp, p_new = _run_group(x0, lws, final_g, [zero] * depth, cfg)

    bs, ts, _ = x_sample.shape
    past = cache_mla_ckv.shape[2]
    cfg_s = _group_config(bs, ts)
    cfg_s["rope"] = tuple(jnp.tile(t, (bs, 1)) for t in _rope_tables(past + jnp.arange(ts)))
    cache_kpe_t = jnp.swapaxes(cache_mla_kpe, 2, 3)
    st_s = [dict(mla=(cache_mla_ckv, cache_kpe_t), lru_conv=_hist8(state_lru_conv[l]),
                 lru_h=state_lru_h[l][:, None].astype(F32), gdn_conv=_hist8(state_gdn_conv[l]),
                 gdn_s=state_gdn_s[l], rwkv_shift=_hist8(state_rwkv_shift[l][:, None]),
                 rwkv_s=state_rwkv_s[l]) for l in range(depth)]
    ys, s_new = _run_group(x_sample, lws, final_g, st_s, cfg_s)
    return (yp[:, N_META:tp], ys) + tuple(p_new) + tuple(s_new)
```

```python
import functools
import math

import jax
import jax.numpy as jnp
import numpy as np
from jax import lax
from jax.experimental import pallas as pl
from jax.experimental.pallas import tpu as pltpu

F32 = jnp.float32
BF16 = jnp.bfloat16

EPS = 1e-6
N_META = 16
CHUNK = 64
CONV_W = 4
HEADS = 4
HD = 64
MIXW = HEADS * HD
MLA_NOPE = 64
MLA_ROPE = 32
MLA_QK = MLA_NOPE + MLA_ROPE
MLA_V = 64
MLA_PAD = 128
ROPE_THETA = 10000.0
LRU_C = 8.0
RWKV_GN_EPS = 64e-5
RWKV_DECAY_SCALE = -0.606531

VMEM_LIMIT = 56 * 1024 * 1024
MXU_N = 256

A_W = 512
B_W = 512
C_W = 1152
D_W = 1024


def _bdot(a, b):
    return jnp.dot(a.astype(BF16), b.astype(BF16), preferred_element_type=F32)


def _bdot_tn(a, b):
    return lax.dot_general(a.astype(BF16), b.astype(BF16), (((0,), (0,)), ((), ())),
                           preferred_element_type=F32)


def _split3(x):
    x1 = x.astype(BF16)
    r1 = x - x1.astype(F32)
    x2 = r1.astype(BF16)
    r2 = r1 - x2.astype(F32)
    return x1, x2, r2.astype(BF16)


def _xdot_r01(x, m01):
    return jnp.dot(x.astype(BF16), m01, preferred_element_type=F32)


def _xdot_l01(m01, x):
    x1, x2, x3 = _split3(x)
    d = functools.partial(jnp.dot, preferred_element_type=F32)
    return d(m01, x1) + d(m01, x2) + d(m01, x3)


def _xdot3_r01(x, m01):
    x1, x2, x3 = _split3(x)
    d = functools.partial(jnp.dot, preferred_element_type=F32)
    return d(x1, m01) + d(x2, m01) + d(x3, m01)


def _sigmoid(x):
    return 0.5 * jnp.tanh(0.5 * x) + 0.5


def _silu(x):
    return x * _sigmoid(x)


def _softplus(x):
    return jnp.maximum(x, 0.0) + jnp.log(1.0 + jnp.exp(-jnp.abs(x)))


def _gelu_tanh(x):
    return 0.5 * x * (1.0 + jnp.tanh(0.7978845608028654 * (x + 0.044715 * (x * x * x))))


def _rms_rows(x, g):
    return x * lax.rsqrt(jnp.mean(x * x, axis=-1, keepdims=True) + EPS) * g


def _block_diag_mask():
    r = lax.broadcasted_iota(jnp.int32, (MIXW, MIXW), 0) >> 6
    c = lax.broadcasted_iota(jnp.int32, (MIXW, MIXW), 1) >> 6
    return r == c


def _embed(x, bd01):
    c = x.shape[0]
    x = x.astype(BF16)
    if c < HD:
        x = jnp.concatenate([x, jnp.zeros((HD - c, MIXW), BF16)], axis=0)
    return jnp.concatenate([x] * HEADS, axis=0) * bd01


def _mm(a, b):
    return jnp.dot(a.astype(BF16), b, preferred_element_type=F32)


def _mm_nt(a, b):
    return lax.dot_general(a.astype(BF16), b, (((1,), (1,)), ((), ())), preferred_element_type=F32)


def _head_masks(c):
    row = lax.broadcasted_iota(jnp.int32, (c, MIXW), 0)
    col = lax.broadcasted_iota(jnp.int32, (c, MIXW), 1) & (HD - 1)
    return row >= col, row > col, row == col


def _tril01(c):
    r = lax.broadcasted_iota(jnp.int32, (c, c), 0)
    k = lax.broadcasted_iota(jnp.int32, (c, c), 1)
    return jnp.where(r >= k, 1.0, 0.0).astype(BF16)


def _inv_unit_lower(lms, eye4, bd01):
    c = lms[0].shape[0]
    levels = int(math.log2(c))
    eye = jnp.where(eye4, 1.0, 0.0)
    n = [-x for x in lms]
    t = [eye + x for x in n]
    if levels < 2:
        return t
    p = [_mm(x, _embed(x, bd01)) for x in n]
    for _ in range(levels - 2):
        r = [_mm(jnp.concatenate([pi, ti], axis=0), _embed(pi, bd01)) for pi, ti in zip(p, t)]
        p = [x[0:c] for x in r]
        t = [ti + x[c:2 * c] for ti, x in zip(t, r)]
    return [ti + _mm(ti, _embed(pi, bd01)) for pi, ti in zip(p, t)]


def _load_state(st_ref, s0_ref, s):
    st_ref[s] = jnp.zeros((MIXW, MIXW), F32)
    for h in range(HEADS):
        st_ref[s, h * HD:(h + 1) * HD, h * HD:(h + 1) * HD] = s0_ref[s, h].astype(F32)


def _store_state(so_ref, st_ref, s):
    for h in range(HEADS):
        so_ref[s, h] = st_ref[s, h * HD:(h + 1) * HD, h * HD:(h + 1) * HD]


def _shift_rows(x, s, fill):
    rolled = pltpu.roll(x, s, 0)
    row = lax.broadcasted_iota(jnp.int32, x.shape, 0)
    return jnp.where(row >= s, rolled, fill)


def _causal_conv(xext_ref, w_ref, tt):
    acc = None
    for j in range(CONV_W):
        term = xext_ref[pl.ds(8 - (CONV_W - 1) + j, tt), :] * w_ref[j:j + 1, :]
        acc = term if acc is None else acc + term
    return acc


def _const_spec(shape):
    nd = len(shape)
    return pl.BlockSpec(shape, lambda *_: (0,) * nd, pipeline_mode=pl.Buffered(1))


def _token_kernel(*refs, has_pre, has_post, has_final, d_ff, ff_chunk):
    it = iter(refs)
    x_ref = next(it)
    if has_pre:
        y_refs = [next(it) for _ in range(4)]
        wo_refs = [next(it) for _ in range(4)]
    gffn_ref, w1_ref, w2_ref = next(it), next(it), next(it)
    if has_post:
        gmix_ref, win_ref = next(it), next(it)
        prep_in = [next(it) for _ in range(8)]
    if has_final:
        gfin_ref = next(it)
    xo_ref = next(it)
    if has_post:
        prep_out = [next(it) for _ in range(5)]
        col_refs = [next(it) for _ in range(3)]

    x = x_ref[...]
    if has_pre:
        for y_ref, wo_ref in zip(y_refs, wo_refs):
            x = x + jnp.dot(y_ref[...], wo_ref[...], preferred_element_type=F32)
    h = _rms_rows(x, gffn_ref[...]).astype(BF16)
    acc = None
    for lo, hi in ((0, ff_chunk), (ff_chunk, d_ff)):
        gate = jnp.dot(h, w1_ref[:, lo:hi], preferred_element_type=F32)
        up = jnp.dot(h, w1_ref[:, d_ff + lo:d_ff + hi], preferred_element_type=F32)
        act = (_silu(gate) * up).astype(BF16)
        part = jnp.dot(act, w2_ref[lo:hi, :], preferred_element_type=F32)
        acc = part if acc is None else acc + part
    x = x + 0.5 * acc
    if has_final:
        xo_ref[...] = _rms_rows(x, gfin_ref[...])
    else:
        xo_ref[...] = x
    if has_post:
        hm = _rms_rows(x, gmix_ref[...]).astype(BF16)
        _mla_prep(jnp.dot(hm, win_ref[:, 0:A_W], preferred_element_type=F32), *prep_in, *prep_out)
        off = A_W
        for c_ref in col_refs:
            wd = c_ref.shape[-1]
            c_ref[...] = jnp.dot(hm, win_ref[:, off:off + wd], preferred_element_type=F32)
            off += wd


def _token_call(x, lw, *, pre=None, post=None, final_g=None, tm):
    n, d = x.shape
    d_ff = lw["w2"].shape[1]
    ff_chunk = -(-(d_ff // MXU_N) // 2) * MXU_N
    row = lambda w: pl.BlockSpec((tm, w), lambda i: (i, 0))
    args, specs = [x], [row(d)]
    if pre is not None:
        ys, wos = pre
        for y in ys:
            args.append(y)
            specs.append(row(y.shape[1]))
        for w in wos:
            args.append(w)
            specs.append(_const_spec(w.shape))
    args.append(lw["g_ffn"])
    specs.append(_const_spec(lw["g_ffn"].shape))
    layer = lw["layer"]
    for name in ("w1", "w2"):
        args.append(lw[name])
        specs.append(pl.BlockSpec((None,) + lw[name].shape[1:], lambda i: (layer, 0, 0),
                                  pipeline_mode=pl.Buffered(1)))
    if post is not None:
        cos_t, sin_t, mw, (b, lp, t_valid) = post
        for name in ("g_mix", "w_in"):
            args.append(lw[name])
            specs.append(_const_spec(lw[name].shape))
        args += [cos_t, sin_t]
        specs += [row(MLA_PAD), row(MLA_PAD)]
        for name in ("mla_gqa", "mla_wq", "mla_gkva", "mla_wkv", "mla_gq", "mla_gk"):
            args.append(mw[name])
            specs.append(_const_spec(mw[name].shape))
    if final_g is not None:
        args.append(final_g)
        specs.append(_const_spec(final_g.shape))
    out_shape = [jax.ShapeDtypeStruct((n, d), F32)]
    out_specs = [row(d)]
    if post is not None:
        hw = HEADS * MLA_PAD
        out_shape += [jax.ShapeDtypeStruct((n, hw), BF16)] * 3
        out_specs += [row(hw)] * 3
        for wd in (128, MLA_ROPE):
            if lp > t_valid:
                tps = lp // tm
                out_shape.append(jax.ShapeDtypeStruct((b, t_valid, wd), F32))
                out_specs.append(pl.BlockSpec((None, tm, wd), lambda i: (i // tps, i % tps, 0)))
            else:
                out_shape.append(jax.ShapeDtypeStruct((n, wd), F32))
                out_specs.append(row(wd))
        for wd in (B_W, C_W, D_W):
            out_shape.append(jax.ShapeDtypeStruct((n, wd), F32))
            out_specs.append(row(wd))
    body = functools.partial(_token_kernel, has_pre=pre is not None, has_post=post is not None,
                             has_final=final_g is not None, d_ff=d_ff, ff_chunk=ff_chunk)
    return pl.pallas_call(
        body, grid=(n // tm,), in_specs=specs, out_specs=out_specs, out_shape=out_shape,
        compiler_params=pltpu.CompilerParams(dimension_semantics=("parallel",),
                                             vmem_limit_bytes=VMEM_LIMIT),
        name="token_block")(*args)


def _mla_prep(ca, cos_ref, sin_ref, gqa_ref, wq_ref, gkva_ref, wkv_ref, gq_ref, gk_ref,
              q_ref, k_ref, v_ref, ckv_ref, kpe_ref):
    cs = cos_ref[...]
    sn = sin_ref[...]
    hw = HEADS * MLA_PAD
    cqn = _rms_rows(ca[:, 0:256], gqa_ref[...])
    qq = _bdot(cqn, wq_ref[...])
    scale = math.log2(math.e) / math.sqrt(MLA_QK)
    for h in range(HEADS):
        lo = h * MLA_PAD
        blk = qq[:, lo:lo + MLA_PAD] * cs + qq[:, hw + lo:hw + lo + MLA_PAD] * sn
        ms = jnp.sum(blk * blk, axis=-1, keepdims=True) * (1.0 / MLA_QK)
        q_ref[:, lo:lo + MLA_PAD] = (blk * lax.rsqrt(ms + EPS) * (gq_ref[...] * scale)).astype(BF16)
    ckv = _rms_rows(ca[:, 256:384], gkva_ref[...])
    ckv_ref[...] = ckv
    kblk = ca[:, 384:512]
    rope_lane = lax.broadcasted_iota(jnp.int32, kblk.shape, 1) < MLA_ROPE
    kpe = jnp.where(rope_lane, kblk * cs, 0.0) + pltpu.roll(kblk, MLA_PAD - MLA_ROPE, 1) * sn
    kpe_ref[...] = kpe[:, 0:MLA_ROPE]
    kv = _bdot(ckv, wkv_ref[...])
    ones_lane = lax.broadcasted_iota(jnp.int32, (kv.shape[0], MLA_PAD), 1) == MLA_V
    for h in range(HEADS):
        lo = h * MLA_PAD
        kb = kv[:, lo:lo + MLA_PAD] + kpe
        ms = jnp.sum(kb * kb, axis=-1, keepdims=True) * (1.0 / MLA_QK)
        k_ref[:, lo:lo + MLA_PAD] = (kb * lax.rsqrt(ms + EPS) * gk_ref[...]).astype(BF16)
        v_ref[:, lo:lo + MLA_PAD] = jnp.where(ones_lane, 1.0, kv[:, hw + lo:hw + lo + MLA_PAD]).astype(BF16)


def _softmax_step(carry, s, vblk):
    m, acc = carry
    m_new = jnp.maximum(m, jnp.max(s, axis=-1, keepdims=True))
    p = jnp.exp2(s - m_new)
    acc = jnp.exp2(m - m_new) * acc + jnp.dot(p.astype(BF16), vblk, preferred_element_type=F32)
    return m_new, acc


def _softmax_init(t):
    return jnp.full((t, 1), -1e30, F32), jnp.zeros((t, MLA_PAD), F32)


def _softmax_finish(acc):
    return (acc / acc[:, MLA_V:MLA_V + 1]).astype(BF16)


def _mla_prompt_kernel(q_ref, k_ref, v_ref, o_ref, *, tq, tk, n_kb):
    q0 = pl.program_id(1) * tq
    first_chunk = (q0 - N_META) >> 6
    last_chunk = (q0 + tq - 1 - N_META) >> 6
    n_full = (N_META + CHUNK * (first_chunk + 1)) // tk
    nkb = jnp.minimum((N_META + CHUNK * (last_chunk + 1)) // tk, n_kb)
    qchunk = (q0 + lax.broadcasted_iota(jnp.int32, (tq, 1), 0) - N_META) >> 6
    hs = range(HEADS)
    qs = [q_ref[:, h * MLA_PAD:(h + 1) * MLA_PAD] for h in hs]

    def body(it, state, masked, nblk):
        k0s = [pl.multiple_of((it * nblk + j) * tk, tk) for j in range(nblk)]
        blk = [slice(h * MLA_PAD, (h + 1) * MLA_PAD) for h in hs]
        s = [[lax.dot_general(qs[h], k_ref[pl.ds(k0, tk), blk[h]], (((1,), (1,)), ((), ())),
                              preferred_element_type=F32) for k0 in k0s] for h in hs]
        if masked:
            vis = [((k0 + lax.broadcasted_iota(jnp.int32, (1, tk), 1) - N_META) >> 6) <= qchunk for k0 in k0s]
            s = [[jnp.where(vis[j], s[h][j], -1e30) for j in range(nblk)] for h in hs]
        m_new = []
        for h in hs:
            m = state[h][0]
            for x in s[h]:
                m = jnp.maximum(m, jnp.max(x, axis=-1, keepdims=True))
            m_new.append(m)
        p = [[jnp.exp2(x - m_new[h]).astype(BF16) for x in s[h]] for h in hs]
        pv = []
        for h in hs:
            acc = None
            for j, k0 in enumerate(k0s):
                part = jnp.dot(p[h][j], v_ref[pl.ds(k0, tk), blk[h]], preferred_element_type=F32)
                acc = part if acc is None else acc + part
            pv.append(acc)
        return tuple((m_new[h], jnp.exp2(state[h][0] - m_new[h]) * state[h][1] + pv[h]) for h in hs)

    state = tuple(_softmax_init(tq) for _ in hs)
    n_pair = n_full // 2
    state = lax.fori_loop(0, n_pair, functools.partial(body, masked=False, nblk=2), state)
    state = lax.fori_loop(2 * n_pair, nkb, functools.partial(body, masked=True, nblk=1), state)
    lp = n_kb * tk
    k_t = pl.multiple_of(jnp.minimum(nkb * tk, lp - N_META), N_META)
    tail_chunk = (k_t + lax.broadcasted_iota(jnp.int32, (1, N_META), 1) - N_META) >> 6
    visible = (tail_chunk <= qchunk) & (nkb < n_kb)
    for h in hs:
        blk = slice(h * MLA_PAD, (h + 1) * MLA_PAD)
        s = lax.dot_general(qs[h], k_ref[pl.ds(k_t, N_META), blk], (((1,), (1,)), ((), ())),
                            preferred_element_type=F32)
        _, acc = _softmax_step(state[h], jnp.where(visible, s, -1e30), v_ref[pl.ds(k_t, N_META), blk])
        o_ref[:, blk] = _softmax_finish(acc)


def _mla_prompt_call(q, k, v, *, tq, tk):
    b, lp, hw = q.shape
    assert tq == tk and tq % CHUNK == 0 and lp % tk == 0
    body = functools.partial(_mla_prompt_kernel, tq=tq, tk=tk, n_kb=lp // tk)
    return pl.pallas_call(
        body, grid=(b, lp // tq),
        in_specs=[pl.BlockSpec((None, tq, hw), lambda i, j: (i, j, 0)),
                  pl.BlockSpec((None, lp, hw), lambda i, j: (i, 0, 0)),
                  pl.BlockSpec((None, lp, hw), lambda i, j: (i, 0, 0))],
        out_specs=pl.BlockSpec((None, tq, hw), lambda i, j: (i, j, 0)),
        out_shape=jax.ShapeDtypeStruct((b, lp, hw), BF16),
        compiler_params=pltpu.CompilerParams(dimension_semantics=("parallel", "arbitrary"),
                                             vmem_limit_bytes=VMEM_LIMIT),
        name="mla_prompt_attn")(q, k, v)


def _mla_sample_kernel(q_ref, kn_ref, vn_ref, cckv_ref, ckpet_ref, wkt_ref, wv_ref, gk_ref, o_ref):
    t = q_ref.shape[0]
    hs = range(HEADS)
    blk = [slice(h * MLA_PAD, (h + 1) * MLA_PAD) for h in hs]
    ckv = cckv_ref[...].astype(BF16)
    knt = lax.dot_general(wkt_ref[...], ckv, (((1,), (1,)), ((), ())), preferred_element_type=F32)
    kpet = ckpet_ref[...]
    ssq_pe = jnp.sum(kpet * kpet, axis=0, keepdims=True)
    kpet16 = kpet.astype(BF16)
    zpad = jnp.zeros((MLA_PAD - MLA_QK, kpet.shape[1]), BF16)
    kts = [knt[h * MLA_NOPE:(h + 1) * MLA_NOPE, :] for h in hs]
    rk = [lax.rsqrt((jnp.sum(x * x, axis=0, keepdims=True) + ssq_pe) * (1.0 / MLA_QK) + EPS) for x in kts]
    qg = [(q_ref[:, blk[h]].astype(F32) * gk_ref[...]).astype(BF16) for h in hs]
    s_c = [jnp.dot(qg[h], jnp.concatenate([kpet16, kts[h].astype(BF16), zpad], axis=0),
                   preferred_element_type=F32) * rk[h] for h in hs]
    s_n = [lax.dot_general(q_ref[:, blk[h]], kn_ref[:, blk[h]], (((1,), (1,)), ((), ())),
                           preferred_element_type=F32) for h in hs]
    m = [jnp.maximum(jnp.max(s_c[h], axis=-1, keepdims=True), jnp.max(s_n[h], axis=-1, keepdims=True))
         for h in hs]
    p_c = [jnp.exp2(s_c[h] - m[h]) for h in hs]
    p_n = [jnp.exp2(s_n[h] - m[h]) for h in hs]
    den = [jnp.sum(p_c[h], axis=-1, keepdims=True) + jnp.sum(p_n[h], axis=-1, keepdims=True) for h in hs]
    lat = jnp.dot(jnp.concatenate([x.astype(BF16) for x in p_c], axis=0), ckv, preferred_element_type=F32)
    for h in hs:
        o = jnp.dot(lat[h * t:(h + 1) * t].astype(BF16), wv_ref[:, blk[h]], preferred_element_type=F32)
        o = o + jnp.dot(p_n[h].astype(BF16), vn_ref[:, blk[h]], preferred_element_type=F32)
        o_ref[:, blk[h]] = (o / den[h]).astype(BF16)


def _mla_sample_call(q, kn, vn, cache_ckv, cache_kpe_t, layer, lw):
    b, t, hw = q.shape
    past = cache_ckv.shape[2]
    seq = lambda: pl.BlockSpec((None, t, hw), lambda i: (i, 0, 0))
    consts = [lw["mla_wkt"], lw["mla_wv"], lw["mla_gk"]]
    return pl.pallas_call(
        _mla_sample_kernel, grid=(b,),
        in_specs=[seq(), seq(), seq(),
                  pl.BlockSpec((None, None, past, cache_ckv.shape[3]), lambda i: (layer, i, 0, 0)),
                  pl.BlockSpec((None, None, cache_kpe_t.shape[2], past), lambda i: (layer, i, 0, 0))]
        + [_const_spec(c.shape) for c in consts],
        out_specs=seq(),
        out_shape=jax.ShapeDtypeStruct((b, t, hw), BF16),
        compiler_params=pltpu.CompilerParams(dimension_semantics=("arbitrary",),
                                             vmem_limit_bytes=VMEM_LIMIT),
        name="mla_sample_attn")(q, kn, vn, cache_ckv, cache_kpe_t, *consts)


def _lru_kernel(cb_ref, buf_ref, h0_ref, cw_ref, cbias_ref, wa_ref, ba_ref, wx_ref, bx_ref, lam_ref,
                y_ref, nbuf_ref, hl_ref, xext, hcar, sa, sb, sh, *, nb, tt, t_valid):
    for s in range(nb):
        _lru_sequence(cb_ref.at[s], buf_ref.at[s], h0_ref.at[s], cw_ref, cbias_ref, wa_ref, ba_ref, wx_ref,
                      bx_ref, lam_ref, y_ref.at[s], nbuf_ref.at[s], hl_ref.at[s], xext.at[s], hcar.at[s],
                      sa.at[s], sb.at[s], sh.at[s], tt=tt, t_valid=t_valid)


def _lru_sequence(cb_ref, buf_ref, h0_ref, cw_ref, cbias_ref, wa_ref, ba_ref, wx_ref, bx_ref, lam_ref,
                  y_ref, nbuf_ref, hl_ref, xext, hcar, sa, sb, sh, *, tt, t_valid):
    ti = pl.program_id(1)

    @pl.when(ti == 0)
    def _():
        xext[0:8, :] = buf_ref[...]
        hcar[...] = h0_ref[...]

    @pl.when(ti > 0)
    def _():
        xext[0:8, :] = xext[tt:tt + 8, :]

    xext[8:tt + 8, :] = cb_ref[:, 0:MIXW]
    xc = _causal_conv(xext, cw_ref, tt) + cbias_ref[...]
    r = _sigmoid(_bdot(xc, wa_ref[...]) + ba_ref[...])
    i = _sigmoid(_bdot(xc, wx_ref[...]) + bx_ref[...])
    log_a = -LRU_C * r * _softplus(-lam_ref[...])
    pos = ti * tt + lax.broadcasted_iota(jnp.int32, (tt, 1), 0)
    log_a = jnp.where(pos < t_valid, log_a, 0.0)
    a = jnp.exp(log_a)
    b2 = -jnp.tanh(log_a) * (a * a + 1.0)
    pos2 = b2 > 0.0
    b = jnp.where(pos2, b2 * lax.rsqrt(jnp.where(pos2, b2, 1.0)), 0.0) * (i * xc)
    g = tt // 8
    a = a.reshape(g, 8, MIXW)
    b = b.reshape(g, 8, MIXW)
    sub = lax.broadcasted_iota(jnp.int32, (g, 8, 1), 1)
    for s in (1, 2, 4):
        keep = sub >= s
        b = a * jnp.where(keep, pltpu.roll(b, s, 1), 0.0) + b
        a = a * jnp.where(keep, pltpu.roll(a, s, 1), 1.0)
    a = a.reshape(tt, MIXW)
    b = b.reshape(tt, MIXW)
    halves = [slice(0, 128), slice(128, MIXW)]
    for k, lanes in enumerate(halves):
        sa[k] = a[:, lanes]
        sb[k] = b[:, lanes]
    ag =jnp.concatenate([sa[k, pl.ds(7, g, stride=8), :] for k in range(2)], axis=1)
    bg = jnp.concatenate([sb[k, pl.ds(7, g, stride=8), :] for k in range(2)], axis=1)
    if g % 8 == 0:
        s = 1
        while s < g:
            bg = ag * _shift_rows(bg, s, 0.0) + bg
            ag = ag * _shift_rows(ag, s, 1.0)
            s *= 2
        h_end = ag * hcar[...] + bg
        h_in = _shift_rows(h_end, 1, hcar[...])
        hcar[...] = h_end[g - 1:g, :]
    else:
        states = [hcar[...]]
        for j in range(g):
            states.append(ag[j:j + 1, :] * states[-1] + bg[j:j + 1, :])
        h_in = jnp.concatenate(states[:g], axis=0)
        hcar[...] = states[g]
    for r in range(8):
        rows = pl.ds(r, g, stride=8)
        for k, lanes in enumerate(halves):
            sh[k, rows, :] = sa[k, rows, :] * h_in[:, lanes] + sb[k, rows, :]
    h = jnp.concatenate([sh[0], sh[1]], axis=1)
    y_ref[...] = (h * _gelu_tanh(cb_ref[:, MIXW:2 * MIXW])).astype(BF16)

    last = t_valid - 1

    @pl.when(ti == last // tt)
    def _():
        r0 = last % tt
        hl_ref[...] = h[r0:r0 + 1, :]
        nbuf_ref[...] = xext[pl.ds(8 + r0 - (CONV_W - 2), CONV_W - 1), :]


def _lru_call(cb, buf8, h0, lw, *, nb, tt, t_valid):
    b, lp, _ = cb.shape
    consts = [lw["lru_cw"], lw["lru_cb"], lw["lru_wa"], lw["lru_ba"], lw["lru_wx"], lw["lru_bx"],
              lw["lru_lam"]]
    body = functools.partial(_lru_kernel, nb=nb, tt=tt, t_valid=t_valid)
    per_b = lambda s: pl.BlockSpec((nb,) + s, lambda i, j: (i, 0, 0))
    return pl.pallas_call(
        body, grid=(b // nb, lp // tt),
        in_specs=[pl.BlockSpec((nb, tt, B_W), lambda i, j: (i, j, 0)), per_b((8, MIXW)),
                  per_b((1, MIXW))] + [_const_spec(c.shape) for c in consts],
        out_specs=[pl.BlockSpec((nb, tt, MIXW), lambda i, j: (i, j, 0)),
                   per_b((CONV_W - 1, MIXW)), per_b((1, MIXW))],
        out_shape=[jax.ShapeDtypeStruct((b, lp, MIXW), BF16),
                   jax.ShapeDtypeStruct((b, CONV_W - 1, MIXW), F32),
                   jax.ShapeDtypeStruct((b, 1, MIXW), F32)],
        scratch_shapes=[pltpu.VMEM((nb, tt + 8, MIXW), F32), pltpu.VMEM((nb, 1, MIXW), F32)]
        + [pltpu.VMEM((nb, 2, tt, 128), F32)] * 3,
        compiler_params=pltpu.CompilerParams(dimension_semantics=("parallel", "arbitrary"),
                                             vmem_limit_bytes=VMEM_LIMIT),
        name="rglru")(cb, buf8, h0, *consts)


def _gdn_kernel(cc_ref, buf_ref, s0_ref, cw_ref, alog_ref, dtb_ref, gon_ref,
                y_ref, nbuf_ref, so_ref, xext, q_s, k_s, v_s, b_s, g_s, o_s, u_s, a_s, w_s, e_s, l_s,
                m_s, h_s, n_s, st_ref, *, nb, n_tiles, tt, c, t_valid):
    ti = pl.program_id(1)
    qkw = 3 * MIXW
    bd = _block_diag_mask()
    bd01 = jnp.where(bd, 1.0, 0.0).astype(BF16)
    pos = ti * tt + lax.broadcasted_iota(jnp.int32, (tt, 1), 0)
    valid = pos < t_valid
    src = lax.broadcasted_iota(jnp.int32, (128, 2 * MIXW), 0)
    dst = lax.broadcasted_iota(jnp.int32, (128, 2 * MIXW), 1)
    spread01 = jnp.where(src == (dst >> 6), 1.0, 0.0).astype(BF16)

    for s in range(nb):
        xe = xext.at[s]
        sr = slice(s * tt, (s + 1) * tt)

        @pl.when(ti == 0)
        def _():
            xe[0:8, :] = buf_ref[s]
            _load_state(st_ref, s0_ref, s)

        @pl.when(ti > 0)
        def _():
            xe[0:8, :] = xe[tt:tt + 8, :]

        xe[8:tt + 8, :] = cc_ref[s, :, 0:qkw]
        xs = _silu(_causal_conv(xe, cw_ref, tt))
        q = xs[:, 0:MIXW]
        k = xs[:, MIXW:2 * MIXW]
        q_s[sr, :] = q * lax.rsqrt(_xdot_r01(q * q, bd01) + EPS) * (HD ** -0.5)
        k_s[sr, :] = k * lax.rsqrt(_xdot_r01(k * k, bd01) + EPS)
        v_s[sr, :] = xs[:, 2 * MIXW:3 * MIXW]
        ab = _xdot3_r01(cc_ref[s, :, qkw + MIXW:qkw + MIXW + 128], spread01)
        a_in = ab[:, 0:MIXW]
        b_in = ab[:, MIXW:2 * MIXW]
        b_s[sr, :] = jnp.where(valid, _sigmoid(b_in), 0.0)
        g_s[sr, :] = jnp.where(valid, -jnp.exp(alog_ref[...]) * _softplus(a_in + dtb_ref[...]), 0.0)

    tril4, strict4, eye4 = _head_masks(c)
    tril01 = _tril01(c)
    n_chunks = tt // c

    ch = range(nb * n_chunks)
    rs = [slice(ci * c, (ci + 1) * c) for ci in ch]
    gc = [_xdot_l01(tril01, g_s[r, :]) for r in rs]
    gct = [jnp.sum(jnp.where(eye4, x, 0.0), axis=0, keepdims=True) for x in gc]
    decay = [jnp.where(tril4, jnp.exp(jnp.where(tril4, gc[i] - gct[i], 0.0)), 0.0) for i in ch]
    kc = [k_s[r, :] for r in rs]
    beta = [b_s[r, :] for r in rs]
    kb = [kc[i] * beta[i] for i in ch]
    sc = [_mm_nt(jnp.concatenate([kb[i], q_s[rs[i], :]], axis=0), _embed(kc[i], bd01)) for i in ch]
    tinv = _inv_unit_lower([jnp.where(strict4, sc[i][0:c] * decay[i], 0.0) for i in ch], eye4, bd01)
    attn = [jnp.where(tril4, sc[i][c:2 * c] * decay[i], 0.0).astype(BF16) for i in ch]
    egc = [jnp.exp(x) for x in gc]
    uw = [_mm(tinv[i], jnp.concatenate([_embed(v_s[rs[i], :] * beta[i], bd01),
                                        _embed(kb[i] * egc[i], bd01)], axis=1)) for i in ch]
    glast = [x[c - 1:c, :] for x in gc]
    kd = [(kc[i] * jnp.exp(glast[i] - gc[i])).astype(BF16) for i in ch]
    an = [lax.dot_general(kd[i], jnp.concatenate([uw[i][:, MIXW:], uw[i][:, :MIXW]], axis=1).astype(BF16),
                          (((0,), (0,)), ((), ())), preferred_element_type=F32) for i in ch]
    for i in ch:
        u_s[rs[i], :] = uw[i][:, 0:MIXW]
        w_s[rs[i], :] = uw[i][:, MIXW:2 * MIXW].astype(BF16)
        e_s[rs[i], :] = (q_s[rs[i], :] * egc[i]).astype(BF16)
        a_s[rs[i], :] = attn[i]
        m_s[i] = jnp.where(bd, an[i][:, 0:MIXW], 0.0).astype(BF16)
        n_s[i] = jnp.where(bd, an[i][:, MIXW:2 * MIXW], 0.0)
        l_s[i:i + 1, :] = jnp.exp(glast[i])

    for ci in range(n_chunks):
        for s in range(nb):
            i = s * n_chunks + ci
            s_bd = st_ref[s]
            s16 = s_bd.astype(BF16)
            h_s[i] = s16
            st_ref[s] = s_bd * l_s[i:i + 1, :] + n_s[i] - jnp.dot(m_s[i], s16, preferred_element_type=F32)

    r = [jnp.dot(jnp.concatenate([w_s[rs[i], :], e_s[rs[i], :]], axis=0), h_s[i], preferred_element_type=F32)
         for i in ch]
    v_new = [u_s[rs[i], :] - r[i][0:c] for i in ch]
    ov = [jnp.dot(a_s[rs[i], :], _embed(v_new[i], bd01), preferred_element_type=F32) for i in ch]
    for i in ch:
        o_s[rs[i], :] = r[i][c:2 * c] + ov[i]

    last = t_valid - 1
    for s in range(nb):
        o = o_s[s * tt:(s + 1) * tt, :]
        ms = _xdot_r01(o * o, bd01) * (1.0 / HD)
        z = cc_ref[s, :, qkw:qkw + MIXW]
        y_ref[s] = (o * lax.rsqrt(ms + EPS) * gon_ref[...] * _silu(z)).astype(BF16)

        @pl.when(ti == last // tt)
        def _():
            nbuf_ref[s] = xext[s, pl.ds(8 + last % tt - (CONV_W - 2), CONV_W - 1), :]

        @pl.when(ti == n_tiles - 1)
        def _():
            _store_state(so_ref, st_ref, s)


def _gdn_call(cc, buf8, s0, lw, *, nb, tt, c, t_valid):
    b, lp, _ = cc.shape
    qkw = 3 * MIXW
    consts = [lw["gdn_cw"], lw["gdn_alog"], lw["gdn_dtb"], lw["gdn_gon"]]
    body = functools.partial(_gdn_kernel, nb=nb, n_tiles=lp // tt, tt=tt, c=c, t_valid=t_valid)
    per_b = lambda s: pl.BlockSpec((nb,) + s, lambda i, j: (i,) + (0,) * len(s))
    n_ch = nb * (tt // c)
    return pl.pallas_call(
        body, grid=(b // nb, lp // tt),
        in_specs=[pl.BlockSpec((nb, tt, C_W), lambda i, j: (i, j, 0)), per_b((8, qkw)),
                  per_b((HEADS, HD, HD))] + [_const_spec(x.shape) for x in consts],
        out_specs=[pl.BlockSpec((nb, tt, MIXW), lambda i, j: (i, j, 0)),
                   per_b((CONV_W - 1, qkw)), per_b((HEADS, HD, HD))],
        out_shape=[jax.ShapeDtypeStruct((b, lp, MIXW), BF16),
                   jax.ShapeDtypeStruct((b, CONV_W - 1, qkw), F32),
                   jax.ShapeDtypeStruct((b, HEADS, HD, HD), F32)],
        scratch_shapes=[pltpu.VMEM((nb, tt + 8, qkw), F32)] + [pltpu.VMEM((nb * tt, MIXW), F32)] * 7
        + [pltpu.VMEM((nb * tt, MIXW), BF16)] * 3 + [pltpu.VMEM((-(-n_ch // 8) * 8, MIXW), F32),
                                                     pltpu.VMEM((n_ch, MIXW, MIXW), BF16),
                                                     pltpu.VMEM((n_ch, MIXW, MIXW), BF16),
                                                     pltpu.VMEM((n_ch, MIXW, MIXW), F32),
                                                     pltpu.VMEM((nb, MIXW, MIXW), F32)],
        compiler_params=pltpu.CompilerParams(dimension_semantics=("parallel", "arbitrary"),
                                             vmem_limit_bytes=VMEM_LIMIT),
        name="gated_delta")(cc, buf8, s0, *consts)


def _rwkv_kernel(cd_ref, prev_ref, s0_ref, mu_ref, w0_ref, wb_ref, a0_ref, ab_ref, gb_ref, kk_ref,
                 ka_ref, rk_ref, lnw_ref, lnb_ref,
                 y_ref, shift_ref, so_ref, xext, r_s, k_s, v_s, n_s, b_s, w_s, o_s, xv_s,
                 t_s, a_s, lhs_s, p_s, g_s, h_s, d_s, rk_s, gt_s, st_ref, *, nb, n_tiles, tt, c, t_valid):
    ti = pl.program_id(1)
    pos = ti * tt + lax.broadcasted_iota(jnp.int32, (tt, 1), 0)
    valid = pos < t_valid
    bd = _block_diag_mask()
    bd01 = jnp.where(bd, 1.0, 0.0).astype(BF16)

    for s in range(nb):
        xe = xext.at[s]
        sr = slice(s * tt, (s + 1) * tt)

        @pl.when(ti == 0)
        def _():
            xe[0:8, :] = prev_ref[s]
            _load_state(st_ref, s0_ref, s)

        @pl.when(ti > 0)
        def _():
            xe[0:8, :] = xe[tt:tt + 8, :]

        x = cd_ref[s]
        xe[8:tt + 8, :] = x
        xm = x + (xe[pl.ds(7, tt), :] - x) * mu_ref[...]
        r = xm[:, 0:MIXW]
        k = xm[:, MIXW:2 * MIXW]
        lo_blk = xm[:, 3 * MIXW:3 * MIXW + 128]
        logw = RWKV_DECAY_SCALE * _sigmoid(w0_ref[...] + _bdot(jnp.tanh(lo_blk), wb_ref[...]))
        a = _sigmoid(a0_ref[...] + _bdot(lo_blk, ab_ref[...]))
        kkv = k * kk_ref[...]
        kkn = kkv * lax.rsqrt(_xdot_r01(kkv * kkv, bd01) + EPS)
        kkn = jnp.where(valid, kkn, 0.0)
        kmod = k * (1.0 + (a - 1.0) * ka_ref[...])
        r_s[sr, :] = r
        k_s[sr, :] = jnp.where(valid, kmod, 0.0)
        v_s[sr, :] = xm[:, 2 * MIXW:3 * MIXW]
        n_s[sr, :] = kkn
        b_s[sr, :] = kkn * a
        w_s[sr, :] = jnp.where(valid, logw, 0.0)
        rk_s[sr, :] = _xdot_r01(r * kmod * rk_ref[...], bd01)
        gt_s[sr, :] = _bdot(_sigmoid(xm[:, 3 * MIXW + 128:3 * MIXW + 256]), gb_ref[...])

    tril4, strict4, eye4 = _head_masks(c)
    tril01 = _tril01(c)

    n_chunks = tt // c

    ch = range(nb * n_chunks)
    rs = [slice(ci * c, (ci + 1) * c) for ci in ch]
    lw = [w_s[r, :] for r in rs]
    cum = [_xdot_l01(tril01, x) for x in lw]
    ecum = [jnp.exp(x) for x in cum]
    pin = [jnp.exp(-x) for x in cum]
    pc = [x[c - 1:c, :] for x in ecum]
    kinv = [k_s[rs[i], :] * pin[i] for i in ch]
    binv = [b_s[rs[i], :] * pin[i] for i in ch]
    nd = [n_s[rs[i], :] * jnp.exp(cum[i] - lw[i]) for i in ch]
    lhs = [jnp.concatenate([nd[i], r_s[rs[i], :] * ecum[i]], axis=0).astype(BF16) for i in ch]
    sk = [_mm_nt(lhs[i], _embed(kinv[i], bd01)) for i in ch]
    sb = [_mm_nt(lhs[i], _embed(binv[i], bd01)) for i in ch]
    tinv = _inv_unit_lower([jnp.where(strict4, sb[i][0:c], 0.0) for i in ch], eye4, bd01)
    av = [_mm(jnp.concatenate([jnp.where(strict4, sk[i][0:c], 0.0), jnp.where(tril4, sk[i][c:2 * c], 0.0)],
                              axis=0), _embed(v_s[rs[i], :], bd01)) for i in ch]
    tn = [_mm(tinv[i], jnp.concatenate([_embed(nd[i], bd01), _embed(av[i][0:c], bd01)], axis=1)) for i in ch]
    bp = [(binv[i] * pc[i]).astype(BF16) for i in ch]
    gh = [lax.dot_general(tn[i].astype(BF16), bp[i], (((0,), (0,)), ((), ())), preferred_element_type=F32)
          for i in ch]
    vk = [_bdot_tn(v_s[rs[i], :], kinv[i] * pc[i]) for i in ch]
    for i in ch:
        lhs_s[2 * i * c:2 * (i + 1) * c, :] = lhs[i]
        t_s[rs[i], :] = tinv[i].astype(BF16)
        a_s[rs[i], :] = jnp.where(tril4, sb[i][c:2 * c], 0.0).astype(BF16)
        xv_s[rs[i], :] = av[i][0:c]
        o_s[rs[i], :] = av[i][c:2 * c]
        p_s[i:i + 1, :] = pc[i]
        g_s[i] = jnp.where(bd, gh[i][0:MIXW], 0.0).astype(BF16)
        d_s[i] = jnp.where(bd, vk[i] - gh[i][MIXW:2 * MIXW], 0.0)

    for ci in range(n_chunks):
        for s in range(nb):
            i = s * n_chunks + ci
            st = st_ref[s]
            s16 = st.astype(BF16)
            h_s[i] = s16
            st_ref[s] = st * p_s[i:i + 1, :] + d_s[i] - jnp.dot(s16, g_s[i], preferred_element_type=F32)

    r2 = [lax.dot_general(lhs_s[2 * i * c:2 * (i + 1) * c, :], h_s[i], (((1,), (1,)), ((), ())),
                          preferred_element_type=F32) for i in ch]
    u = [jnp.dot(t_s[rs[i], :], _embed(r2[i][0:c] + xv_s[rs[i], :], bd01), preferred_element_type=F32)
         for i in ch]
    au = [jnp.dot(a_s[rs[i], :], _embed(u[i], bd01), preferred_element_type=F32) for i in ch]
    for i in ch:
        o_s[rs[i], :] = r2[i][c:2 * c] + o_s[rs[i], :] - au[i]

    last = t_valid - 1
    for s in range(nb):
        sr = slice(s * tt, (s + 1) * tt)
        o = o_s[sr, :]
        mean = _xdot_r01(o, bd01) * (1.0 / HD)
        d = o - mean
        var = _xdot_r01(d * d, bd01) * (1.0 / HD)
        o = d * lax.rsqrt(var + RWKV_GN_EPS) * lnw_ref[...] + lnb_ref[...]
        o = o + rk_s[sr, :] * v_s[sr, :]
        y_ref[s] = (o * gt_s[sr, :]).astype(BF16)

        @pl.when(ti == last // tt)
        def _():
            shift_ref[s] = xext[s, pl.ds(8 + last % tt, 1), :]

        @pl.when(ti == n_tiles - 1)
        def _():
            _store_state(so_ref, st_ref, s)


def _rwkv_call(cd, prev8, s0, lw, *, nb, tt, c, t_valid):
    b, lp, _ = cd.shape
    consts = [lw["rwkv_mu"], lw["rwkv_w0"], lw["rwkv_wb"], lw["rwkv_a0"], lw["rwkv_ab"], lw["rwkv_gb"],
              lw["rwkv_kk"], lw["rwkv_ka"], lw["rwkv_rk"], lw["rwkv_lnw"], lw["rwkv_lnb"]]
    body = functools.partial(_rwkv_kernel, nb=nb, n_tiles=lp // tt, tt=tt, c=c, t_valid=t_valid)
    per_b = lambda s: pl.BlockSpec((nb,) + s, lambda i, j: (i,) + (0,) * len(s))
    n_ch = nb * (tt // c)
    return pl.pallas_call(
        body, grid=(b // nb, lp // tt),
        in_specs=[pl.BlockSpec((nb, tt, D_W), lambda i, j: (i, j, 0)), per_b((8, D_W)),
                  per_b((HEADS, HD, HD))] + [_const_spec(x.shape) for x in consts],
        out_specs=[pl.BlockSpec((nb, tt, MIXW), lambda i, j: (i, j, 0)),
                   per_b((1, D_W)), per_b((HEADS, HD, HD))],
        out_shape=[jax.ShapeDtypeStruct((b, lp, MIXW), BF16),
                   jax.ShapeDtypeStruct((b, 1, D_W), F32),
                   jax.ShapeDtypeStruct((b, HEADS, HD, HD), F32)],
        scratch_shapes=[pltpu.VMEM((nb, tt + 8, D_W), F32)] + [pltpu.VMEM((nb * tt, MIXW), F32)] * 8
        + [pltpu.VMEM((nb * tt, MIXW), BF16)] * 2 + [pltpu.VMEM((2 * nb * tt, MIXW), BF16),
                                                     pltpu.VMEM((-(-n_ch // 8) * 8, MIXW), F32),
                                                     pltpu.VMEM((n_ch, MIXW, MIXW), BF16),
                                                     pltpu.VMEM((n_ch, MIXW, MIXW), BF16),
                                                     pltpu.VMEM((n_ch, MIXW, MIXW), F32)]
        + [pltpu.VMEM((nb * tt, MIXW), F32)] * 2 + [pltpu.VMEM((nb, MIXW, MIXW), F32)],
        compiler_params=pltpu.CompilerParams(dimension_semantics=("parallel", "arbitrary"),
                                             vmem_limit_bytes=VMEM_LIMIT),
        name="rwkv7")(cd, prev8, s0, *consts)


def _pad_cols(w, width):
    return jnp.pad(w, ((0, 0), (0, width - w.shape[1])))


def _mla_head_cols(w, n_heads, per_head, pieces):
    blocks = []
    for h in range(n_heads):
        cols = [sign * w[:, h * per_head + lo:h * per_head + hi] for lo, hi, sign in pieces]
        blocks.append(_pad_cols(jnp.concatenate(cols, axis=1), MLA_PAD))
    return jnp.concatenate(blocks, axis=1)


def _block_diag_heads(w):
    eye = jnp.eye(HEADS, dtype=w.dtype)[:, None, :, None]
    return (w[:, :, None, :] * eye).reshape(MIXW, MIXW)


def _prep_layer_weights(p, l):
    half = MLA_ROPE // 2
    row = lambda v: v.reshape(1, -1).astype(F32)
    lw = {}
    for tag in ("1", "2"):
        lw["ffn" + tag] = dict(g_ffn=row(p["norm_ffn" + tag][l]), layer=l, w1=p["ffn%s_w1_bf16" % tag],
                               w2=p["ffn%s_w2_bf16" % tag])
    w_in = p["w_in"][l]
    o = 0
    cq = w_in[:, o:o + 256]; o += 256
    ckv = w_in[:, o:o + 128]; o += 128
    kpe = w_in[:, o:o + MLA_ROPE]; o += MLA_ROPE
    xb_gb = w_in[:, o:o + 512]; o += 512
    qkvz = w_in[:, o:o + 1024]; o += 1024
    a_in = w_in[:, o:o + HEADS]; o += HEADS
    b_in = w_in[:, o:o + HEADS]; o += HEADS
    rw = w_in[:, o:o + 1024]
    kpe_rot = jnp.concatenate([-kpe[:, half:], kpe[:, :half]], axis=1)
    w_in_p = jnp.concatenate([
        cq, ckv, _pad_cols(jnp.concatenate([kpe, kpe_rot], axis=1), 128),
        xb_gb,
        qkvz, _pad_cols(jnp.concatenate([a_in, b_in], axis=1), 128),
        rw], axis=1)
    lw["ffn1"]["g_mix"] = row(p["norm_mix"][l])
    lw["ffn1"]["w_in"] = w_in_p.astype(BF16)
    w_out = p["w_out"][l]
    wo_a = jnp.concatenate([jnp.pad(w_out[h * HD:(h + 1) * HD], ((0, MLA_PAD - HD), (0, 0)))
                            for h in range(HEADS)], axis=0)
    lw["w_out"] = [wo_a.astype(BF16)] + [w_out[MIXW * i:MIXW * (i + 1)].astype(BF16) for i in (1, 2, 3)]
    wuq = p["mla_w_uq"][l]
    wq = _mla_head_cols(wuq, HEADS, MLA_QK, [(MLA_NOPE, MLA_QK, 1.0), (0, MLA_NOPE, 1.0)])
    wq_rot = _mla_head_cols(wuq, HEADS, MLA_QK, [(MLA_NOPE + half, MLA_QK, -1.0),
                                                (MLA_NOPE, MLA_NOPE + half, 1.0)])
    lw["mla_wq"] = jnp.concatenate([wq, wq_rot], axis=1).astype(BF16)
    wukv = p["mla_w_ukv"][l]
    zero_rope = jnp.zeros((wukv.shape[0], MLA_ROPE), wukv.dtype)
    wk = jnp.concatenate([_pad_cols(jnp.concatenate([zero_rope, wukv[:, h * 128:h * 128 + MLA_NOPE]], axis=1),
                                    MLA_PAD) for h in range(HEADS)], axis=1)
    wv = jnp.concatenate([_pad_cols(wukv[:, h * 128 + MLA_NOPE:(h + 1) * 128], MLA_PAD)
                          for h in range(HEADS)], axis=1)
    lw["mla_wkv"] = jnp.concatenate([wk, wv], axis=1).astype(BF16)
    lw["mla_wv"] = wv.astype(BF16)
    lw["mla_wkt"] = jnp.concatenate([wukv[:, h * 128:h * 128 + MLA_NOPE] for h in range(HEADS)],
                                    axis=1).T.astype(BF16)
    lw["mla_gqa"] = row(p["mla_q_a_norm"][l])
    lw["mla_gkva"] = row(p["mla_kv_a_norm"][l])
    perm = lambda g: row(_pad_cols(jnp.concatenate([g[MLA_NOPE:], g[:MLA_NOPE]])[None], MLA_PAD))
    lw["mla_gq"] = perm(p["mla_q_norm"][l])
    lw["mla_gk"] = perm(p["mla_k_norm"][l])
    lw["lru_cw"] = p["lru_conv_w"][l]
    lw["lru_cb"] = row(p["lru_conv_b"][l])
    lw["lru_wa"] = _block_diag_heads(p["lru_wa"][l]).astype(BF16)
    lw["lru_ba"] = row(p["lru_ba"][l])
    lw["lru_wx"] = _block_diag_heads(p["lru_wx"][l]).astype(BF16)
    lw["lru_bx"] = row(p["lru_bx"][l])
    lw["lru_lam"] = row(p["lru_lambda"][l])
    lw["gdn_cw"] = p["gdn_conv_w"][l]
    lw["gdn_alog"] = row(jnp.repeat(p["gdn_a_log"][l], HD))
    lw["gdn_dtb"] = row(jnp.repeat(p["gdn_dt_bias"][l], HD))
    lw["gdn_gon"] = row(jnp.tile(p["gdn_o_norm"][l], HEADS))
    z64 = jnp.zeros((64, MIXW), F32)
    lw["rwkv_mu"] = row(p["rwkv_mu"][l])
    lw["rwkv_w0"] = row(p["rwkv_w0"][l])
    lw["rwkv_wb"] = jnp.concatenate([p["rwkv_w_b"][l], z64], axis=0).astype(BF16)
    lw["rwkv_a0"] = row(p["rwkv_a0"][l])
    lw["rwkv_ab"] = jnp.concatenate([z64, p["rwkv_a_b"][l]], axis=0).astype(BF16)
    lw["rwkv_gb"] = p["rwkv_g_b"][l].astype(BF16)
    lw["rwkv_kk"] = row(p["rwkv_k_k"][l])
    lw["rwkv_ka"] = row(p["rwkv_k_a"][l])
    lw["rwkv_rk"] = row(p["rwkv_r_k"][l])
    lw["rwkv_lnw"] = row(p["rwkv_ln_w"][l])
    lw["rwkv_lnb"] = row(p["rwkv_ln_b"][l])
    return lw


def _rope_tables(pos):
    inv = ROPE_THETA ** (-jnp.arange(0, MLA_ROPE, 2, dtype=F32) / MLA_ROPE)
    ang = pos.astype(F32)[:, None] * inv[None, :]
    cos, sin = jnp.cos(ang), jnp.sin(ang)
    n = pos.shape[0]
    cos_t = jnp.concatenate([cos, cos, jnp.ones((n, MLA_PAD - MLA_ROPE), F32)], axis=1)
    sin_t = jnp.concatenate([sin, sin, jnp.zeros((n, MLA_PAD - MLA_ROPE), F32)], axis=1)
    return cos_t, sin_t


def _hist8(rows):
    return jnp.pad(rows.astype(F32), ((0, 0), (8 - rows.shape[1], 0), (0, 0)))


def _run_group(x, lws, final_g, states, cfg):
    b, lp, d = x.shape
    n = b * lp
    t_valid, tm, tt, c = cfg["t_valid"], cfg["tm"], cfg["tt"], cfg["c"]
    cos_t, sin_t = cfg["rope"]
    xf = x.reshape(n, d)
    new_states = []
    depth = len(lws)
    for l in range(depth):
        lw = lws[l]
        st = states[l]
        xf, q, k, v, ckv, kpe, cb, cc, cd = _token_call(xf, lw["ffn1"], tm=tm,
                                                        post=(cos_t, sin_t, lw, (b, lp, t_valid)))
        r3 = lambda a: a.reshape(b, lp, a.shape[-1])
        if st["mla"] is None:
            ya = _mla_prompt_call(r3(q), r3(k), r3(v), tq=cfg["tq"], tk=cfg["tk"])
        else:
            cache_ckv, cache_kpe_t = st["mla"]
            ya = _mla_sample_call(r3(q), r3(k), r3(v), cache_ckv, cache_kpe_t, l, lw)
        yb, lru_conv, lru_h = _lru_call(r3(cb), st["lru_conv"], st["lru_h"], lw, nb=cfg["nb"], tt=tt,
                                        t_valid=t_valid)
        yc, gdn_conv, gdn_s = _gdn_call(r3(cc), st["gdn_conv"], st["gdn_s"], lw, nb=cfg["nb"], tt=tt, c=c,
                                        t_valid=t_valid)
        yd, shift, rwkv_s = _rwkv_call(r3(cd), st["rwkv_shift"], st["rwkv_s"], lw, nb=cfg["nb"], tt=tt, c=c,
                                       t_valid=t_valid)
        ys = [ya.reshape(n, -1), yb.reshape(n, -1), yc.reshape(n, -1), yd.reshape(n, -1)]
        xf = _token_call(xf, lw["ffn2"], pre=(ys, lw["w_out"]), tm=tm,
                         final_g=final_g if l == depth - 1 else None)[0]
        new_states.append((ckv.reshape(b, t_valid, -1), kpe.reshape(b, t_valid, -1), lru_conv, lru_h[:, 0],
                           gdn_conv, gdn_s, shift[:, 0], rwkv_s))
    stacked = [jnp.stack(t) for t in zip(*new_states)]
    return xf.reshape(b, lp, d), stacked


def _group_config(b, t_valid):
    divisor = lambda target: max(d for d in range(1, target + 1) if b % d == 0)
    if t_valid <= CHUNK:
        lp = t_valid
        return dict(lp=lp, t_valid=t_valid, tm=b * lp if b * lp <= 512 else lp, tt=lp, c=lp, tq=lp, tk=1024,
                    nb=divisor(8))
    lp = -(-t_valid // 384) * 384
    return dict(lp=lp, t_valid=t_valid, tm=384, tt=384, c=CHUNK, tq=384, tk=384, nb=divisor(2))


def kernel(x_prompt, x_sample, cache_mla_ckv, cache_mla_kpe, state_lru_conv, state_lru_h, state_gdn_conv, state_gdn_s, state_rwkv_shift, state_rwkv_s, meta_tokens, norm_ffn1, ffn1_w1, ffn1_w2, norm_mix, w_in, mla_q_a_norm, mla_w_uq, mla_kv_a_norm, mla_w_ukv, mla_q_norm, mla_k_norm, lru_conv_w, lru_conv_b, lru_wa, lru_ba, lru_wx, lru_bx, lru_lambda, gdn_conv_w, gdn_a_log, gdn_dt_bias, gdn_o_norm, rwkv_mu, rwkv_w0, rwkv_w_b, rwkv_a0, rwkv_a_b, rwkv_g_b, rwkv_k_k, rwkv_k_a, rwkv_r_k, rwkv_ln_w, rwkv_ln_b, w_out, norm_ffn2, ffn2_w1, ffn2_w2, final_norm):
    p = dict(norm_ffn1=norm_ffn1, ffn1_w1=ffn1_w1, ffn1_w2=ffn1_w2, norm_mix=norm_mix, w_in=w_in,
             mla_q_a_norm=mla_q_a_norm, mla_w_uq=mla_w_uq, mla_kv_a_norm=mla_kv_a_norm,
             mla_w_ukv=mla_w_ukv, mla_q_norm=mla_q_norm, mla_k_norm=mla_k_norm,
             lru_conv_w=lru_conv_w, lru_conv_b=lru_conv_b, lru_wa=lru_wa, lru_ba=lru_ba,
             lru_wx=lru_wx, lru_bx=lru_bx, lru_lambda=lru_lambda, gdn_conv_w=gdn_conv_w,
             gdn_a_log=gdn_a_log, gdn_dt_bias=gdn_dt_bias, gdn_o_norm=gdn_o_norm, rwkv_mu=rwkv_mu,
             rwkv_w0=rwkv_w0, rwkv_w_b=rwkv_w_b, rwkv_a0=rwkv_a0, rwkv_a_b=rwkv_a_b, rwkv_g_b=rwkv_g_b,
             rwkv_k_k=rwkv_k_k, rwkv_k_a=rwkv_k_a, rwkv_r_k=rwkv_r_k, rwkv_ln_w=rwkv_ln_w,
             rwkv_ln_b=rwkv_ln_b, w_out=w_out, norm_ffn2=norm_ffn2, ffn2_w1=ffn2_w1, ffn2_w2=ffn2_w2)
    depth = w_in.shape[0]
    for name in ("ffn1_w1", "ffn1_w2", "ffn2_w1", "ffn2_w2"):
        p[name + "_bf16"] = p[name].astype(BF16)
    lws = [_prep_layer_weights(p, l) for l in range(depth)]
    final_g = final_norm.reshape(1, -1).astype(F32)
    d = x_prompt.shape[-1]

    bp, seq, _ = x_prompt.shape
    tp = N_META + seq
    cfg = _group_config(bp, tp)
    lp = cfg["lp"]
    x0 = jnp.concatenate([jnp.broadcast_to(meta_tokens.astype(F32)[None], (bp, N_META, d)), x_prompt,
                          jnp.zeros((bp, lp - tp, d), F32)], axis=1)
    cfg["rope"] = tuple(jnp.tile(t, (bp, 1)) for t in _rope_tables(jnp.arange(lp)))
    zero = dict(mla=None, lru_conv=jnp.zeros((bp, 8, MIXW), F32), lru_h=jnp.zeros((bp, 1, MIXW), F32),
                gdn_conv=jnp.zeros((bp, 8, 3 * MIXW), F32), gdn_s=jnp.zeros((bp, HEADS, HD, HD), F32),
                rwkv_shift=jnp.zeros((bp, 8, D_W), F32), rwkv_s=jnp.zeros((bp, HEADS, HD, HD), F32))
    yp, p_new = _run_group(x0, lws, final_g, [zero] * depth, cfg)

    bs, ts, _ = x_sample.shape
    past = cache_mla_ckv.shape[2]
    cfg_s = _group_config(bs, ts)
    cfg_s["rope"] = tuple(jnp.tile(t, (bs, 1)) for t in _rope_tables(past + jnp.arange(ts)))
    cache_kpe_t = jnp.swapaxes(cache_mla_kpe, 2, 3)
    st_s = [dict(mla=(cache_mla_ckv, cache_kpe_t), lru_conv=_hist8(state_lru_conv[l]),
                 lru_h=state_lru_h[l][:, None].astype(F32), gdn_conv=_hist8(state_gdn_conv[l]),
                 gdn_s=state_gdn_s[l], rwkv_shift=_hist8(state_rwkv_shift[l][:, None]),
                 rwkv_s=state_rwkv_s[l]) for l in range(depth)]
    ys, s_new = _run_group(x_sample, lws, final_g, st_s, cfg_s)
    return (yp[:, N_META:tp], ys) + tuple(p_new) + tuple(s_new)
```

```python
import functools
import math

import jax
import jax.numpy as jnp
import numpy as np
from jax import lax
from jax.experimental import pallas as pl
from jax.experimental.pallas import tpu as pltpu

F32 = jnp.float32
BF16 = jnp.bfloat16

EPS = 1e-6
N_META = 16
CHUNK = 64
CONV_W = 4
HEADS = 4
HD = 64
MIXW = HEADS * HD
MLA_NOPE = 64
MLA_ROPE = 32
MLA_QK = MLA_NOPE + MLA_ROPE
MLA_V = 64
MLA_PAD = 128
ROPE_THETA = 10000.0
LRU_C = 8.0
RWKV_GN_EPS = 64e-5
RWKV_DECAY_SCALE = -0.606531

VMEM_LIMIT = 56 * 1024 * 1024
MXU_N = 256

A_W = 512
B_W = 512
C_W = 1536
D_W = 1024


def _bdot(a, b):
    return jnp.dot(a.astype(BF16), b.astype(BF16), preferred_element_type=F32)


def _bdot_tn(a, b):
    return lax.dot_general(a.astype(BF16), b.astype(BF16), (((0,), (0,)), ((), ())),
                           preferred_element_type=F32)


def _split3(x):
    x1 = x.astype(BF16)
    r1 = x - x1.astype(F32)
    x2 = r1.astype(BF16)
    r2 = r1 - x2.astype(F32)
    return x1, x2, r2.astype(BF16)


def _xdot_r01(x, m01):
    return jnp.dot(x.astype(BF16), m01, preferred_element_type=F32)


def _xdot_l01(m01, x):
    x1, x2, x3 = _split3(x)
    d = functools.partial(jnp.dot, preferred_element_type=F32)
    return d(m01, x1) + d(m01, x2) + d(m01, x3)


def _sigmoid(x):
    return 0.5 * jnp.tanh(0.5 * x) + 0.5


def _silu(x):
    return x * _sigmoid(x)


def _softplus(x):
    return jnp.maximum(x, 0.0) + jnp.log(1.0 + jnp.exp(-jnp.abs(x)))


def _gelu_tanh(x):
    return 0.5 * x * (1.0 + jnp.tanh(0.7978845608028654 * (x + 0.044715 * (x * x * x))))


def _rms_rows(x, g):
    return x * lax.rsqrt(jnp.mean(x * x, axis=-1, keepdims=True) + EPS) * g


def _block_diag_mask():
    r = lax.broadcasted_iota(jnp.int32, (MIXW, MIXW), 0) >> 6
    c = lax.broadcasted_iota(jnp.int32, (MIXW, MIXW), 1) >> 6
    return r == c


def _embed(x, bd01):
    c = x.shape[0]
    x = x.astype(BF16)
    if c < HD:
        x = jnp.concatenate([x, jnp.zeros((HD - c, MIXW), BF16)], axis=0)
    return jnp.concatenate([x] * HEADS, axis=0) * bd01


def _mm(a, b):
    return jnp.dot(a.astype(BF16), b, preferred_element_type=F32)


def _mm_nt(a, b):
    return lax.dot_general(a.astype(BF16), b, (((1,), (1,)), ((), ())), preferred_element_type=F32)


def _head_masks(c):
    row = lax.broadcasted_iota(jnp.int32, (c, MIXW), 0)
    col = lax.broadcasted_iota(jnp.int32, (c, MIXW), 1) & (HD - 1)
    return row >= col, row > col, row == col


def _tril01(c):
    r = lax.broadcasted_iota(jnp.int32, (c, c), 0)
    k = lax.broadcasted_iota(jnp.int32, (c, c), 1)
    return jnp.where(r >= k, 1.0, 0.0).astype(BF16)


def _inv_unit_lower(lms, eye4, bd01):
    c = lms[0].shape[0]
    levels = int(math.log2(c))
    eye = jnp.where(eye4, 1.0, 0.0)
    n = [-x for x in lms]
    t = [eye + x for x in n]
    if levels < 2:
        return t
    p = [_mm(x, _embed(x, bd01)) for x in n]
    for _ in range(levels - 2):
        r = [_mm(jnp.concatenate([pi, ti], axis=0), _embed(pi, bd01)) for pi, ti in zip(p, t)]
        p = [x[0:c] for x in r]
        t = [ti + x[c:2 * c] for ti, x in zip(t, r)]
    return [ti + _mm(ti, _embed(pi, bd01)) for pi, ti in zip(p, t)]


def _load_state(st_ref, s0_ref, s):
    st_ref[s] = jnp.zeros((MIXW, MIXW), F32)
    for h in range(HEADS):
        st_ref[s, h * HD:(h + 1) * HD, h * HD:(h + 1) * HD] = s0_ref[s, h].astype(F32)


def _store_state(so_ref, st_ref, s):
    for h in range(HEADS):
        so_ref[s, h] = st_ref[s, h * HD:(h + 1) * HD, h * HD:(h + 1) * HD]


def _shift_rows(x, s, fill):
    rolled = pltpu.roll(x, s, 0)
    row = lax.broadcasted_iota(jnp.int32, x.shape, 0)
    return jnp.where(row >= s, rolled, fill)


def _causal_conv(xext_ref, w_ref, tt):
    acc = None
    for j in range(CONV_W):
        term = xext_ref[pl.ds(8 - (CONV_W - 1) + j, tt), :] * w_ref[j:j + 1, :]
        acc = term if acc is None else acc + term
    return acc


def _const_spec(shape):
    nd = len(shape)
    return pl.BlockSpec(shape, lambda *_: (0,) * nd, pipeline_mode=pl.Buffered(1))


def _token_kernel(*refs, has_pre, has_post, has_final, d_ff, ff_chunk):
    it = iter(refs)
    x_ref = next(it)
    if has_pre:
        y_refs = [next(it) for _ in range(4)]
        wo_refs = [next(it) for _ in range(4)]
    gffn_ref, w1_ref, w2_ref = next(it), next(it), next(it)
    if has_post:
        gmix_ref, win_ref = next(it), next(it)
        prep_in = [next(it) for _ in range(8)]
    if has_final:
        gfin_ref = next(it)
    xo_ref = next(it)
    if has_post:
        prep_out = [next(it) for _ in range(5)]
        col_refs = [next(it) for _ in range(3)]

    x = x_ref[...]
    if has_pre:
        for y_ref, wo_ref in zip(y_refs, wo_refs):
            x = x + jnp.dot(y_ref[...], wo_ref[...], preferred_element_type=F32)
    h = _rms_rows(x, gffn_ref[...]).astype(BF16)
    acc = None
    for lo, hi in ((0, ff_chunk), (ff_chunk, d_ff)):
        gate = jnp.dot(h, w1_ref[:, lo:hi], preferred_element_type=F32)
        up = jnp.dot(h, w1_ref[:, d_ff + lo:d_ff + hi], preferred_element_type=F32)
        act = (_silu(gate) * up).astype(BF16)
        part = jnp.dot(act, w2_ref[lo:hi, :], preferred_element_type=F32)
        acc = part if acc is None else acc + part
    x = x + 0.5 * acc
    if has_final:
        xo_ref[...] = _rms_rows(x, gfin_ref[...])
    else:
        xo_ref[...] = x
    if has_post:
        hm = _rms_rows(x, gmix_ref[...]).astype(BF16)
        _mla_prep(jnp.dot(hm, win_ref[:, 0:A_W], preferred_element_type=F32), *prep_in, *prep_out)
        off = A_W
        for c_ref in col_refs:
            wd = c_ref.shape[-1]
            c_ref[...] = jnp.dot(hm, win_ref[:, off:off + wd], preferred_element_type=F32)
            off += wd


def _token_call(x, lw, *, pre=None, post=None, final_g=None, tm):
    n, d = x.shape
    d_ff = lw["w2"].shape[1]
    ff_chunk = -(-(d_ff // MXU_N) // 2) * MXU_N
    row = lambda w: pl.BlockSpec((tm, w), lambda i: (i, 0))
    args, specs = [x], [row(d)]
    if pre is not None:
        ys, wos = pre
        for y in ys:
            args.append(y)
            specs.append(row(y.shape[1]))
        for w in wos:
            args.append(w)
            specs.append(_const_spec(w.shape))
    args.append(lw["g_ffn"])
    specs.append(_const_spec(lw["g_ffn"].shape))
    layer = lw["layer"]
    for name in ("w1", "w2"):
        args.append(lw[name])
        specs.append(pl.BlockSpec((None,) + lw[name].shape[1:], lambda i: (layer, 0, 0),
                                  pipeline_mode=pl.Buffered(1)))
    if post is not None:
        cos_t, sin_t, mw, (b, lp, t_valid) = post
        for name in ("g_mix", "w_in"):
            args.append(lw[name])
            specs.append(_const_spec(lw[name].shape))
        args += [cos_t, sin_t]
        specs += [row(MLA_PAD), row(MLA_PAD)]
        for name in ("mla_gqa", "mla_wq", "mla_gkva", "mla_wkv", "mla_gq", "mla_gk"):
            args.append(mw[name])
            specs.append(_const_spec(mw[name].shape))
    if final_g is not None:
        args.append(final_g)
        specs.append(_const_spec(final_g.shape))
    out_shape = [jax.ShapeDtypeStruct((n, d), F32)]
    out_specs = [row(d)]
    if post is not None:
        hw = HEADS * MLA_PAD
        out_shape += [jax.ShapeDtypeStruct((n, hw), BF16)] * 3
        out_specs += [row(hw)] * 3
        for wd in (128, MLA_ROPE):
            if lp > t_valid:
                tps = lp // tm
                out_shape.append(jax.ShapeDtypeStruct((b, t_valid, wd), F32))
                out_specs.append(pl.BlockSpec((None, tm, wd), lambda i: (i // tps, i % tps, 0)))
            else:
                out_shape.append(jax.ShapeDtypeStruct((n, wd), F32))
                out_specs.append(row(wd))
        for wd in (B_W, C_W, D_W):
            out_shape.append(jax.ShapeDtypeStruct((n, wd), F32))
            out_specs.append(row(wd))
    body = functools.partial(_token_kernel, has_pre=pre is not None, has_post=post is not None,
                             has_final=final_g is not None, d_ff=d_ff, ff_chunk=ff_chunk)
    return pl.pallas_call(
        body, grid=(n // tm,), in_specs=specs, out_specs=out_specs, out_shape=out_shape,
        compiler_params=pltpu.CompilerParams(dimension_semantics=("parallel",),
                                             vmem_limit_bytes=VMEM_LIMIT),
        name="token_block")(*args)


def _mla_prep(ca, cos_ref, sin_ref, gqa_ref, wq_ref, gkva_ref, wkv_ref, gq_ref, gk_ref,
              q_ref, k_ref, v_ref, ckv_ref, kpe_ref):
    cs = cos_ref[...]
    sn = sin_ref[...]
    hw = HEADS * MLA_PAD
    cqn = _rms_rows(ca[:, 0:256], gqa_ref[...])
    qq = _bdot(cqn, wq_ref[...])
    scale = math.log2(math.e) / math.sqrt(MLA_QK)
    for h in range(HEADS):
        lo = h * MLA_PAD
        blk = qq[:, lo:lo + MLA_PAD] * cs + qq[:, hw + lo:hw + lo + MLA_PAD] * sn
        ms = jnp.sum(blk * blk, axis=-1, keepdims=True) * (1.0 / MLA_QK)
        q_ref[:, lo:lo + MLA_PAD] = (blk * lax.rsqrt(ms + EPS) * (gq_ref[...] * scale)).astype(BF16)
    ckv = _rms_rows(ca[:, 256:384], gkva_ref[...])
    ckv_ref[...] = ckv
    kblk = ca[:, 384:512]
    rope_lane = lax.broadcasted_iota(jnp.int32, kblk.shape, 1) < MLA_ROPE
    kpe = jnp.where(rope_lane, kblk * cs, 0.0) + pltpu.roll(kblk, MLA_PAD - MLA_ROPE, 1) * sn
    kpe_ref[...] = kpe[:, 0:MLA_ROPE]
    kv = _bdot(ckv, wkv_ref[...])
    ones_lane = lax.broadcasted_iota(jnp.int32, (kv.shape[0], MLA_PAD), 1) == MLA_V
    for h in range(HEADS):
        lo = h * MLA_PAD
        kb = kv[:, lo:lo + MLA_PAD] + kpe
        ms = jnp.sum(kb * kb, axis=-1, keepdims=True) * (1.0 / MLA_QK)
        k_ref[:, lo:lo + MLA_PAD] = (kb * lax.rsqrt(ms + EPS) * gk_ref[...]).astype(BF16)
        v_ref[:, lo:lo + MLA_PAD] = jnp.where(ones_lane, 1.0, kv[:, hw + lo:hw + lo + MLA_PAD]).astype(BF16)


def _softmax_step(carry, s, vblk):
    m, acc = carry
    m_new = jnp.maximum(m, jnp.max(s, axis=-1, keepdims=True))
    p = jnp.exp2(s - m_new)
    acc = jnp.exp2(m - m_new) * acc + jnp.dot(p.astype(BF16), vblk, preferred_element_type=F32)
    return m_new, acc


def _softmax_init(t):
    return jnp.full((t, 1), -1e30, F32), jnp.zeros((t, MLA_PAD), F32)


def _softmax_finish(acc):
    return (acc / acc[:, MLA_V:MLA_V + 1]).astype(BF16)


def _mla_prompt_kernel(q_ref, k_ref, v_ref, o_ref, *, tq, tk, n_kb):
    q0 = pl.program_id(1) * tq
    first_chunk = (q0 - N_META) >> 6
    last_chunk = (q0 + tq - 1 - N_META) >> 6
    n_full = (N_META + CHUNK * (first_chunk + 1)) // tk
    nkb = jnp.minimum((N_META + CHUNK * (last_chunk + 1)) // tk, n_kb)
    qchunk = (q0 + lax.broadcasted_iota(jnp.int32, (tq, 1), 0) - N_META) >> 6
    hs = range(HEADS)
    qs = [q_ref[:, h * MLA_PAD:(h + 1) * MLA_PAD] for h in hs]

    def body(it, state, masked, nblk):
        k0s = [pl.multiple_of((it * nblk + j) * tk, tk) for j in range(nblk)]
        blk = [slice(h * MLA_PAD, (h + 1) * MLA_PAD) for h in hs]
        s = [[lax.dot_general(qs[h], k_ref[pl.ds(k0, tk), blk[h]], (((1,), (1,)), ((), ())),
                              preferred_element_type=F32) for k0 in k0s] for h in hs]
        if masked:
            vis = [((k0 + lax.broadcasted_iota(jnp.int32, (1, tk), 1) - N_META) >> 6) <= qchunk for k0 in k0s]
            s = [[jnp.where(vis[j], s[h][j], -1e30) for j in range(nblk)] for h in hs]
        m_new = []
        for h in hs:
            m = state[h][0]
            for x in s[h]:
                m = jnp.maximum(m, jnp.max(x, axis=-1, keepdims=True))
            m_new.append(m)
        p = [[jnp.exp2(x - m_new[h]).astype(BF16) for x in s[h]] for h in hs]
        pv = []
        for h in hs:
            acc = None
            for j, k0 in enumerate(k0s):
                part = jnp.dot(p[h][j], v_ref[pl.ds(k0, tk), blk[h]], preferred_element_type=F32)
                acc = part if acc is None else acc + part
            pv.append(acc)
        return tuple((m_new[h], jnp.exp2(state[h][0] - m_new[h]) * state[h][1] + pv[h]) for h in hs)

    state = tuple(_softmax_init(tq) for _ in hs)
    n_pair = n_full // 2
    state = lax.fori_loop(0, n_pair, functools.partial(body, masked=False, nblk=2), state)
    state = lax.fori_loop(2 * n_pair, nkb, functools.partial(body, masked=True, nblk=1), state)
    lp = n_kb * tk
    k_t = pl.multiple_of(jnp.minimum(nkb * tk, lp - N_META), N_META)
    tail_chunk = (k_t + lax.broadcasted_iota(jnp.int32, (1, N_META), 1) - N_META) >> 6
    visible = (tail_chunk <= qchunk) & (nkb < n_kb)
    for h in hs:
        blk = slice(h * MLA_PAD, (h + 1) * MLA_PAD)
        s = lax.dot_general(qs[h], k_ref[pl.ds(k_t, N_META), blk], (((1,), (1,)), ((), ())),
                            preferred_element_type=F32)
        _, acc = _softmax_step(state[h], jnp.where(visible, s, -1e30), v_ref[pl.ds(k_t, N_META), blk])
        o_ref[:, blk] = _softmax_finish(acc)


def _mla_prompt_call(q, k, v, *, tq, tk):
    b, lp, hw = q.shape
    assert tq == tk and tq % CHUNK == 0 and lp % tk == 0
    body = functools.partial(_mla_prompt_kernel, tq=tq, tk=tk, n_kb=lp // tk)
    return pl.pallas_call(
        body, grid=(b, lp // tq),
        in_specs=[pl.BlockSpec((None, tq, hw), lambda i, j: (i, j, 0)),
                  pl.BlockSpec((None, lp, hw), lambda i, j: (i, 0, 0)),
                  pl.BlockSpec((None, lp, hw), lambda i, j: (i, 0, 0))],
        out_specs=pl.BlockSpec((None, tq, hw), lambda i, j: (i, j, 0)),
        out_shape=jax.ShapeDtypeStruct((b, lp, hw), BF16),
        compiler_params=pltpu.CompilerParams(dimension_semantics=("parallel", "arbitrary"),
                                             vmem_limit_bytes=VMEM_LIMIT),
        name="mla_prompt_attn")(q, k, v)


def _mla_sample_kernel(q_ref, kn_ref, vn_ref, cckv_ref, ckpet_ref, wkt_ref, wv_ref, gk_ref, o_ref):
    t = q_ref.shape[0]
    hs = range(HEADS)
    blk = [slice(h * MLA_PAD, (h + 1) * MLA_PAD) for h in hs]
    ckv = cckv_ref[...].astype(BF16)
    knt = lax.dot_general(wkt_ref[...], ckv, (((1,), (1,)), ((), ())), preferred_element_type=F32)
    kpet = ckpet_ref[...]
    ssq_pe = jnp.sum(kpet * kpet, axis=0, keepdims=True)
    kpet16 = kpet.astype(BF16)
    zpad = jnp.zeros((MLA_PAD - MLA_QK, kpet.shape[1]), BF16)
    kts = [knt[h * MLA_NOPE:(h + 1) * MLA_NOPE, :] for h in hs]
    rk = [lax.rsqrt((jnp.sum(x * x, axis=0, keepdims=True) + ssq_pe) * (1.0 / MLA_QK) + EPS) for x in kts]
    qg = [(q_ref[:, blk[h]].astype(F32) * gk_ref[...]).astype(BF16) for h in hs]
    s_c = [jnp.dot(qg[h], jnp.concatenate([kpet16, kts[h].astype(BF16), zpad], axis=0),
                   preferred_element_type=F32) * rk[h] for h in hs]
    s_n = [lax.dot_general(q_ref[:, blk[h]], kn_ref[:, blk[h]], (((1,), (1,)), ((), ())),
                           preferred_element_type=F32) for h in hs]
    m = [jnp.maximum(jnp.max(s_c[h], axis=-1, keepdims=True), jnp.max(s_n[h], axis=-1, keepdims=True))
         for h in hs]
    p_c = [jnp.exp2(s_c[h] - m[h]) for h in hs]
    p_n = [jnp.exp2(s_n[h] - m[h]) for h in hs]
    den = [jnp.sum(p_c[h], axis=-1, keepdims=True) + jnp.sum(p_n[h], axis=-1, keepdims=True) for h in hs]
    lat = jnp.dot(jnp.concatenate([x.astype(BF16) for x in p_c], axis=0), ckv, preferred_element_type=F32)
    for h in hs:
        o = jnp.dot(lat[h * t:(h + 1) * t].astype(BF16), wv_ref[:, blk[h]], preferred_element_type=F32)
        o = o + jnp.dot(p_n[h].astype(BF16), vn_ref[:, blk[h]], preferred_element_type=F32)
        o_ref[:, blk[h]] = (o / den[h]).astype(BF16)


def _mla_sample_call(q, kn, vn, cache_ckv, cache_kpe_t, layer, lw):
    b, t, hw = q.shape
    past = cache_ckv.shape[2]
    seq = lambda: pl.BlockSpec((None, t, hw), lambda i: (i, 0, 0))
    consts = [lw["mla_wkt"], lw["mla_wv"], lw["mla_gk"]]
    return pl.pallas_call(
        _mla_sample_kernel, grid=(b,),
        in_specs=[seq(), seq(), seq(),
                  pl.BlockSpec((None, None, past, cache_ckv.shape[3]), lambda i: (layer, i, 0, 0)),
                  pl.BlockSpec((None, None, cache_kpe_t.shape[2], past), lambda i: (layer, i, 0, 0))]
        + [_const_spec(c.shape) for c in consts],
        out_specs=seq(),
        out_shape=jax.ShapeDtypeStruct((b, t, hw), BF16),
        compiler_params=pltpu.CompilerParams(dimension_semantics=("arbitrary",),
                                             vmem_limit_bytes=VMEM_LIMIT),
        name="mla_sample_attn")(q, kn, vn, cache_ckv, cache_kpe_t, *consts)


def _lru_kernel(cb_ref, buf_ref, h0_ref, cw_ref, cbias_ref, wa_ref, ba_ref, wx_ref, bx_ref, lam_ref,
                y_ref, nbuf_ref, hl_ref, xext, hcar, sa, sb, sh, *, nb, tt, t_valid):
    for s in range(nb):
        _lru_sequence(cb_ref.at[s], buf_ref.at[s], h0_ref.at[s], cw_ref, cbias_ref, wa_ref, ba_ref, wx_ref,
                      bx_ref, lam_ref, y_ref.at[s], nbuf_ref.at[s], hl_ref.at[s], xext.at[s], hcar.at[s],
                      sa.at[s], sb.at[s], sh.at[s], tt=tt, t_valid=t_valid)


def _lru_sequence(cb_ref, buf_ref, h0_ref, cw_ref, cbias_ref, wa_ref, ba_ref, wx_ref, bx_ref, lam_ref,
                  y_ref, nbuf_ref, hl_ref, xext, hcar, sa, sb, sh, *, tt, t_valid):
    ti = pl.program_id(1)

    @pl.when(ti == 0)
    def _():
        xext[0:8, :] = buf_ref[...]
        hcar[...] = h0_ref[...]

    @pl.when(ti > 0)
    def _():
        xext[0:8, :] = xext[tt:tt + 8, :]

    xext[8:tt + 8, :] = cb_ref[:, 0:MIXW]
    xc = _causal_conv(xext, cw_ref, tt) + cbias_ref[...]
    r = _sigmoid(_bdot(xc, wa_ref[...]) + ba_ref[...])
    i = _sigmoid(_bdot(xc, wx_ref[...]) + bx_ref[...])
    log_a = -LRU_C * r * _softplus(-lam_ref[...])
    pos = ti * tt + lax.broadcasted_iota(jnp.int32, (tt, 1), 0)
    log_a = jnp.where(pos < t_valid, log_a, 0.0)
    a = jnp.exp(log_a)
    b2 = -jnp.tanh(log_a) * (a * a + 1.0)
    pos2 = b2 > 0.0
    b = jnp.where(pos2, b2 * lax.rsqrt(jnp.where(pos2, b2, 1.0)), 0.0) * (i * xc)
    g = tt // 8
    a = a.reshape(g, 8, MIXW)
    b = b.reshape(g, 8, MIXW)
    sub = lax.broadcasted_iota(jnp.int32, (g, 8, 1), 1)
    for s in (1, 2, 4):
        keep = sub >= s
        b = a * jnp.where(keep, pltpu.roll(b, s, 1), 0.0) + b
        a = a * jnp.where(keep, pltpu.roll(a, s, 1), 1.0)
    a = a.reshape(tt, MIXW)
    b = b.reshape(tt, MIXW)
    halves = [slice(0, 128), slice(128, MIXW)]
    for k, lanes in enumerate(halves):
        sa[k] = a[:, lanes]
        sb[k] = b[:, lanes]
    ag =jnp.concatenate([sa[k, pl.ds(7, g, stride=8), :] for k in range(2)], axis=1)
    bg = jnp.concatenate([sb[k, pl.ds(7, g, stride=8), :] for k in range(2)], axis=1)
    if g % 8 == 0:
        s = 1
        while s < g:
            bg = ag * _shift_rows(bg, s, 0.0) + bg
            ag = ag * _shift_rows(ag, s, 1.0)
            s *= 2
        h_end = ag * hcar[...] + bg
        h_in = _shift_rows(h_end, 1, hcar[...])
        hcar[...] = h_end[g - 1:g, :]
    else:
        states = [hcar[...]]
        for j in range(g):
            states.append(ag[j:j + 1, :] * states[-1] + bg[j:j + 1, :])
        h_in = jnp.concatenate(states[:g], axis=0)
        hcar[...] = states[g]
    for r in range(8):
        rows = pl.ds(r, g, stride=8)
        for k, lanes in enumerate(halves):
            sh[k, rows, :] = sa[k, rows, :] * h_in[:, lanes] + sb[k, rows, :]
    h = jnp.concatenate([sh[0], sh[1]], axis=1)
    y_ref[...] = (h * _gelu_tanh(cb_ref[:, MIXW:2 * MIXW])).astype(BF16)

    last = t_valid - 1

    @pl.when(ti == last // tt)
    def _():
        r0 = last % tt
        hl_ref[...] = h[r0:r0 + 1, :]
        nbuf_ref[...] = xext[pl.ds(8 + r0 - (CONV_W - 2), CONV_W - 1), :]


def _lru_call(cb, buf8, h0, lw, *, nb, tt, t_valid):
    b, lp, _ = cb.shape
    consts = [lw["lru_cw"], lw["lru_cb"], lw["lru_wa"], lw["lru_ba"], lw["lru_wx"], lw["lru_bx"],
              lw["lru_lam"]]
    body = functools.partial(_lru_kernel, nb=nb, tt=tt, t_valid=t_valid)
    per_b = lambda s: pl.BlockSpec((nb,) + s, lambda i, j: (i, 0, 0))
    return pl.pallas_call(
        body, grid=(b // nb, lp // tt),
        in_specs=[pl.BlockSpec((nb, tt, B_W), lambda i, j: (i, j, 0)), per_b((8, MIXW)),
                  per_b((1, MIXW))] + [_const_spec(c.shape) for c in consts],
        out_specs=[pl.BlockSpec((nb, tt, MIXW), lambda i, j: (i, j, 0)),
                   per_b((CONV_W - 1, MIXW)), per_b((1, MIXW))],
        out_shape=[jax.ShapeDtypeStruct((b, lp, MIXW), BF16),
                   jax.ShapeDtypeStruct((b, CONV_W - 1, MIXW), F32),
                   jax.ShapeDtypeStruct((b, 1, MIXW), F32)],
        scratch_shapes=[pltpu.VMEM((nb, tt + 8, MIXW), F32), pltpu.VMEM((nb, 1, MIXW), F32)]
        + [pltpu.VMEM((nb, 2, tt, 128), F32)] * 3,
        compiler_params=pltpu.CompilerParams(dimension_semantics=("parallel", "arbitrary"),
                                             vmem_limit_bytes=VMEM_LIMIT),
        name="rglru")(cb, buf8, h0, *consts)


def _gdn_kernel(cc_ref, buf_ref, s0_ref, cw_ref, alog_ref, dtb_ref, gon_ref,
                y_ref, nbuf_ref, so_ref, xext, q_s, k_s, v_s, b_s, g_s, o_s, u_s, a_s, w_s, e_s, l_s,
                m_s, h_s, n_s, st_ref, *, nb, n_tiles, tt, c, t_valid):
    ti = pl.program_id(1)
    qkw = 3 * MIXW
    bd = _block_diag_mask()
    bd01 = jnp.where(bd, 1.0, 0.0).astype(BF16)
    pos = ti * tt + lax.broadcasted_iota(jnp.int32, (tt, 1), 0)
    valid = pos < t_valid

    for s in range(nb):
        xe = xext.at[s]
        sr = slice(s * tt, (s + 1) * tt)

        @pl.when(ti == 0)
        def _():
            xe[0:8, :] = buf_ref[s]
            _load_state(st_ref, s0_ref, s)

        @pl.when(ti > 0)
        def _():
            xe[0:8, :] = xe[tt:tt + 8, :]

        xe[8:tt + 8, :] = cc_ref[s, :, 0:qkw]
        xs = _silu(_causal_conv(xe, cw_ref, tt))
        q = xs[:, 0:MIXW]
        k = xs[:, MIXW:2 * MIXW]
        q_s[sr, :] = q * lax.rsqrt(_xdot_r01(q * q, bd01) + EPS) * (HD ** -0.5)
        k_s[sr, :] = k * lax.rsqrt(_xdot_r01(k * k, bd01) + EPS)
        v_s[sr, :] = xs[:, 2 * MIXW:3 * MIXW]
        a_in = cc_ref[s, :, qkw + MIXW:qkw + 2 * MIXW]
        b_in = cc_ref[s, :, qkw + 2 * MIXW:qkw + 3 * MIXW]
        b_s[sr, :] = jnp.where(valid, _sigmoid(b_in), 0.0)
        g_s[sr, :] = jnp.where(valid, -jnp.exp(alog_ref[...]) * _softplus(a_in + dtb_ref[...]), 0.0)

    tril4, strict4, eye4 = _head_masks(c)
    tril01 = _tril01(c)
    n_chunks = tt // c

    ch = range(nb * n_chunks)
    rs = [slice(ci * c, (ci + 1) * c) for ci in ch]
    gc = [_xdot_l01(tril01, g_s[r, :]) for r in rs]
    gct = [jnp.sum(jnp.where(eye4, x, 0.0), axis=0, keepdims=True) for x in gc]
    decay = [jnp.where(tril4, jnp.exp(jnp.where(tril4, gc[i] - gct[i], 0.0)), 0.0) for i in ch]
    kc = [k_s[r, :] for r in rs]
    beta = [b_s[r, :] for r in rs]
    kb = [kc[i] * beta[i] for i in ch]
    sc = [_mm_nt(jnp.concatenate([kb[i], q_s[rs[i], :]], axis=0), _embed(kc[i], bd01)) for i in ch]
    tinv = _inv_unit_lower([jnp.where(strict4, sc[i][0:c] * decay[i], 0.0) for i in ch], eye4, bd01)
    attn = [jnp.where(tril4, sc[i][c:2 * c] * decay[i], 0.0).astype(BF16) for i in ch]
    egc = [jnp.exp(x) for x in gc]
    uw = [_mm(tinv[i], jnp.concatenate([_embed(v_s[rs[i], :] * beta[i], bd01),
                                        _embed(kb[i] * egc[i], bd01)], axis=1)) for i in ch]
    glast = [x[c - 1:c, :] for x in gc]
    kd = [(kc[i] * jnp.exp(glast[i] - gc[i])).astype(BF16) for i in ch]
    an = [lax.dot_general(kd[i], jnp.concatenate([uw[i][:, MIXW:], uw[i][:, :MIXW]], axis=1).astype(BF16),
                          (((0,), (0,)), ((), ())), preferred_element_type=F32) for i in ch]
    for i in ch:
        u_s[rs[i], :] = uw[i][:, 0:MIXW]
        w_s[rs[i], :] = uw[i][:, MIXW:2 * MIXW].astype(BF16)
        e_s[rs[i], :] = (q_s[rs[i], :] * egc[i]).astype(BF16)
        a_s[rs[i], :] = attn[i]
        m_s[i] = jnp.where(bd, an[i][:, 0:MIXW], 0.0).astype(BF16)
        n_s[i] = jnp.where(bd, an[i][:, MIXW:2 * MIXW], 0.0)
        l_s[i:i + 1, :] = jnp.exp(glast[i])

    for ci in range(n_chunks):
        for s in range(nb):
            i = s * n_chunks + ci
            s_bd = st_ref[s]
            s16 = s_bd.astype(BF16)
            h_s[i] = s16
            st_ref[s] = s_bd * l_s[i:i + 1, :] + n_s[i] - jnp.dot(m_s[i], s16, preferred_element_type=F32)

    r = [jnp.dot(jnp.concatenate([w_s[rs[i], :], e_s[rs[i], :]], axis=0), h_s[i], preferred_element_type=F32)
         for i in ch]
    v_new = [u_s[rs[i], :] - r[i][0:c] for i in ch]
    ov = [jnp.dot(a_s[rs[i], :], _embed(v_new[i], bd01), preferred_element_type=F32) for i in ch]
    for i in ch:
        o_s[rs[i], :] = r[i][c:2 * c] + ov[i]

    last = t_valid - 1
    for s in range(nb):
        o = o_s[s * tt:(s + 1) * tt, :]
        ms = _xdot_r01(o * o, bd01) * (1.0 / HD)
        z = cc_ref[s, :, qkw:qkw + MIXW]
        y_ref[s] = (o * lax.rsqrt(ms + EPS) * gon_ref[...] * _silu(z)).astype(BF16)

        @pl.when(ti == last // tt)
        def _():
            nbuf_ref[s] = xext[s, pl.ds(8 + last % tt - (CONV_W - 2), CONV_W - 1), :]

        @pl.when(ti == n_tiles - 1)
        def _():
            _store_state(so_ref, st_ref, s)


def _gdn_call(cc, buf8, s0, lw, *, nb, tt, c, t_valid):
    b, lp, _ = cc.shape
    qkw = 3 * MIXW
    consts = [lw["gdn_cw"], lw["gdn_alog"], lw["gdn_dtb"], lw["gdn_gon"]]
    body = functools.partial(_gdn_kernel, nb=nb, n_tiles=lp // tt, tt=tt, c=c, t_valid=t_valid)
    per_b = lambda s: pl.BlockSpec((nb,) + s, lambda i, j: (i,) + (0,) * len(s))
    n_ch = nb * (tt // c)
    return pl.pallas_call(
        body, grid=(b // nb, lp // tt),
        in_specs=[pl.BlockSpec((nb, tt, C_W), lambda i, j: (i, j, 0)), per_b((8, qkw)),
                  per_b((HEADS, HD, HD))] + [_const_spec(x.shape) for x in consts],
        out_specs=[pl.BlockSpec((nb, tt, MIXW), lambda i, j: (i, j, 0)),
                   per_b((CONV_W - 1, qkw)), per_b((HEADS, HD, HD))],
        out_shape=[jax.ShapeDtypeStruct((b, lp, MIXW), BF16),
                   jax.ShapeDtypeStruct((b, CONV_W - 1, qkw), F32),
                   jax.ShapeDtypeStruct((b, HEADS, HD, HD), F32)],
        scratch_shapes=[pltpu.VMEM((nb, tt + 8, qkw), F32)] + [pltpu.VMEM((nb * tt, MIXW), F32)] * 7
        + [pltpu.VMEM((nb * tt, MIXW), BF16)] * 3 + [pltpu.VMEM((-(-n_ch // 8) * 8, MIXW), F32),
                                                     pltpu.VMEM((n_ch, MIXW, MIXW), BF16),
                                                     pltpu.VMEM((n_ch, MIXW, MIXW), BF16),
                                                     pltpu.VMEM((n_ch, MIXW, MIXW), F32),
                                                     pltpu.VMEM((nb, MIXW, MIXW), F32)],
        compiler_params=pltpu.CompilerParams(dimension_semantics=("parallel", "arbitrary"),
                                             vmem_limit_bytes=VMEM_LIMIT),
        name="gated_delta")(cc, buf8, s0, *consts)


def _rwkv_kernel(cd_ref, prev_ref, s0_ref, mu_ref, w0_ref, wb_ref, a0_ref, ab_ref, gb_ref, kk_ref,
                 ka_ref, rk_ref, lnw_ref, lnb_ref,
                 y_ref, shift_ref, so_ref, xext, r_s, k_s, v_s, n_s, b_s, w_s, o_s, xv_s,
                 t_s, a_s, lhs_s, p_s, g_s, h_s, d_s, rk_s, gt_s, st_ref, *, nb, n_tiles, tt, c, t_valid):
    ti = pl.program_id(1)
    pos = ti * tt + lax.broadcasted_iota(jnp.int32, (tt, 1), 0)
    valid = pos < t_valid
    bd = _block_diag_mask()
    bd01 = jnp.where(bd, 1.0, 0.0).astype(BF16)

    for s in range(nb):
        xe = xext.at[s]
        sr = slice(s * tt, (s + 1) * tt)

        @pl.when(ti == 0)
        def _():
            xe[0:8, :] = prev_ref[s]
            _load_state(st_ref, s0_ref, s)

        @pl.when(ti > 0)
        def _():
            xe[0:8, :] = xe[tt:tt + 8, :]

        x = cd_ref[s]
        xe[8:tt + 8, :] = x
        xm = x + (xe[pl.ds(7, tt), :] - x) * mu_ref[...]
        r = xm[:, 0:MIXW]
        k = xm[:, MIXW:2 * MIXW]
        lo_blk = xm[:, 3 * MIXW:3 * MIXW + 128]
        logw = RWKV_DECAY_SCALE * _sigmoid(w0_ref[...] + _bdot(jnp.tanh(lo_blk), wb_ref[...]))
        a = _sigmoid(a0_ref[...] + _bdot(lo_blk, ab_ref[...]))
        kkv = k * kk_ref[...]
        kkn = kkv * lax.rsqrt(_xdot_r01(kkv * kkv, bd01) + EPS)
        kkn = jnp.where(valid, kkn, 0.0)
        kmod = k * (1.0 + (a - 1.0) * ka_ref[...])
        r_s[sr, :] = r
        k_s[sr, :] = jnp.where(valid, kmod, 0.0)
        v_s[sr, :] = xm[:, 2 * MIXW:3 * MIXW]
        n_s[sr, :] = kkn
        b_s[sr, :] = kkn * a
        w_s[sr, :] = jnp.where(valid, logw, 0.0)
        rk_s[sr, :] = _xdot_r01(r * kmod * rk_ref[...], bd01)
        gt_s[sr, :] = _bdot(_sigmoid(xm[:, 3 * MIXW + 128:3 * MIXW + 256]), gb_ref[...])

    tril4, strict4, eye4 = _head_masks(c)
    tril01 = _tril01(c)

    n_chunks = tt // c

    ch = range(nb * n_chunks)
    rs = [slice(ci * c, (ci + 1) * c) for ci in ch]
    lw = [w_s[r, :] for r in rs]
    cum = [_xdot_l01(tril01, x) for x in lw]
    ecum = [jnp.exp(x) for x in cum]
    pin = [jnp.exp(-x) for x in cum]
    pc = [x[c - 1:c, :] for x in ecum]
    kinv = [k_s[rs[i], :] * pin[i] for i in ch]
    binv = [b_s[rs[i], :] * pin[i] for i in ch]
    nd = [n_s[rs[i], :] * jnp.exp(cum[i] - lw[i]) for i in ch]
    lhs = [jnp.concatenate([nd[i], r_s[rs[i], :] * ecum[i]], axis=0).astype(BF16) for i in ch]
    sk = [_mm_nt(lhs[i], _embed(kinv[i], bd01)) for i in ch]
    sb = [_mm_nt(lhs[i], _embed(binv[i], bd01)) for i in ch]
    tinv = _inv_unit_lower([jnp.where(strict4, sb[i][0:c], 0.0) for i in ch], eye4, bd01)
    av = [_mm(jnp.concatenate([jnp.where(strict4, sk[i][0:c], 0.0), jnp.where(tril4, sk[i][c:2 * c], 0.0)],
                              axis=0), _embed(v_s[rs[i], :], bd01)) for i in ch]
    tn = [_mm(tinv[i], jnp.concatenate([_embed(nd[i], bd01), _embed(av[i][0:c], bd01)], axis=1)) for i in ch]
    bp = [(binv[i] * pc[i]).astype(BF16) for i in ch]
    gh = [lax.dot_general(tn[i].astype(BF16), bp[i], (((0,), (0,)), ((), ())), preferred_element_type=F32)
          for i in ch]
    vk = [_bdot_tn(v_s[rs[i], :], kinv[i] * pc[i]) for i in ch]
    for i in ch:
        lhs_s[2 * i * c:2 * (i + 1) * c, :] = lhs[i]
        t_s[rs[i], :] = tinv[i].astype(BF16)
        a_s[rs[i], :] = jnp.where(tril4, sb[i][c:2 * c], 0.0).astype(BF16)
        xv_s[rs[i], :] = av[i][0:c]
        o_s[rs[i], :] = av[i][c:2 * c]
        p_s[i:i + 1, :] = pc[i]
        g_s[i] = jnp.where(bd, gh[i][0:MIXW], 0.0).astype(BF16)
        d_s[i] = jnp.where(bd, vk[i] - gh[i][MIXW:2 * MIXW], 0.0)

    for ci in range(n_chunks):
        for s in range(nb):
            i = s * n_chunks + ci
            st = st_ref[s]
            s16 = st.astype(BF16)
            h_s[i] = s16
            st_ref[s] = st * p_s[i:i + 1, :] + d_s[i] - jnp.dot(s16, g_s[i], preferred_element_type=F32)

    r2 = [lax.dot_general(lhs_s[2 * i * c:2 * (i + 1) * c, :], h_s[i], (((1,), (1,)), ((), ())),
                          preferred_element_type=F32) for i in ch]
    u = [jnp.dot(t_s[rs[i], :], _embed(r2[i][0:c] + xv_s[rs[i], :], bd01), preferred_element_type=F32)
         for i in ch]
    au = [jnp.dot(a_s[rs[i], :], _embed(u[i], bd01), preferred_element_type=F32) for i in ch]
    for i in ch:
        o_s[rs[i], :] = r2[i][c:2 * c] + o_s[rs[i], :] - au[i]

    last = t_valid - 1
    for s in range(nb):
        sr = slice(s * tt, (s + 1) * tt)
        o = o_s[sr, :]
        mean = _xdot_r01(o, bd01) * (1.0 / HD)
        d = o - mean
        var = _xdot_r01(d * d, bd01) * (1.0 / HD)
        o = d * lax.rsqrt(var + RWKV_GN_EPS) * lnw_ref[...] + lnb_ref[...]
        o = o + rk_s[sr, :] * v_s[sr, :]
        y_ref[s] = (o * gt_s[sr, :]).astype(BF16)

        @pl.when(ti == last // tt)
        def _():
            shift_ref[s] = xext[s, pl.ds(8 + last % tt, 1), :]

        @pl.when(ti == n_tiles - 1)
        def _():
            _store_state(so_ref, st_ref, s)


def _rwkv_call(cd, prev8, s0, lw, *, nb, tt, c, t_valid):
    b, lp, _ = cd.shape
    consts = [lw["rwkv_mu"], lw["rwkv_w0"], lw["rwkv_wb"], lw["rwkv_a0"], lw["rwkv_ab"], lw["rwkv_gb"],
              lw["rwkv_kk"], lw["rwkv_ka"], lw["rwkv_rk"], lw["rwkv_lnw"], lw["rwkv_lnb"]]
    body = functools.partial(_rwkv_kernel, nb=nb, n_tiles=lp // tt, tt=tt, c=c, t_valid=t_valid)
    per_b = lambda s: pl.BlockSpec((nb,) + s, lambda i, j: (i,) + (0,) * len(s))
    n_ch = nb * (tt // c)
    return pl.pallas_call(
        body, grid=(b // nb, lp // tt),
        in_specs=[pl.BlockSpec((nb, tt, D_W), lambda i, j: (i, j, 0)), per_b((8, D_W)),
                  per_b((HEADS, HD, HD))] + [_const_spec(x.shape) for x in consts],
        out_specs=[pl.BlockSpec((nb, tt, MIXW), lambda i, j: (i, j, 0)),
                   per_b((1, D_W)), per_b((HEADS, HD, HD))],
        out_shape=[jax.ShapeDtypeStruct((b, lp, MIXW), BF16),
                   jax.ShapeDtypeStruct((b, 1, D_W), F32),
                   jax.ShapeDtypeStruct((b, HEADS, HD, HD), F32)],
        scratch_shapes=[pltpu.VMEM((nb, tt + 8, D_W), F32)] + [pltpu.VMEM((nb * tt, MIXW), F32)] * 8
        + [pltpu.VMEM((nb * tt, MIXW), BF16)] * 2 + [pltpu.VMEM((2 * nb * tt, MIXW), BF16),
                                                     pltpu.VMEM((-(-n_ch // 8) * 8, MIXW), F32),
                                                     pltpu.VMEM((n_ch, MIXW, MIXW), BF16),
                                                     pltpu.VMEM((n_ch, MIXW, MIXW), BF16),
                                                     pltpu.VMEM((n_ch, MIXW, MIXW), F32)]
        + [pltpu.VMEM((nb * tt, MIXW), F32)] * 2 + [pltpu.VMEM((nb, MIXW, MIXW), F32)],
        compiler_params=pltpu.CompilerParams(dimension_semantics=("parallel", "arbitrary"),
                                             vmem_limit_bytes=VMEM_LIMIT),
        name="rwkv7")(cd, prev8, s0, *consts)


def _pad_cols(w, width):
    return jnp.pad(w, ((0, 0), (0, width - w.shape[1])))


def _mla_head_cols(w, n_heads, per_head, pieces):
    blocks = []
    for h in range(n_heads):
        cols = [sign * w[:, h * per_head + lo:h * per_head + hi] for lo, hi, sign in pieces]
        blocks.append(_pad_cols(jnp.concatenate(cols, axis=1), MLA_PAD))
    return jnp.concatenate(blocks, axis=1)


def _block_diag_heads(w):
    eye = jnp.eye(HEADS, dtype=w.dtype)[:, None, :, None]
    return (w[:, :, None, :] * eye).reshape(MIXW, MIXW)


def _prep_layer_weights(p, l):
    half = MLA_ROPE // 2
    row = lambda v: v.reshape(1, -1).astype(F32)
    lw = {}
    for tag in ("1", "2"):
        lw["ffn" + tag] = dict(g_ffn=row(p["norm_ffn" + tag][l]), layer=l, w1=p["ffn%s_w1_bf16" % tag],
                               w2=p["ffn%s_w2_bf16" % tag])
    w_in = p["w_in"][l]
    o = 0
    cq = w_in[:, o:o + 256]; o += 256
    ckv = w_in[:, o:o + 128]; o += 128
    kpe = w_in[:, o:o + MLA_ROPE]; o += MLA_ROPE
    xb_gb = w_in[:, o:o + 512]; o += 512
    qkvz = w_in[:, o:o + 1024]; o += 1024
    a_in = w_in[:, o:o + HEADS]; o += HEADS
    b_in = w_in[:, o:o + HEADS]; o += HEADS
    rw = w_in[:, o:o + 1024]
    kpe_rot = jnp.concatenate([-kpe[:, half:], kpe[:, :half]], axis=1)
    w_in_p = jnp.concatenate([
        cq, ckv, _pad_cols(jnp.concatenate([kpe, kpe_rot], axis=1), 128),
        xb_gb,
        qkvz, jnp.repeat(a_in, HD, axis=1), jnp.repeat(b_in, HD, axis=1),
        rw], axis=1)
    lw["ffn1"]["g_mix"] = row(p["norm_mix"][l])
    lw["ffn1"]["w_in"] = w_in_p.astype(BF16)
    w_out = p["w_out"][l]
    wo_a = jnp.concatenate([jnp.pad(w_out[h * HD:(h + 1) * HD], ((0, MLA_PAD - HD), (0, 0)))
                            for h in range(HEADS)], axis=0)
    lw["w_out"] = [wo_a.astype(BF16)] + [w_out[MIXW * i:MIXW * (i + 1)].astype(BF16) for i in (1, 2, 3)]
    wuq = p["mla_w_uq"][l]
    wq = _mla_head_cols(wuq, HEADS, MLA_QK, [(MLA_NOPE, MLA_QK, 1.0), (0, MLA_NOPE, 1.0)])
    wq_rot = _mla_head_cols(wuq, HEADS, MLA_QK, [(MLA_NOPE + half, MLA_QK, -1.0),
                                                (MLA_NOPE, MLA_NOPE + half, 1.0)])
    lw["mla_wq"] = jnp.concatenate([wq, wq_rot], axis=1).astype(BF16)
    wukv = p["mla_w_ukv"][l]
    zero_rope = jnp.zeros((wukv.shape[0], MLA_ROPE), wukv.dtype)
    wk = jnp.concatenate([_pad_cols(jnp.concatenate([zero_rope, wukv[:, h * 128:h * 128 + MLA_NOPE]], axis=1),
                                    MLA_PAD) for h in range(HEADS)], axis=1)
    wv = jnp.concatenate([_pad_cols(wukv[:, h * 128 + MLA_NOPE:(h + 1) * 128], MLA_PAD)
                          for h in range(HEADS)], axis=1)
    lw["mla_wkv"] = jnp.concatenate([wk, wv], axis=1).astype(BF16)
    lw["mla_wv"] = wv.astype(BF16)
    lw["mla_wkt"] = jnp.concatenate([wukv[:, h * 128:h * 128 + MLA_NOPE] for h in range(HEADS)],
                                    axis=1).T.astype(BF16)
    lw["mla_gqa"] = row(p["mla_q_a_norm"][l])
    lw["mla_gkva"] = row(p["mla_kv_a_norm"][l])
    perm = lambda g: row(_pad_cols(jnp.concatenate([g[MLA_NOPE:], g[:MLA_NOPE]])[None], MLA_PAD))
    lw["mla_gq"] = perm(p["mla_q_norm"][l])
    lw["mla_gk"] = perm(p["mla_k_norm"][l])
    lw["lru_cw"] = p["lru_conv_w"][l]
    lw["lru_cb"] = row(p["lru_conv_b"][l])
    lw["lru_wa"] = _block_diag_heads(p["lru_wa"][l]).astype(BF16)
    lw["lru_ba"] = row(p["lru_ba"][l])
    lw["lru_wx"] = _block_diag_heads(p["lru_wx"][l]).astype(BF16)
    lw["lru_bx"] = row(p["lru_bx"][l])
    lw["lru_lam"] = row(p["lru_lambda"][l])
    lw["gdn_cw"] = p["gdn_conv_w"][l]
    lw["gdn_alog"] = row(jnp.repeat(p["gdn_a_log"][l], HD))
    lw["gdn_dtb"] = row(jnp.repeat(p["gdn_dt_bias"][l], HD))
    lw["gdn_gon"] = row(jnp.tile(p["gdn_o_norm"][l], HEADS))
    z64 = jnp.zeros((64, MIXW), F32)
    lw["rwkv_mu"] = row(p["rwkv_mu"][l])
    lw["rwkv_w0"] = row(p["rwkv_w0"][l])
    lw["rwkv_wb"] = jnp.concatenate([p["rwkv_w_b"][l], z64], axis=0).astype(BF16)
    lw["rwkv_a0"] = row(p["rwkv_a0"][l])
    lw["rwkv_ab"] = jnp.concatenate([z64, p["rwkv_a_b"][l]], axis=0).astype(BF16)
    lw["rwkv_gb"] = p["rwkv_g_b"][l].astype(BF16)
    lw["rwkv_kk"] = row(p["rwkv_k_k"][l])
    lw["rwkv_ka"] = row(p["rwkv_k_a"][l])
    lw["rwkv_rk"] = row(p["rwkv_r_k"][l])
    lw["rwkv_lnw"] = row(p["rwkv_ln_w"][l])
    lw["rwkv_lnb"] = row(p["rwkv_ln_b"][l])
    return lw


def _rope_tables(pos):
    inv = ROPE_THETA ** (-jnp.arange(0, MLA_ROPE, 2, dtype=F32) / MLA_ROPE)
    ang = pos.astype(F32)[:, None] * inv[None, :]
    cos, sin = jnp.cos(ang), jnp.sin(ang)
    n = pos.shape[0]
    cos_t = jnp.concatenate([cos, cos, jnp.ones((n, MLA_PAD - MLA_ROPE), F32)], axis=1)
    sin_t = jnp.concatenate([sin, sin, jnp.zeros((n, MLA_PAD - MLA_ROPE), F32)], axis=1)
    return cos_t, sin_t


def _hist8(rows):
    return jnp.pad(rows.astype(F32), ((0, 0), (8 - rows.shape[1], 0), (0, 0)))


def _run_group(x, lws, final_g, states, cfg):
    b, lp, d = x.shape
    n = b * lp
    t_valid, tm, tt, c = cfg["t_valid"], cfg["tm"], cfg["tt"], cfg["c"]
    cos_t, sin_t = cfg["rope"]
    xf = x.reshape(n, d)
    new_states = []
    depth = len(lws)
    for l in range(depth):
        lw = lws[l]
        st = states[l]
        xf, q, k, v, ckv, kpe, cb, cc, cd = _token_call(xf, lw["ffn1"], tm=tm,
                                                        post=(cos_t, sin_t, lw, (b, lp, t_valid)))
        r3 = lambda a: a.reshape(b, lp, a.shape[-1])
        if st["mla"] is None:
            ya = _mla_prompt_call(r3(q), r3(k), r3(v), tq=cfg["tq"], tk=cfg["tk"])
        else:
            cache_ckv, cache_kpe_t = st["mla"]
            ya = _mla_sample_call(r3(q), r3(k), r3(v), cache_ckv, cache_kpe_t, l, lw)
        yb, lru_conv, lru_h = _lru_call(r3(cb), st["lru_conv"], st["lru_h"], lw, nb=cfg["nb"], tt=tt,
                                        t_valid=t_valid)
        yc, gdn_conv, gdn_s = _gdn_call(r3(cc), st["gdn_conv"], st["gdn_s"], lw, nb=cfg["nb"], tt=tt, c=c,
                                        t_valid=t_valid)
        yd, shift, rwkv_s = _rwkv_call(r3(cd), st["rwkv_shift"], st["rwkv_s"], lw, nb=cfg["nb"], tt=tt, c=c,
                                       t_valid=t_valid)
        ys = [ya.reshape(n, -1), yb.reshape(n, -1), yc.reshape(n, -1), yd.reshape(n, -1)]
        xf = _token_call(xf, lw["ffn2"], pre=(ys, lw["w_out"]), tm=tm,
                         final_g=final_g if l == depth - 1 else None)[0]
        new_states.append((ckv.reshape(b, t_valid, -1), kpe.reshape(b, t_valid, -1), lru_conv, lru_h[:, 0],
                           gdn_conv, gdn_s, shift[:, 0], rwkv_s))
    stacked = [jnp.stack(t) for t in zip(*new_states)]
    return xf.reshape(b, lp, d), stacked


def _group_config(b, t_valid):
    divisor = lambda target: max(d for d in range(1, target + 1) if b % d == 0)
    if t_valid <= CHUNK:
        lp = t_valid
        return dict(lp=lp, t_valid=t_valid, tm=b * lp if b * lp <= 512 else lp, tt=lp, c=lp, tq=lp, tk=1024,
                    nb=divisor(8))
    lp = -(-t_valid // 384) * 384
    return dict(lp=lp, t_valid=t_valid, tm=384, tt=384, c=CHUNK, tq=384, tk=384, nb=divisor(2))


def kernel(x_prompt, x_sample, cache_mla_ckv, cache_mla_kpe, state_lru_conv, state_lru_h, state_gdn_conv, state_gdn_s, state_rwkv_shift, state_rwkv_s, meta_tokens, norm_ffn1, ffn1_w1, ffn1_w2, norm_mix, w_in, mla_q_a_norm, mla_w_uq, mla_kv_a_norm, mla_w_ukv, mla_q_norm, mla_k_norm, lru_conv_w, lru_conv_b, lru_wa, lru_ba, lru_wx, lru_bx, lru_lambda, gdn_conv_w, gdn_a_log, gdn_dt_bias, gdn_o_norm, rwkv_mu, rwkv_w0, rwkv_w_b, rwkv_a0, rwkv_a_b, rwkv_g_b, rwkv_k_k, rwkv_k_a, rwkv_r_k, rwkv_ln_w, rwkv_ln_b, w_out, norm_ffn2, ffn2_w1, ffn2_w2, final_norm):
    p = dict(norm_ffn1=norm_ffn1, ffn1_w1=ffn1_w1, ffn1_w2=ffn1_w2, norm_mix=norm_mix, w_in=w_in,
             mla_q_a_norm=mla_q_a_norm, mla_w_uq=mla_w_uq, mla_kv_a_norm=mla_kv_a_norm,
             mla_w_ukv=mla_w_ukv, mla_q_norm=mla_q_norm, mla_k_norm=mla_k_norm,
             lru_conv_w=lru_conv_w, lru_conv_b=lru_conv_b, lru_wa=lru_wa, lru_ba=lru_ba,
             lru_wx=lru_wx, lru_bx=lru_bx, lru_lambda=lru_lambda, gdn_conv_w=gdn_conv_w,
             gdn_a_log=gdn_a_log, gdn_dt_bias=gdn_dt_bias, gdn_o_norm=gdn_o_norm, rwkv_mu=rwkv_mu,
             rwkv_w0=rwkv_w0, rwkv_w_b=rwkv_w_b, rwkv_a0=rwkv_a0, rwkv_a_b=rwkv_a_b, rwkv_g_b=rwkv_g_b,
             rwkv_k_k=rwkv_k_k, rwkv_k_a=rwkv_k_a, rwkv_r_k=rwkv_r_k, rwkv_ln_w=rwkv_ln_w,
             rwkv_ln_b=rwkv_ln_b, w_out=w_out, norm_ffn2=norm_ffn2, ffn2_w1=ffn2_w1, ffn2_w2=ffn2_w2)
    depth = w_in.shape[0]
    for name in ("ffn1_w1", "ffn1_w2", "ffn2_w1", "ffn2_w2"):
        p[name + "_bf16"] = p[name].astype(BF16)
    lws = [_prep_layer_weights(p, l) for l in range(depth)]
    final_g = final_norm.reshape(1, -1).astype(F32)
    d = x_prompt.shape[-1]

    bp, seq, _ = x_prompt.shape
    tp = N_META + seq
    cfg = _group_config(bp, tp)
    lp = cfg["lp"]
    x0 = jnp.concatenate([jnp.broadcast_to(meta_tokens.astype(F32)[None], (bp, N_META, d)), x_prompt,
                          jnp.zeros((bp, lp - tp, d), F32)], axis=1)
    cfg["rope"] = tuple(jnp.tile(t, (bp, 1)) for t in _rope_tables(jnp.arange(lp)))
    zero = dict(mla=None, lru_conv=jnp.zeros((bp, 8, MIXW), F32), lru_h=jnp.zeros((bp, 1, MIXW), F32),
                gdn_conv=jnp.zeros((bp, 8, 3 * MIXW), F32), gdn_s=jnp.zeros((bp, HEADS, HD, HD), F32),
                rwkv_shift=jnp.zeros((bp, 8, D_W), F32), rwkv_s=jnp.zeros((bp, HEADS, HD, HD), F32))
    yp, p_new = _run_group(x0, lws, final_g, [zero] * depth, cfg)

    bs, ts, _ = x_sample.shape
    past = cache_mla_ckv.shape[2]
    cfg_s = _group_config(bs, ts)
    cfg_s["rope"] = tuple(jnp.tile(t, (bs, 1)) for t in _rope_tables(past + jnp.arange(ts)))
    cache_kpe_t = jnp.swapaxes(cache_mla_kpe, 2, 3)
    st_s = [dict(mla=(cache_mla_ckv, cache_kpe_t), lru_conv=_hist8(state_lru_conv[l]),
                 lru_h=state_lru_h[l][:, None].astype(F32), gdn_conv=_hist8(state_gdn_conv[l]),
                 gdn_s=state_gdn_s[l], rwkv_shift=_hist8(state_rwkv_shift[l][:, None]),
                 rwkv_s=state_rwkv_s[l]) for l in range(depth)]
    ys, s_new = _run_group(x_sample, lws, final_g, st_s, cfg_s)
    return (yp[:, N_META:tp], ys) + tuple(p_new) + tuple(s_new)
```

```python
import functools
import math

import jax
import jax.numpy as jnp
import numpy as np
from jax import lax
from jax.experimental import pallas as pl
from jax.experimental.pallas import tpu as pltpu

F32 = jnp.float32
BF16 = jnp.bfloat16

EPS = 1e-6
N_META = 16
CHUNK = 64
CONV_W = 4
HEADS = 4
HD = 64
MIXW = HEADS * HD
MLA_NOPE = 64
MLA_ROPE = 32
MLA_QK = MLA_NOPE + MLA_ROPE
MLA_V = 64
MLA_PAD = 128
ROPE_THETA = 10000.0
LRU_C = 8.0
RWKV_GN_EPS = 64e-5
RWKV_DECAY_SCALE = -0.606531

VMEM_LIMIT = 56 * 1024 * 1024
MXU_N = 256

A_W = 512
B_W = 512
C_W = 1536
D_W = 1024


def _bdot(a, b):
    return jnp.dot(a.astype(BF16), b.astype(BF16), preferred_element_type=F32)


def _bdot_tn(a, b):
    return lax.dot_general(a.astype(BF16), b.astype(BF16), (((0,), (0,)), ((), ())),
                           preferred_element_type=F32)


def _split3(x):
    x1 = x.astype(BF16)
    r1 = x - x1.astype(F32)
    x2 = r1.astype(BF16)
    r2 = r1 - x2.astype(F32)
    return x1, x2, r2.astype(BF16)


def _xdot_r01(x, m01):
    return jnp.dot(x.astype(BF16), m01, preferred_element_type=F32)


def _xdot_l01(m01, x):
    x1, x2, x3 = _split3(x)
    d = functools.partial(jnp.dot, preferred_element_type=F32)
    return d(m01, x1) + d(m01, x2) + d(m01, x3)


def _sigmoid(x):
    return 0.5 * jnp.tanh(0.5 * x) + 0.5


def _silu(x):
    return x * _sigmoid(x)


def _softplus(x):
    return jnp.maximum(x, 0.0) + jnp.log(1.0 + jnp.exp(-jnp.abs(x)))


def _gelu_tanh(x):
    return 0.5 * x * (1.0 + jnp.tanh(0.7978845608028654 * (x + 0.044715 * (x * x * x))))


def _rms_rows(x, g):
    return x * lax.rsqrt(jnp.mean(x * x, axis=-1, keepdims=True) + EPS) * g


def _block_diag_mask():
    r = lax.broadcasted_iota(jnp.int32, (MIXW, MIXW), 0) >> 6
    c = lax.broadcasted_iota(jnp.int32, (MIXW, MIXW), 1) >> 6
    return r == c


def _embed(x, bd01):
    c = x.shape[0]
    x = x.astype(BF16)
    if c < HD:
        x = jnp.concatenate([x, jnp.zeros((HD - c, MIXW), BF16)], axis=0)
    return jnp.concatenate([x] * HEADS, axis=0) * bd01


def _mm(a, b):
    return jnp.dot(a.astype(BF16), b, preferred_element_type=F32)


def _mm_nt(a, b):
    return lax.dot_general(a.astype(BF16), b, (((1,), (1,)), ((), ())), preferred_element_type=F32)


def _head_masks(c):
    row = lax.broadcasted_iota(jnp.int32, (c, MIXW), 0)
    col = lax.broadcasted_iota(jnp.int32, (c, MIXW), 1) & (HD - 1)
    return row >= col, row > col, row == col


def _tril01(c):
    r = lax.broadcasted_iota(jnp.int32, (c, c), 0)
    k = lax.broadcasted_iota(jnp.int32, (c, c), 1)
    return jnp.where(r >= k, 1.0, 0.0).astype(BF16)


def _inv_unit_lower(lms, eye4, bd01):
    c = lms[0].shape[0]
    levels = int(math.log2(c))
    eye = jnp.where(eye4, 1.0, 0.0)
    n = [-x for x in lms]
    t = [eye + x for x in n]
    if levels < 2:
        return t
    p = [_mm(x, _embed(x, bd01)) for x in n]
    for _ in range(levels - 2):
        r = [_mm(jnp.concatenate([pi, ti], axis=0), _embed(pi, bd01)) for pi, ti in zip(p, t)]
        p = [x[0:c] for x in r]
        t = [ti + x[c:2 * c] for ti, x in zip(t, r)]
    return [ti + _mm(ti, _embed(pi, bd01)) for pi, ti in zip(p, t)]


def _load_state(st_ref, s0_ref, s):
    st_ref[s] = jnp.zeros((MIXW, MIXW), F32)
    for h in range(HEADS):
        st_ref[s, h * HD:(h + 1) * HD, h * HD:(h + 1) * HD] = s0_ref[s, h].astype(F32)


def _store_state(so_ref, st_ref, s):
    for h in range(HEADS):
        so_ref[s, h] = st_ref[s, h * HD:(h + 1) * HD, h * HD:(h + 1) * HD]


def _shift_rows(x, s, fill):
    rolled = pltpu.roll(x, s, 0)
    row = lax.broadcasted_iota(jnp.int32, x.shape, 0)
    return jnp.where(row >= s, rolled, fill)


def _causal_conv(xext_ref, w_ref, tt):
    acc = None
    for j in range(CONV_W):
        term = xext_ref[pl.ds(8 - (CONV_W - 1) + j, tt), :] * w_ref[j:j + 1, :]
        acc = term if acc is None else acc + term
    return acc


def _const_spec(shape):
    nd = len(shape)
    return pl.BlockSpec(shape, lambda *_: (0,) * nd, pipeline_mode=pl.Buffered(1))


def _token_kernel(*refs, has_pre, has_post, has_final, d_ff, ff_chunk):
    it = iter(refs)
    x_ref = next(it)
    if has_pre:
        y_refs = [next(it) for _ in range(4)]
        wo_refs = [next(it) for _ in range(4)]
    gffn_ref, w1_ref, w2_ref = next(it), next(it), next(it)
    if has_post:
        gmix_ref, win_ref = next(it), next(it)
        prep_in = [next(it) for _ in range(8)]
    if has_final:
        gfin_ref = next(it)
    xo_ref = next(it)
    if has_post:
        prep_out = [next(it) for _ in range(5)]
        col_refs = [next(it) for _ in range(3)]

    x = x_ref[...]
    if has_pre:
        for y_ref, wo_ref in zip(y_refs, wo_refs):
            x = x + jnp.dot(y_ref[...], wo_ref[...], preferred_element_type=F32)
    h = _rms_rows(x, gffn_ref[...]).astype(BF16)
    acc = None
    for lo, hi in ((0, ff_chunk), (ff_chunk, d_ff)):
        gate = jnp.dot(h, w1_ref[:, lo:hi], preferred_element_type=F32)
        up = jnp.dot(h, w1_ref[:, d_ff + lo:d_ff + hi], preferred_element_type=F32)
        act = (_silu(gate) * up).astype(BF16)
        part = jnp.dot(act, w2_ref[lo:hi, :], preferred_element_type=F32)
        acc = part if acc is None else acc + part
    x = x + 0.5 * acc
    if has_final:
        xo_ref[...] = _rms_rows(x, gfin_ref[...])
    else:
        xo_ref[...] = x
    if has_post:
        hm = _rms_rows(x, gmix_ref[...]).astype(BF16)
        _mla_prep(jnp.dot(hm, win_ref[:, 0:A_W], preferred_element_type=F32), *prep_in, *prep_out)
        off = A_W
        for c_ref in col_refs:
            wd = c_ref.shape[-1]
            c_ref[...] = jnp.dot(hm, win_ref[:, off:off + wd], preferred_element_type=F32)
            off += wd


def _token_call(x, lw, *, pre=None, post=None, final_g=None, tm):
    n, d = x.shape
    d_ff = lw["w2"].shape[1]
    ff_chunk = -(-(d_ff // MXU_N) // 2) * MXU_N
    row = lambda w: pl.BlockSpec((tm, w), lambda i: (i, 0))
    args, specs = [x], [row(d)]
    if pre is not None:
        ys, wos = pre
        for y in ys:
            args.append(y)
            specs.append(row(y.shape[1]))
        for w in wos:
            args.append(w)
            specs.append(_const_spec(w.shape))
    args.append(lw["g_ffn"])
    specs.append(_const_spec(lw["g_ffn"].shape))
    layer = lw["layer"]
    for name in ("w1", "w2"):
        args.append(lw[name])
        specs.append(pl.BlockSpec((None,) + lw[name].shape[1:], lambda i: (layer, 0, 0),
                                  pipeline_mode=pl.Buffered(1)))
    if post is not None:
        cos_t, sin_t, mw, (b, lp, t_valid) = post
        for name in ("g_mix", "w_in"):
            args.append(lw[name])
            specs.append(_const_spec(lw[name].shape))
        args += [cos_t, sin_t]
        specs += [row(MLA_PAD), row(MLA_PAD)]
        for name in ("mla_gqa", "mla_wq", "mla_gkva", "mla_wkv", "mla_gq", "mla_gk"):
            args.append(mw[name])
            specs.append(_const_spec(mw[name].shape))
    if final_g is not None:
        args.append(final_g)
        specs.append(_const_spec(final_g.shape))
    out_shape = [jax.ShapeDtypeStruct((n, d), F32)]
    out_specs = [row(d)]
    if post is not None:
        hw = HEADS * MLA_PAD
        out_shape += [jax.ShapeDtypeStruct((n, hw), BF16)] * 3
        out_specs += [row(hw)] * 3
        for wd in (128, MLA_ROPE):
            if lp > t_valid:
                tps = lp // tm
                out_shape.append(jax.ShapeDtypeStruct((b, t_valid, wd), F32))
                out_specs.append(pl.BlockSpec((None, tm, wd), lambda i: (i // tps, i % tps, 0)))
            else:
                out_shape.append(jax.ShapeDtypeStruct((n, wd), F32))
                out_specs.append(row(wd))
        for wd in (B_W, C_W, D_W):
            out_shape.append(jax.ShapeDtypeStruct((n, wd), F32))
            out_specs.append(row(wd))
    body = functools.partial(_token_kernel, has_pre=pre is not None, has_post=post is not None,
                             has_final=final_g is not None, d_ff=d_ff, ff_chunk=ff_chunk)
    return pl.pallas_call(
        body, grid=(n // tm,), in_specs=specs, out_specs=out_specs, out_shape=out_shape,
        compiler_params=pltpu.CompilerParams(dimension_semantics=("parallel",),
                                             vmem_limit_bytes=VMEM_LIMIT),
        name="token_block")(*args)


def _mla_prep(ca, cos_ref, sin_ref, gqa_ref, wq_ref, gkva_ref, wkv_ref, gq_ref, gk_ref,
              q_ref, k_ref, v_ref, ckv_ref, kpe_ref):
    cs = cos_ref[...]
    sn = sin_ref[...]
    hw = HEADS * MLA_PAD
    cqn = _rms_rows(ca[:, 0:256], gqa_ref[...])
    qq = _bdot(cqn, wq_ref[...])
    scale = math.log2(math.e) / math.sqrt(MLA_QK)
    for h in range(HEADS):
        lo = h * MLA_PAD
        blk = qq[:, lo:lo + MLA_PAD] * cs + qq[:, hw + lo:hw + lo + MLA_PAD] * sn
        ms = jnp.sum(blk * blk, axis=-1, keepdims=True) * (1.0 / MLA_QK)
        q_ref[:, lo:lo + MLA_PAD] = (blk * lax.rsqrt(ms + EPS) * (gq_ref[...] * scale)).astype(BF16)
    ckv = _rms_rows(ca[:, 256:384], gkva_ref[...])
    ckv_ref[...] = ckv
    kblk = ca[:, 384:512]
    rope_lane = lax.broadcasted_iota(jnp.int32, kblk.shape, 1) < MLA_ROPE
    kpe = jnp.where(rope_lane, kblk * cs, 0.0) + pltpu.roll(kblk, MLA_PAD - MLA_ROPE, 1) * sn
    kpe_ref[...] = kpe[:, 0:MLA_ROPE]
    kv = _bdot(ckv, wkv_ref[...])
    ones_lane = lax.broadcasted_iota(jnp.int32, (kv.shape[0], MLA_PAD), 1) == MLA_V
    for h in range(HEADS):
        lo = h * MLA_PAD
        kb = kv[:, lo:lo + MLA_PAD] + kpe
        ms = jnp.sum(kb * kb, axis=-1, keepdims=True) * (1.0 / MLA_QK)
        k_ref[:, lo:lo + MLA_PAD] = (kb * lax.rsqrt(ms + EPS) * gk_ref[...]).astype(BF16)
        v_ref[:, lo:lo + MLA_PAD] = jnp.where(ones_lane, 1.0, kv[:, hw + lo:hw + lo + MLA_PAD]).astype(BF16)


def _softmax_step(carry, s, vblk):
    m, acc = carry
    m_new = jnp.maximum(m, jnp.max(s, axis=-1, keepdims=True))
    p = jnp.exp2(s - m_new)
    acc = jnp.exp2(m - m_new) * acc + jnp.dot(p.astype(BF16), vblk, preferred_element_type=F32)
    return m_new, acc


def _softmax_init(t):
    return jnp.full((t, 1), -1e30, F32), jnp.zeros((t, MLA_PAD), F32)


def _softmax_finish(acc):
    return (acc / acc[:, MLA_V:MLA_V + 1]).astype(BF16)


def _mla_prompt_kernel(q_ref, k_ref, v_ref, o_ref, *, tq, tk, n_kb):
    q0 = pl.program_id(1) * tq
    first_chunk = (q0 - N_META) >> 6
    last_chunk = (q0 + tq - 1 - N_META) >> 6
    n_full = (N_META + CHUNK * (first_chunk + 1)) // tk
    nkb = jnp.minimum((N_META + CHUNK * (last_chunk + 1)) // tk, n_kb)
    qchunk = (q0 + lax.broadcasted_iota(jnp.int32, (tq, 1), 0) - N_META) >> 6
    hs = range(HEADS)
    qs = [q_ref[:, h * MLA_PAD:(h + 1) * MLA_PAD] for h in hs]

    def body(it, state, masked, nblk):
        k0s = [pl.multiple_of((it * nblk + j) * tk, tk) for j in range(nblk)]
        blk = [slice(h * MLA_PAD, (h + 1) * MLA_PAD) for h in hs]
        s = [[lax.dot_general(qs[h], k_ref[pl.ds(k0, tk), blk[h]], (((1,), (1,)), ((), ())),
                              preferred_element_type=F32) for k0 in k0s] for h in hs]
        if masked:
            vis = [((k0 + lax.broadcasted_iota(jnp.int32, (1, tk), 1) - N_META) >> 6) <= qchunk for k0 in k0s]
            s = [[jnp.where(vis[j], s[h][j], -1e30) for j in range(nblk)] for h in hs]
        m_new = []
        for h in hs:
            m = state[h][0]
            for x in s[h]:
                m = jnp.maximum(m, jnp.max(x, axis=-1, keepdims=True))
            m_new.append(m)
        p = [[jnp.exp2(x - m_new[h]).astype(BF16) for x in s[h]] for h in hs]
        pv = []
        for h in hs:
            acc = None
            for j, k0 in enumerate(k0s):
                part = jnp.dot(p[h][j], v_ref[pl.ds(k0, tk), blk[h]], preferred_element_type=F32)
                acc = part if acc is None else acc + part
            pv.append(acc)
        return tuple((m_new[h], jnp.exp2(state[h][0] - m_new[h]) * state[h][1] + pv[h]) for h in hs)

    state = tuple(_softmax_init(tq) for _ in hs)
    n_pair = n_full // 2
    state = lax.fori_loop(0, n_pair, functools.partial(body, masked=False, nblk=2), state)
    state = lax.fori_loop(2 * n_pair, nkb, functools.partial(body, masked=True, nblk=1), state)
    lp = n_kb * tk
    k_t = pl.multiple_of(jnp.minimum(nkb * tk, lp - N_META), N_META)
    tail_chunk = (k_t + lax.broadcasted_iota(jnp.int32, (1, N_META), 1) - N_META) >> 6
    visible = (tail_chunk <= qchunk) & (nkb < n_kb)
    for h in hs:
        blk = slice(h * MLA_PAD, (h + 1) * MLA_PAD)
        s = lax.dot_general(qs[h], k_ref[pl.ds(k_t, N_META), blk], (((1,), (1,)), ((), ())),
                            preferred_element_type=F32)
        _, acc = _softmax_step(state[h], jnp.where(visible, s, -1e30), v_ref[pl.ds(k_t, N_META), blk])
        o_ref[:, blk] = _softmax_finish(acc)


def _mla_prompt_call(q, k, v, *, tq, tk):
    b, lp, hw = q.shape
    assert tq == tk and tq % CHUNK == 0 and lp % tk == 0
    body = functools.partial(_mla_prompt_kernel, tq=tq, tk=tk, n_kb=lp // tk)
    return pl.pallas_call(
        body, grid=(b, lp // tq),
        in_specs=[pl.BlockSpec((None, tq, hw), lambda i, j: (i, j, 0)),
                  pl.BlockSpec((None, lp, hw), lambda i, j: (i, 0, 0)),
                  pl.BlockSpec((None, lp, hw), lambda i, j: (i, 0, 0))],
        out_specs=pl.BlockSpec((None, tq, hw), lambda i, j: (i, j, 0)),
        out_shape=jax.ShapeDtypeStruct((b, lp, hw), BF16),
        compiler_params=pltpu.CompilerParams(dimension_semantics=("parallel", "arbitrary"),
                                             vmem_limit_bytes=VMEM_LIMIT),
        name="mla_prompt_attn")(q, k, v)


def _mla_sample_kernel(q_ref, kn_ref, vn_ref, cckv_ref, ckpet_ref, wkt_ref, wv_ref, gk_ref, o_ref):
    nb, t = q_ref.shape[0], q_ref.shape[1]
    hs = range(HEADS)
    ss = range(nb)
    it = [(s, h) for s in ss for h in hs]
    blk = [slice(h * MLA_PAD, (h + 1) * MLA_PAD) for h in hs]
    ckv = [cckv_ref[s].astype(BF16) for s in ss]
    knt = [lax.dot_general(wkt_ref[...], x, (((1,), (1,)), ((), ())), preferred_element_type=F32) for x in ckv]
    kpet = [ckpet_ref[s] for s in ss]
    ssq_pe = [jnp.sum(x * x, axis=0, keepdims=True) for x in kpet]
    kpet16 = [x.astype(BF16) for x in kpet]
    zpad = jnp.zeros((MLA_PAD - MLA_QK, kpet[0].shape[1]), BF16)
    kts = [knt[s][h * MLA_NOPE:(h + 1) * MLA_NOPE, :] for s, h in it]
    rk = [lax.rsqrt((jnp.sum(kts[i] * kts[i], axis=0, keepdims=True) + ssq_pe[s]) * (1.0 / MLA_QK) + EPS)
          for i, (s, h) in enumerate(it)]
    qg = [(q_ref[s, :, blk[h]].astype(F32) * gk_ref[...]).astype(BF16) for s, h in it]
    s_c = [jnp.dot(qg[i], jnp.concatenate([kpet16[s], kts[i].astype(BF16), zpad], axis=0),
                   preferred_element_type=F32) * rk[i] for i, (s, h) in enumerate(it)]
    s_n = [lax.dot_general(q_ref[s, :, blk[h]], kn_ref[s, :, blk[h]], (((1,), (1,)), ((), ())),
                           preferred_element_type=F32) for s, h in it]
    m = [jnp.maximum(jnp.max(s_c[i], axis=-1, keepdims=True), jnp.max(s_n[i], axis=-1, keepdims=True))
         for i in range(len(it))]
    p_c = [jnp.exp2(s_c[i] - m[i]) for i in range(len(it))]
    p_n = [jnp.exp2(s_n[i] - m[i]) for i in range(len(it))]
    den = [jnp.sum(p_c[i], axis=-1, keepdims=True) + jnp.sum(p_n[i], axis=-1, keepdims=True)
           for i in range(len(it))]
    lat = [jnp.dot(jnp.concatenate([p_c[s * HEADS + h].astype(BF16) for h in hs], axis=0), ckv[s],
                   preferred_element_type=F32) for s in ss]
    for i, (s, h) in enumerate(it):
        o = jnp.dot(lat[s][h * t:(h + 1) * t].astype(BF16), wv_ref[:, blk[h]], preferred_element_type=F32)
        o = o + jnp.dot(p_n[i].astype(BF16), vn_ref[s, :, blk[h]], preferred_element_type=F32)
        o_ref[s, :, blk[h]] = (o / den[i]).astype(BF16)


def _mla_sample_call(q, kn, vn, cache_ckv, cache_kpe_t, layer, lw, *, nb):
    b, t, hw = q.shape
    past = cache_ckv.shape[2]
    seq = lambda: pl.BlockSpec((nb, t, hw), lambda i: (i, 0, 0))
    consts = [lw["mla_wkt"], lw["mla_wv"], lw["mla_gk"]]
    return pl.pallas_call(
        _mla_sample_kernel, grid=(b // nb,),
        in_specs=[seq(), seq(), seq(),
                  pl.BlockSpec((None, nb, past, cache_ckv.shape[3]), lambda i: (layer, i, 0, 0)),
                  pl.BlockSpec((None, nb, cache_kpe_t.shape[2], past), lambda i: (layer, i, 0, 0))]
        + [_const_spec(c.shape) for c in consts],
        out_specs=seq(),
        out_shape=jax.ShapeDtypeStruct((b, t, hw), BF16),
        compiler_params=pltpu.CompilerParams(dimension_semantics=("arbitrary",),
                                             vmem_limit_bytes=VMEM_LIMIT),
        name="mla_sample_attn")(q, kn, vn, cache_ckv, cache_kpe_t, *consts)


def _lru_kernel(cb_ref, buf_ref, h0_ref, cw_ref, cbias_ref, wa_ref, ba_ref, wx_ref, bx_ref, lam_ref,
                y_ref, nbuf_ref, hl_ref, xext, hcar, sa, sb, sh, *, nb, tt, t_valid):
    for s in range(nb):
        _lru_sequence(cb_ref.at[s], buf_ref.at[s], h0_ref.at[s], cw_ref, cbias_ref, wa_ref, ba_ref, wx_ref,
                      bx_ref, lam_ref, y_ref.at[s], nbuf_ref.at[s], hl_ref.at[s], xext.at[s], hcar.at[s],
                      sa.at[s], sb.at[s], sh.at[s], tt=tt, t_valid=t_valid)


def _lru_sequence(cb_ref, buf_ref, h0_ref, cw_ref, cbias_ref, wa_ref, ba_ref, wx_ref, bx_ref, lam_ref,
                  y_ref, nbuf_ref, hl_ref, xext, hcar, sa, sb, sh, *, tt, t_valid):
    ti = pl.program_id(1)

    @pl.when(ti == 0)
    def _():
        xext[0:8, :] = buf_ref[...]
        hcar[...] = h0_ref[...]

    @pl.when(ti > 0)
    def _():
        xext[0:8, :] = xext[tt:tt + 8, :]

    xext[8:tt + 8, :] = cb_ref[:, 0:MIXW]
    xc = _causal_conv(xext, cw_ref, tt) + cbias_ref[...]
    r = _sigmoid(_bdot(xc, wa_ref[...]) + ba_ref[...])
    i = _sigmoid(_bdot(xc, wx_ref[...]) + bx_ref[...])
    log_a = -LRU_C * r * _softplus(-lam_ref[...])
    pos = ti * tt + lax.broadcasted_iota(jnp.int32, (tt, 1), 0)
    log_a = jnp.where(pos < t_valid, log_a, 0.0)
    a = jnp.exp(log_a)
    b2 = -jnp.tanh(log_a) * (a * a + 1.0)
    pos2 = b2 > 0.0
    b = jnp.where(pos2, b2 * lax.rsqrt(jnp.where(pos2, b2, 1.0)), 0.0) * (i * xc)
    g = tt // 8
    a = a.reshape(g, 8, MIXW)
    b = b.reshape(g, 8, MIXW)
    sub = lax.broadcasted_iota(jnp.int32, (g, 8, 1), 1)
    for s in (1, 2, 4):
        keep = sub >= s
        b = a * jnp.where(keep, pltpu.roll(b, s, 1), 0.0) + b
        a = a * jnp.where(keep, pltpu.roll(a, s, 1), 1.0)
    a = a.reshape(tt, MIXW)
    b = b.reshape(tt, MIXW)
    halves = [slice(0, 128), slice(128, MIXW)]
    for k, lanes in enumerate(halves):
        sa[k] = a[:, lanes]
        sb[k] = b[:, lanes]
    ag =jnp.concatenate([sa[k, pl.ds(7, g, stride=8), :] for k in range(2)], axis=1)
    bg = jnp.concatenate([sb[k, pl.ds(7, g, stride=8), :] for k in range(2)], axis=1)
    if g % 8 == 0:
        s = 1
        while s < g:
            bg = ag * _shift_rows(bg, s, 0.0) + bg
            ag = ag * _shift_rows(ag, s, 1.0)
            s *= 2
        h_end = ag * hcar[...] + bg
        h_in = _shift_rows(h_end, 1, hcar[...])
        hcar[...] = h_end[g - 1:g, :]
    else:
        states = [hcar[...]]
        for j in range(g):
            states.append(ag[j:j + 1, :] * states[-1] + bg[j:j + 1, :])
        h_in = jnp.concatenate(states[:g], axis=0)
        hcar[...] = states[g]
    for r in range(8):
        rows = pl.ds(r, g, stride=8)
        for k, lanes in enumerate(halves):
            sh[k, rows, :] = sa[k, rows, :] * h_in[:, lanes] + sb[k, rows, :]
    h = jnp.concatenate([sh[0], sh[1]], axis=1)
    y_ref[...] = (h * _gelu_tanh(cb_ref[:, MIXW:2 * MIXW])).astype(BF16)

    last = t_valid - 1

    @pl.when(ti == last // tt)
    def _():
        r0 = last % tt
        hl_ref[...] = h[r0:r0 + 1, :]
        nbuf_ref[...] = xext[pl.ds(8 + r0 - (CONV_W - 2), CONV_W - 1), :]


def _lru_call(cb, buf8, h0, lw, *, nb, tt, t_valid):
    b, lp, _ = cb.shape
    consts = [lw["lru_cw"], lw["lru_cb"], lw["lru_wa"], lw["lru_ba"], lw["lru_wx"], lw["lru_bx"],
              lw["lru_lam"]]
    body = functools.partial(_lru_kernel, nb=nb, tt=tt, t_valid=t_valid)
    per_b = lambda s: pl.BlockSpec((nb,) + s, lambda i, j: (i, 0, 0))
    return pl.pallas_call(
        body, grid=(b // nb, lp // tt),
        in_specs=[pl.BlockSpec((nb, tt, B_W), lambda i, j: (i, j, 0)), per_b((8, MIXW)),
                  per_b((1, MIXW))] + [_const_spec(c.shape) for c in consts],
        out_specs=[pl.BlockSpec((nb, tt, MIXW), lambda i, j: (i, j, 0)),
                   per_b((CONV_W - 1, MIXW)), per_b((1, MIXW))],
        out_shape=[jax.ShapeDtypeStruct((b, lp, MIXW), BF16),
                   jax.ShapeDtypeStruct((b, CONV_W - 1, MIXW), F32),
                   jax.ShapeDtypeStruct((b, 1, MIXW), F32)],
        scratch_shapes=[pltpu.VMEM((nb, tt + 8, MIXW), F32), pltpu.VMEM((nb, 1, MIXW), F32)]
        + [pltpu.VMEM((nb, 2, tt, 128), F32)] * 3,
        compiler_params=pltpu.CompilerParams(dimension_semantics=("parallel", "arbitrary"),
                                             vmem_limit_bytes=VMEM_LIMIT),
        name="rglru")(cb, buf8, h0, *consts)


def _gdn_kernel(cc_ref, buf_ref, s0_ref, cw_ref, alog_ref, dtb_ref, gon_ref,
                y_ref, nbuf_ref, so_ref, xext, q_s, k_s, v_s, b_s, g_s, o_s, u_s, a_s, w_s, e_s, l_s,
                m_s, h_s, n_s, st_ref, *, nb, n_tiles, tt, c, t_valid):
    ti = pl.program_id(1)
    qkw = 3 * MIXW
    bd = _block_diag_mask()
    bd01 = jnp.where(bd, 1.0, 0.0).astype(BF16)
    pos = ti * tt + lax.broadcasted_iota(jnp.int32, (tt, 1), 0)
    valid = pos < t_valid

    for s in range(nb):
        xe = xext.at[s]
        sr = slice(s * tt, (s + 1) * tt)

        @pl.when(ti == 0)
        def _():
            xe[0:8, :] = buf_ref[s]
            _load_state(st_ref, s0_ref, s)

        @pl.when(ti > 0)
        def _():
            xe[0:8, :] = xe[tt:tt + 8, :]

        xe[8:tt + 8, :] = cc_ref[s, :, 0:qkw]
        xs = _silu(_causal_conv(xe, cw_ref, tt))
        q = xs[:, 0:MIXW]
        k = xs[:, MIXW:2 * MIXW]
        q_s[sr, :] = q * lax.rsqrt(_xdot_r01(q * q, bd01) + EPS) * (HD ** -0.5)
        k_s[sr, :] = k * lax.rsqrt(_xdot_r01(k * k, bd01) + EPS)
        v_s[sr, :] = xs[:, 2 * MIXW:3 * MIXW]
        a_in = cc_ref[s, :, qkw + MIXW:qkw + 2 * MIXW]
        b_in = cc_ref[s, :, qkw + 2 * MIXW:qkw + 3 * MIXW]
        b_s[sr, :] = jnp.where(valid, _sigmoid(b_in), 0.0)
        g_s[sr, :] = jnp.where(valid, -jnp.exp(alog_ref[...]) * _softplus(a_in + dtb_ref[...]), 0.0)

    tril4, strict4, eye4 = _head_masks(c)
    tril01 = _tril01(c)
    n_chunks = tt // c

    ch = range(nb * n_chunks)
    rs = [slice(ci * c, (ci + 1) * c) for ci in ch]
    gc = [_xdot_l01(tril01, g_s[r, :]) for r in rs]
    gct = [jnp.sum(jnp.where(eye4, x, 0.0), axis=0, keepdims=True) for x in gc]
    decay = [jnp.where(tril4, jnp.exp(jnp.where(tril4, gc[i] - gct[i], 0.0)), 0.0) for i in ch]
    kc = [k_s[r, :] for r in rs]
    beta = [b_s[r, :] for r in rs]
    kb = [kc[i] * beta[i] for i in ch]
    sc = [_mm_nt(jnp.concatenate([kb[i], q_s[rs[i], :]], axis=0), _embed(kc[i], bd01)) for i in ch]
    tinv = _inv_unit_lower([jnp.where(strict4, sc[i][0:c] * decay[i], 0.0) for i in ch], eye4, bd01)
    attn = [jnp.where(tril4, sc[i][c:2 * c] * decay[i], 0.0).astype(BF16) for i in ch]
    egc = [jnp.exp(x) for x in gc]
    uw = [_mm(tinv[i], jnp.concatenate([_embed(v_s[rs[i], :] * beta[i], bd01),
                                        _embed(kb[i] * egc[i], bd01)], axis=1)) for i in ch]
    glast = [x[c - 1:c, :] for x in gc]
    kd = [(kc[i] * jnp.exp(glast[i] - gc[i])).astype(BF16) for i in ch]
    an = [lax.dot_general(kd[i], jnp.concatenate([uw[i][:, MIXW:], uw[i][:, :MIXW]], axis=1).astype(BF16),
                          (((0,), (0,)), ((), ())), preferred_element_type=F32) for i in ch]
    for i in ch:
        u_s[rs[i], :] = uw[i][:, 0:MIXW]
        w_s[rs[i], :] = uw[i][:, MIXW:2 * MIXW].astype(BF16)
        e_s[rs[i], :] = (q_s[rs[i], :] * egc[i]).astype(BF16)
        a_s[rs[i], :] = attn[i]
        m_s[i] = jnp.where(bd, an[i][:, 0:MIXW], 0.0).astype(BF16)
        n_s[i] = jnp.where(bd, an[i][:, MIXW:2 * MIXW], 0.0)
        l_s[i:i + 1, :] = jnp.exp(glast[i])

    for ci in range(n_chunks):
        for s in range(nb):
            i = s * n_chunks + ci
            s_bd = st_ref[s]
            s16 = s_bd.astype(BF16)
            h_s[i] = s16
            st_ref[s] = s_bd * l_s[i:i + 1, :] + n_s[i] - jnp.dot(m_s[i], s16, preferred_element_type=F32)

    r = [jnp.dot(jnp.concatenate([w_s[rs[i], :], e_s[rs[i], :]], axis=0), h_s[i], preferred_element_type=F32)
         for i in ch]
    v_new = [u_s[rs[i], :] - r[i][0:c] for i in ch]
    ov = [jnp.dot(a_s[rs[i], :], _embed(v_new[i], bd01), preferred_element_type=F32) for i in ch]
    for i in ch:
        o_s[rs[i], :] = r[i][c:2 * c] + ov[i]

    last = t_valid - 1
    for s in range(nb):
        o = o_s[s * tt:(s + 1) * tt, :]
        ms = _xdot_r01(o * o, bd01) * (1.0 / HD)
        z = cc_ref[s, :, qkw:qkw + MIXW]
        y_ref[s] = (o * lax.rsqrt(ms + EPS) * gon_ref[...] * _silu(z)).astype(BF16)

        @pl.when(ti == last // tt)
        def _():
            nbuf_ref[s] = xext[s, pl.ds(8 + last % tt - (CONV_W - 2), CONV_W - 1), :]

        @pl.when(ti == n_tiles - 1)
        def _():
            _store_state(so_ref, st_ref, s)


def _gdn_call(cc, buf8, s0, lw, *, nb, tt, c, t_valid):
    b, lp, _ = cc.shape
    qkw = 3 * MIXW
    consts = [lw["gdn_cw"], lw["gdn_alog"], lw["gdn_dtb"], lw["gdn_gon"]]
    body = functools.partial(_gdn_kernel, nb=nb, n_tiles=lp // tt, tt=tt, c=c, t_valid=t_valid)
    per_b = lambda s: pl.BlockSpec((nb,) + s, lambda i, j: (i,) + (0,) * len(s))
    n_ch = nb * (tt // c)
    return pl.pallas_call(
        body, grid=(b // nb, lp // tt),
        in_specs=[pl.BlockSpec((nb, tt, C_W), lambda i, j: (i, j, 0)), per_b((8, qkw)),
                  per_b((HEADS, HD, HD))] + [_const_spec(x.shape) for x in consts],
        out_specs=[pl.BlockSpec((nb, tt, MIXW), lambda i, j: (i, j, 0)),
                   per_b((CONV_W - 1, qkw)), per_b((HEADS, HD, HD))],
        out_shape=[jax.ShapeDtypeStruct((b, lp, MIXW), BF16),
                   jax.ShapeDtypeStruct((b, CONV_W - 1, qkw), F32),
                   jax.ShapeDtypeStruct((b, HEADS, HD, HD), F32)],
        scratch_shapes=[pltpu.VMEM((nb, tt + 8, qkw), F32)] + [pltpu.VMEM((nb * tt, MIXW), F32)] * 7
        + [pltpu.VMEM((nb * tt, MIXW), BF16)] * 3 + [pltpu.VMEM((-(-n_ch // 8) * 8, MIXW), F32),
                                                     pltpu.VMEM((n_ch, MIXW, MIXW), BF16),
                                                     pltpu.VMEM((n_ch, MIXW, MIXW), BF16),
                                                     pltpu.VMEM((n_ch, MIXW, MIXW), F32),
                                                     pltpu.VMEM((nb, MIXW, MIXW), F32)],
        compiler_params=pltpu.CompilerParams(dimension_semantics=("parallel", "arbitrary"),
                                             vmem_limit_bytes=VMEM_LIMIT),
        name="gated_delta")(cc, buf8, s0, *consts)


def _rwkv_kernel(cd_ref, prev_ref, s0_ref, mu_ref, w0_ref, wb_ref, a0_ref, ab_ref, gb_ref, kk_ref,
                 ka_ref, rk_ref, lnw_ref, lnb_ref,
                 y_ref, shift_ref, so_ref, xext, r_s, k_s, v_s, n_s, b_s, w_s, o_s, xv_s,
                 t_s, a_s, lhs_s, p_s, g_s, h_s, d_s, rk_s, gt_s, st_ref, *, nb, n_tiles, tt, c, t_valid):
    ti = pl.program_id(1)
    pos = ti * tt + lax.broadcasted_iota(jnp.int32, (tt, 1), 0)
    valid = pos < t_valid
    bd = _block_diag_mask()
    bd01 = jnp.where(bd, 1.0, 0.0).astype(BF16)

    for s in range(nb):
        xe = xext.at[s]
        sr = slice(s * tt, (s + 1) * tt)

        @pl.when(ti == 0)
        def _():
            xe[0:8, :] = prev_ref[s]
            _load_state(st_ref, s0_ref, s)

        @pl.when(ti > 0)
        def _():
            xe[0:8, :] = xe[tt:tt + 8, :]

        x = cd_ref[s]
        xe[8:tt + 8, :] = x
        xm = x + (xe[pl.ds(7, tt), :] - x) * mu_ref[...]
        r = xm[:, 0:MIXW]
        k = xm[:, MIXW:2 * MIXW]
        lo_blk = xm[:, 3 * MIXW:3 * MIXW + 128]
        logw = RWKV_DECAY_SCALE * _sigmoid(w0_ref[...] + _bdot(jnp.tanh(lo_blk), wb_ref[...]))
        a = _sigmoid(a0_ref[...] + _bdot(lo_blk, ab_ref[...]))
        kkv = k * kk_ref[...]
        kkn = kkv * lax.rsqrt(_xdot_r01(kkv * kkv, bd01) + EPS)
        kkn = jnp.where(valid, kkn, 0.0)
        kmod = k * (1.0 + (a - 1.0) * ka_ref[...])
        r_s[sr, :] = r
        k_s[sr, :] = jnp.where(valid, kmod, 0.0)
        v_s[sr, :] = xm[:, 2 * MIXW:3 * MIXW]
        n_s[sr, :] = kkn
        b_s[sr, :] = kkn * a
        w_s[sr, :] = jnp.where(valid, logw, 0.0)
        rk_s[sr, :] = _xdot_r01(r * kmod * rk_ref[...], bd01)
        gt_s[sr, :] = _bdot(_sigmoid(xm[:, 3 * MIXW + 128:3 * MIXW + 256]), gb_ref[...])

    tril4, strict4, eye4 = _head_masks(c)
    tril01 = _tril01(c)

    n_chunks = tt // c

    ch = range(nb * n_chunks)
    rs = [slice(ci * c, (ci + 1) * c) for ci in ch]
    lw = [w_s[r, :] for r in rs]
    cum = [_xdot_l01(tril01, x) for x in lw]
    ecum = [jnp.exp(x) for x in cum]
    pin = [jnp.exp(-x) for x in cum]
    pc = [x[c - 1:c, :] for x in ecum]
    kinv = [k_s[rs[i], :] * pin[i] for i in ch]
    binv = [b_s[rs[i], :] * pin[i] for i in ch]
    nd = [n_s[rs[i], :] * jnp.exp(cum[i] - lw[i]) for i in ch]
    lhs = [jnp.concatenate([nd[i], r_s[rs[i], :] * ecum[i]], axis=0).astype(BF16) for i in ch]
    sk = [_mm_nt(lhs[i], _embed(kinv[i], bd01)) for i in ch]
    sb = [_mm_nt(lhs[i], _embed(binv[i], bd01)) for i in ch]
    tinv = _inv_unit_lower([jnp.where(strict4, sb[i][0:c], 0.0) for i in ch], eye4, bd01)
    av = [_mm(jnp.concatenate([jnp.where(strict4, sk[i][0:c], 0.0), jnp.where(tril4, sk[i][c:2 * c], 0.0)],
                              axis=0), _embed(v_s[rs[i], :], bd01)) for i in ch]
    tn = [_mm(tinv[i], jnp.concatenate([_embed(nd[i], bd01), _embed(av[i][0:c], bd01)], axis=1)) for i in ch]
    bp = [(binv[i] * pc[i]).astype(BF16) for i in ch]
    gh = [lax.dot_general(tn[i].astype(BF16), bp[i], (((0,), (0,)), ((), ())), preferred_element_type=F32)
          for i in ch]
    vk = [_bdot_tn(v_s[rs[i], :], kinv[i] * pc[i]) for i in ch]
    for i in ch:
        lhs_s[2 * i * c:2 * (i + 1) * c, :] = lhs[i]
        t_s[rs[i], :] = tinv[i].astype(BF16)
        a_s[rs[i], :] = jnp.where(tril4, sb[i][c:2 * c], 0.0).astype(BF16)
        xv_s[rs[i], :] = av[i][0:c]
        o_s[rs[i], :] = av[i][c:2 * c]
        p_s[i:i + 1, :] = pc[i]
        g_s[i] = jnp.where(bd, gh[i][0:MIXW], 0.0).astype(BF16)
        d_s[i] = jnp.where(bd, vk[i] - gh[i][MIXW:2 * MIXW], 0.0)

    for ci in range(n_chunks):
        for s in range(nb):
            i = s * n_chunks + ci
            st = st_ref[s]
            s16 = st.astype(BF16)
            h_s[i] = s16
            st_ref[s] = st * p_s[i:i + 1, :] + d_s[i] - jnp.dot(s16, g_s[i], preferred_element_type=F32)

    r2 = [lax.dot_general(lhs_s[2 * i * c:2 * (i + 1) * c, :], h_s[i], (((1,), (1,)), ((), ())),
                          preferred_element_type=F32) for i in ch]
    u = [jnp.dot(t_s[rs[i], :], _embed(r2[i][0:c] + xv_s[rs[i], :], bd01), preferred_element_type=F32)
         for i in ch]
    au = [jnp.dot(a_s[rs[i], :], _embed(u[i], bd01), preferred_element_type=F32) for i in ch]
    for i in ch:
        o_s[rs[i], :] = r2[i][c:2 * c] + o_s[rs[i], :] - au[i]

    last = t_valid - 1
    for s in range(nb):
        sr = slice(s * tt, (s + 1) * tt)
        o = o_s[sr, :]
        mean = _xdot_r01(o, bd01) * (1.0 / HD)
        d = o - mean
        var = _xdot_r01(d * d, bd01) * (1.0 / HD)
        o = d * lax.rsqrt(var + RWKV_GN_EPS) * lnw_ref[...] + lnb_ref[...]
        o = o + rk_s[sr, :] * v_s[sr, :]
        y_ref[s] = (o * gt_s[sr, :]).astype(BF16)

        @pl.when(ti == last // tt)
        def _():
            shift_ref[s] = xext[s, pl.ds(8 + last % tt, 1), :]

        @pl.when(ti == n_tiles - 1)
        def _():
            _store_state(so_ref, st_ref, s)


def _rwkv_call(cd, prev8, s0, lw, *, nb, tt, c, t_valid):
    b, lp, _ = cd.shape
    consts = [lw["rwkv_mu"], lw["rwkv_w0"], lw["rwkv_wb"], lw["rwkv_a0"], lw["rwkv_ab"], lw["rwkv_gb"],
              lw["rwkv_kk"], lw["rwkv_ka"], lw["rwkv_rk"], lw["rwkv_lnw"], lw["rwkv_lnb"]]
    body = functools.partial(_rwkv_kernel, nb=nb, n_tiles=lp // tt, tt=tt, c=c, t_valid=t_valid)
    per_b = lambda s: pl.BlockSpec((nb,) + s, lambda i, j: (i,) + (0,) * len(s))
    n_ch = nb * (tt // c)
    return pl.pallas_call(
        body, grid=(b // nb, lp // tt),
        in_specs=[pl.BlockSpec((nb, tt, D_W), lambda i, j: (i, j, 0)), per_b((8, D_W)),
                  per_b((HEADS, HD, HD))] + [_const_spec(x.shape) for x in consts],
        out_specs=[pl.BlockSpec((nb, tt, MIXW), lambda i, j: (i, j, 0)),
                   per_b((1, D_W)), per_b((HEADS, HD, HD))],
        out_shape=[jax.ShapeDtypeStruct((b, lp, MIXW), BF16),
                   jax.ShapeDtypeStruct((b, 1, D_W), F32),
                   jax.ShapeDtypeStruct((b, HEADS, HD, HD), F32)],
        scratch_shapes=[pltpu.VMEM((nb, tt + 8, D_W), F32)] + [pltpu.VMEM((nb * tt, MIXW), F32)] * 8
        + [pltpu.VMEM((nb * tt, MIXW), BF16)] * 2 + [pltpu.VMEM((2 * nb * tt, MIXW), BF16),
                                                     pltpu.VMEM((-(-n_ch // 8) * 8, MIXW), F32),
                                                     pltpu.VMEM((n_ch, MIXW, MIXW), BF16),
                                                     pltpu.VMEM((n_ch, MIXW, MIXW), BF16),
                                                     pltpu.VMEM((n_ch, MIXW, MIXW), F32)]
        + [pltpu.VMEM((nb * tt, MIXW), F32)] * 2 + [pltpu.VMEM((nb, MIXW, MIXW), F32)],
        compiler_params=pltpu.CompilerParams(dimension_semantics=("parallel", "arbitrary"),
                                             vmem_limit_bytes=VMEM_LIMIT),
        name="rwkv7")(cd, prev8, s0, *consts)


def _pad_cols(w, width):
    return jnp.pad(w, ((0, 0), (0, width - w.shape[1])))


def _mla_head_cols(w, n_heads, per_head, pieces):
    blocks = []
    for h in range(n_heads):
        cols = [sign * w[:, h * per_head + lo:h * per_head + hi] for lo, hi, sign in pieces]
        blocks.append(_pad_cols(jnp.concatenate(cols, axis=1), MLA_PAD))
    return jnp.concatenate(blocks, axis=1)


def _block_diag_heads(w):
    eye = jnp.eye(HEADS, dtype=w.dtype)[:, None, :, None]
    return (w[:, :, None, :] * eye).reshape(MIXW, MIXW)


def _prep_layer_weights(p, l):
    half = MLA_ROPE // 2
    row = lambda v: v.reshape(1, -1).astype(F32)
    lw = {}
    for tag in ("1", "2"):
        lw["ffn" + tag] = dict(g_ffn=row(p["norm_ffn" + tag][l]), layer=l, w1=p["ffn%s_w1_bf16" % tag],
                               w2=p["ffn%s_w2_bf16" % tag])
    w_in = p["w_in"][l]
    o = 0
    cq = w_in[:, o:o + 256]; o += 256
    ckv = w_in[:, o:o + 128]; o += 128
    kpe = w_in[:, o:o + MLA_ROPE]; o += MLA_ROPE
    xb_gb = w_in[:, o:o + 512]; o += 512
    qkvz = w_in[:, o:o + 1024]; o += 1024
    a_in = w_in[:, o:o + HEADS]; o += HEADS
    b_in = w_in[:, o:o + HEADS]; o += HEADS
    rw = w_in[:, o:o + 1024]
    kpe_rot = jnp.concatenate([-kpe[:, half:], kpe[:, :half]], axis=1)
    w_in_p = jnp.concatenate([
        cq, ckv, _pad_cols(jnp.concatenate([kpe, kpe_rot], axis=1), 128),
        xb_gb,
        qkvz, jnp.repeat(a_in, HD, axis=1), jnp.repeat(b_in, HD, axis=1),
        rw], axis=1)
    lw["ffn1"]["g_mix"] = row(p["norm_mix"][l])
    lw["ffn1"]["w_in"] = w_in_p.astype(BF16)
    w_out = p["w_out"][l]
    wo_a = jnp.concatenate([jnp.pad(w_out[h * HD:(h + 1) * HD], ((0, MLA_PAD - HD), (0, 0)))
                            for h in range(HEADS)], axis=0)
    lw["w_out"] = [wo_a.astype(BF16)] + [w_out[MIXW * i:MIXW * (i + 1)].astype(BF16) for i in (1, 2, 3)]
    wuq = p["mla_w_uq"][l]
    wq = _mla_head_cols(wuq, HEADS, MLA_QK, [(MLA_NOPE, MLA_QK, 1.0), (0, MLA_NOPE, 1.0)])
    wq_rot = _mla_head_cols(wuq, HEADS, MLA_QK, [(MLA_NOPE + half, MLA_QK, -1.0),
                                                (MLA_NOPE, MLA_NOPE + half, 1.0)])
    lw["mla_wq"] = jnp.concatenate([wq, wq_rot], axis=1).astype(BF16)
    wukv = p["mla_w_ukv"][l]
    zero_rope = jnp.zeros((wukv.shape[0], MLA_ROPE), wukv.dtype)
    wk = jnp.concatenate([_pad_cols(jnp.concatenate([zero_rope, wukv[:, h * 128:h * 128 + MLA_NOPE]], axis=1),
                                    MLA_PAD) for h in range(HEADS)], axis=1)
    wv = jnp.concatenate([_pad_cols(wukv[:, h * 128 + MLA_NOPE:(h + 1) * 128], MLA_PAD)
                          for h in range(HEADS)], axis=1)
    lw["mla_wkv"] = jnp.concatenate([wk, wv], axis=1).astype(BF16)
    lw["mla_wv"] = wv.astype(BF16)
    lw["mla_wkt"] = jnp.concatenate([wukv[:, h * 128:h * 128 + MLA_NOPE] for h in range(HEADS)],
                                    axis=1).T.astype(BF16)
    lw["mla_gqa"] = row(p["mla_q_a_norm"][l])
    lw["mla_gkva"] = row(p["mla_kv_a_norm"][l])
    perm = lambda g: row(_pad_cols(jnp.concatenate([g[MLA_NOPE:], g[:MLA_NOPE]])[None], MLA_PAD))
    lw["mla_gq"] = perm(p["mla_q_norm"][l])
    lw["mla_gk"] = perm(p["mla_k_norm"][l])
    lw["lru_cw"] = p["lru_conv_w"][l]
    lw["lru_cb"] = row(p["lru_conv_b"][l])
    lw["lru_wa"] = _block_diag_heads(p["lru_wa"][l]).astype(BF16)
    lw["lru_ba"] = row(p["lru_ba"][l])
    lw["lru_wx"] = _block_diag_heads(p["lru_wx"][l]).astype(BF16)
    lw["lru_bx"] = row(p["lru_bx"][l])
    lw["lru_lam"] = row(p["lru_lambda"][l])
    lw["gdn_cw"] = p["gdn_conv_w"][l]
    lw["gdn_alog"] = row(jnp.repeat(p["gdn_a_log"][l], HD))
    lw["gdn_dtb"] = row(jnp.repeat(p["gdn_dt_bias"][l], HD))
    lw["gdn_gon"] = row(jnp.tile(p["gdn_o_norm"][l], HEADS))
    z64 = jnp.zeros((64, MIXW), F32)
    lw["rwkv_mu"] = row(p["rwkv_mu"][l])
    lw["rwkv_w0"] = row(p["rwkv_w0"][l])
    lw["rwkv_wb"] = jnp.concatenate([p["rwkv_w_b"][l], z64], axis=0).astype(BF16)
    lw["rwkv_a0"] = row(p["rwkv_a0"][l])
    lw["rwkv_ab"] = jnp.concatenate([z64, p["rwkv_a_b"][l]], axis=0).astype(BF16)
    lw["rwkv_gb"] = p["rwkv_g_b"][l].astype(BF16)
    lw["rwkv_kk"] = row(p["rwkv_k_k"][l])
    lw["rwkv_ka"] = row(p["rwkv_k_a"][l])
    lw["rwkv_rk"] = row(p["rwkv_r_k"][l])
    lw["rwkv_lnw"] = row(p["rwkv_ln_w"][l])
    lw["rwkv_lnb"] = row(p["rwkv_ln_b"][l])
    return lw


def _rope_tables(pos):
    inv = ROPE_THETA ** (-jnp.arange(0, MLA_ROPE, 2, dtype=F32) / MLA_ROPE)
    ang = pos.astype(F32)[:, None] * inv[None, :]
    cos, sin = jnp.cos(ang), jnp.sin(ang)
    n = pos.shape[0]
    cos_t = jnp.concatenate([cos, cos, jnp.ones((n, MLA_PAD - MLA_ROPE), F32)], axis=1)
    sin_t = jnp.concatenate([sin, sin, jnp.zeros((n, MLA_PAD - MLA_ROPE), F32)], axis=1)
    return cos_t, sin_t


def _hist8(rows):
    return jnp.pad(rows.astype(F32), ((0, 0), (8 - rows.shape[1], 0), (0, 0)))


def _run_group(x, lws, final_g, states, cfg):
    b, lp, d = x.shape
    n = b * lp
    t_valid, tm, tt, c = cfg["t_valid"], cfg["tm"], cfg["tt"], cfg["c"]
    cos_t, sin_t = cfg["rope"]
    xf = x.reshape(n, d)
    new_states = []
    depth = len(lws)
    for l in range(depth):
        lw = lws[l]
        st = states[l]
        xf, q, k, v, ckv, kpe, cb, cc, cd = _token_call(xf, lw["ffn1"], tm=tm,
                                                        post=(cos_t, sin_t, lw, (b, lp, t_valid)))
        r3 = lambda a: a.reshape(b, lp, a.shape[-1])
        if st["mla"] is None:
            ya = _mla_prompt_call(r3(q), r3(k), r3(v), tq=cfg["tq"], tk=cfg["tk"])
        else:
            cache_ckv, cache_kpe_t = st["mla"]
            ya = _mla_sample_call(r3(q), r3(k), r3(v), cache_ckv, cache_kpe_t, l, lw, nb=2 - b % 2)
        yb, lru_conv, lru_h = _lru_call(r3(cb), st["lru_conv"], st["lru_h"], lw, nb=cfg["nb"], tt=tt,
                                        t_valid=t_valid)
        yc, gdn_conv, gdn_s = _gdn_call(r3(cc), st["gdn_conv"], st["gdn_s"], lw, nb=cfg["nb"], tt=tt, c=c,
                                        t_valid=t_valid)
        yd, shift, rwkv_s = _rwkv_call(r3(cd), st["rwkv_shift"], st["rwkv_s"], lw, nb=cfg["nb"], tt=tt, c=c,
                                       t_valid=t_valid)
        ys = [ya.reshape(n, -1), yb.reshape(n, -1), yc.reshape(n, -1), yd.reshape(n, -1)]
        xf = _token_call(xf, lw["ffn2"], pre=(ys, lw["w_out"]), tm=tm,
                         final_g=final_g if l == depth - 1 else None)[0]
        new_states.append((ckv.reshape(b, t_valid, -1), kpe.reshape(b, t_valid, -1), lru_conv, lru_h[:, 0],
                           gdn_conv, gdn_s, shift[:, 0], rwkv_s))
    stacked = [jnp.stack(t) for t in zip(*new_states)]
    return xf.reshape(b, lp, d), stacked


def _group_config(b, t_valid):
    divisor = lambda target: max(d for d in range(1, target + 1) if b % d == 0)
    if t_valid <= CHUNK:
        lp = t_valid
        return dict(lp=lp, t_valid=t_valid, tm=b * lp if b * lp <= 512 else lp, tt=lp, c=lp, tq=lp, tk=1024,
                    nb=divisor(16))
    lp = -(-t_valid // 384) * 384
    return dict(lp=lp, t_valid=t_valid, tm=384, tt=384, c=CHUNK, tq=384, tk=384, nb=divisor(2))


def kernel(x_prompt, x_sample, cache_mla_ckv, cache_mla_kpe, state_lru_conv, state_lru_h, state_gdn_conv, state_gdn_s, state_rwkv_shift, state_rwkv_s, meta_tokens, norm_ffn1, ffn1_w1, ffn1_w2, norm_mix, w_in, mla_q_a_norm, mla_w_uq, mla_kv_a_norm, mla_w_ukv, mla_q_norm, mla_k_norm, lru_conv_w, lru_conv_b, lru_wa, lru_ba, lru_wx, lru_bx, lru_lambda, gdn_conv_w, gdn_a_log, gdn_dt_bias, gdn_o_norm, rwkv_mu, rwkv_w0, rwkv_w_b, rwkv_a0, rwkv_a_b, rwkv_g_b, rwkv_k_k, rwkv_k_a, rwkv_r_k, rwkv_ln_w, rwkv_ln_b, w_out, norm_ffn2, ffn2_w1, ffn2_w2, final_norm):
    p = dict(norm_ffn1=norm_ffn1, ffn1_w1=ffn1_w1, ffn1_w2=ffn1_w2, norm_mix=norm_mix, w_in=w_in,
             mla_q_a_norm=mla_q_a_norm, mla_w_uq=mla_w_uq, mla_kv_a_norm=mla_kv_a_norm,
             mla_w_ukv=mla_w_ukv, mla_q_norm=mla_q_norm, mla_k_norm=mla_k_norm,
             lru_conv_w=lru_conv_w, lru_conv_b=lru_conv_b, lru_wa=lru_wa, lru_ba=lru_ba,
             lru_wx=lru_wx, lru_bx=lru_bx, lru_lambda=lru_lambda, gdn_conv_w=gdn_conv_w,
             gdn_a_log=gdn_a_log, gdn_dt_bias=gdn_dt_bias, gdn_o_norm=gdn_o_norm, rwkv_mu=rwkv_mu,
             rwkv_w0=rwkv_w0, rwkv_w_b=rwkv_w_b, rwkv_a0=rwkv_a0, rwkv_a_b=rwkv_a_b, rwkv_g_b=rwkv_g_b,
             rwkv_k_k=rwkv_k_k, rwkv_k_a=rwkv_k_a, rwkv_r_k=rwkv_r_k, rwkv_ln_w=rwkv_ln_w,
             rwkv_ln_b=rwkv_ln_b, w_out=w_out, norm_ffn2=norm_ffn2, ffn2_w1=ffn2_w1, ffn2_w2=ffn2_w2)
    depth = w_in.shape[0]
    for name in ("ffn1_w1", "ffn1_w2", "ffn2_w1", "ffn2_w2"):
        p[name + "_bf16"] = p[name].astype(BF16)
    lws = [_prep_layer_weights(p, l) for l in range(depth)]
    final_g = final_norm.reshape(1, -1).astype(F32)
    d = x_prompt.shape[-1]

    bp, seq, _ = x_prompt.shape
    tp = N_META + seq
    cfg = _group_config(bp, tp)
    lp = cfg["lp"]
    x0 = jnp.concatenate([jnp.broadcast_to(meta_tokens.astype(F32)[None], (bp, N_META, d)), x_prompt,
                          jnp.zeros((bp, lp - tp, d), F32)], axis=1)
    cfg["rope"] = tuple(jnp.tile(t, (bp, 1)) for t in _rope_tables(jnp.arange(lp)))
    zero = dict(mla=None, lru_conv=jnp.zeros((bp, 8, MIXW), F32), lru_h=jnp.zeros((bp, 1, MIXW), F32),
                gdn_conv=jnp.zeros((bp, 8, 3 * MIXW), F32), gdn_s=jnp.zeros((bp, HEADS, HD, HD), F32),
                rwkv_shift=jnp.zeros((bp, 8, D_W), F32), rwkv_s=jnp.zeros((bp, HEADS, HD, HD), F32))
    yp, p_new = _run_group(x0, lws, final_g, [zero] * depth, cfg)

    bs, ts, _ = x_sample.shape
    past = cache_mla_ckv.shape[2]
    cfg_s = _group_config(bs, ts)
    cfg_s["rope"] = tuple(jnp.tile(t, (bs, 1)) for t in _rope_tables(past + jnp.arange(ts)))
    cache_kpe_t = jnp.swapaxes(cache_mla_kpe, 2, 3)
    st_s = [dict(mla=(cache_mla_ckv, cache_kpe_t), lru_conv=_hist8(state_lru_conv[l]),
                 lru_h=state_lru_h[l][:, None].astype(F32), gdn_conv=_hist8(state_gdn_conv[l]),
                 gdn_s=state_gdn_s[l], rwkv_shift=_hist8(state_rwkv_shift[l][:, None]),
                 rwkv_s=state_rwkv_s[l]) for l in range(depth)]
    ys, s_new = _run_group(x_sample, lws, final_g, st_s, cfg_s)
    return (yp[:, N_META:tp], ys) + tuple(p_new) + tuple(s_new)
```
